```python
import jax, jax.numpy as jnp
from jax import lax
import numpy as np

D_MODEL = 1024
BATCH = 1
SEQ = 16384
DEPTH = 2
DEC_BATCH = 32
DEC_SEQ = 4
PAST_LEN = 16384
PAGE_SIZE = 128

HEAD_DIM = 64
CONV_MIX = D_MODEL // 4
NSA_HEADS = (D_MODEL // 2) // HEAD_DIM
NSA_KV_HEADS = 2
NSA_GQA = NSA_HEADS // NSA_KV_HEADS
NSA_Q = NSA_HEADS * HEAD_DIM
NSA_KV = NSA_KV_HEADS * HEAD_DIM
N_KV_SLOTS = 6
N_PAGED_SLOTS = 4
N_BRANCH = 3
GMLP_W = D_MODEL // 4
GMLP_GROUPS = GMLP_W // HEAD_DIM
GMLP_CHUNK = 128
MIX_WIDTH = CONV_MIX + NSA_Q + GMLP_W
CONV_K = 3
CMP_LEN = 32
CMP_STRIDE = 16
CMP_HID = 128
SLC_BLOCK = 64
N_SELECT = 16
WINDOW = 512
Q_BLOCK = 128
D_FF = 2816
ALPHA = (2 * DEPTH) ** 0.25
BETA = (8 * DEPTH) ** -0.25
LN_EPS = 1e-5
NEG_INF = -1e30
FORCE = 1e4
IN_SIZES = (CONV_MIX, CONV_MIX, CONV_MIX, NSA_Q, N_KV_SLOTS * NSA_KV, NSA_HEADS * N_BRANCH, GMLP_W, GMLP_W)
N_IN = 3 * CONV_MIX + NSA_Q + N_KV_SLOTS * NSA_KV + NSA_HEADS * N_BRANCH + 2 * GMLP_W

kernel_name = 'hybrid_conv_nsa_sgu_convffn_step'


def layer_norm(x, g, b):
    xf = x.astype(jnp.float32)
    mu = jnp.mean(xf, axis=-1, keepdims=True)
    var = jnp.mean(jnp.square(xf - mu), axis=-1, keepdims=True)
    return ((xf - mu) * lax.rsqrt(var + LN_EPS)).astype(x.dtype) * g + b


def causal_dwconv(x, past, w, b):
    xp = jnp.concatenate([past.astype(x.dtype), x], axis=1)
    T = x.shape[1]
    y = sum((xp[:, j:j + T] * w[j] for j in range(CONV_K)), b)
    return y, xp[:, T:]


def compress(k_raw, pe, w1, w2):
    B, L, G, Dh = k_raw.shape
    n_ch = L // CMP_STRIDE
    ch = k_raw[:, :n_ch * CMP_STRIDE].reshape(B, n_ch, CMP_STRIDE, G, Dh)
    w1h = w1.reshape(2, CMP_STRIDE, Dh, CMP_HID)
    part = jnp.einsum('bcjgd,hjde->hbcge', ch, w1h)
    pe_term = jnp.einsum('jd,jde->e', pe, w1)
    hid = jax.nn.gelu(part[0, :, :-1] + part[1, :, 1:] + pe_term)
    return jnp.einsum('bnge,ed->bngd', hid, w2)


def block_cover(n_cmp, n_slc):
    start = jnp.arange(n_cmp)[:, None] * CMP_STRIDE
    blk_start = jnp.arange(n_slc)[None, :] * SLC_BLOCK
    return ((start <= blk_start + SLC_BLOCK - 1) & (start + CMP_LEN - 1 >= blk_start)).astype(jnp.float32)


def nsa_attend(q, gate, qpos, kc, vc, cover, ks, vs, kw, vw, kwpos):
    B, Tq = q.shape[:2]
    G, R = NSA_KV_HEADS, NSA_GQA
    f32 = jnp.float32
    qg = q.reshape(B, Tq, G, R, HEAD_DIM).astype(f32) * HEAD_DIM ** -0.5
    n_cmp = kc.shape[1]
    cmask = (jnp.arange(n_cmp) * CMP_STRIDE + CMP_LEN - 1)[None, :] <= qpos[:, None]
    s = jnp.einsum('bqgrd,bngd->bgrqn', qg, kc.astype(f32))
    p_cmp = jax.nn.softmax(jnp.where(cmask, s, NEG_INF), axis=-1) * cmask
    o_cmp = jnp.einsum('bgrqn,bngd->bqgrd', p_cmp, vc.astype(f32))
    n_slc = ks.shape[2]
    blk = jnp.arange(n_slc)[None, :]
    qblk = (qpos // SLC_BLOCK)[:, None]
    forced = (blk == 0) | (blk == qblk) | (blk == qblk - 1)
    imp = jnp.einsum('bgrqn,nm->bgqm', p_cmp, cover)
    imp = jnp.where(blk * SLC_BLOCK <= qpos[:, None], jnp.where(forced, FORCE, imp), -FORCE)
    top_val, top_idx = lax.top_k(imp, min(N_SELECT, n_slc))
    bi = jnp.arange(B)[:, None, None, None]
    gi = jnp.arange(G)[None, :, None, None]
    k_sel = ks[bi, gi, top_idx].astype(f32)
    v_sel = vs[bi, gi, top_idx].astype(f32)
    kpos = top_idx[..., None] * SLC_BLOCK + jnp.arange(SLC_BLOCK)
    smask = (kpos <= qpos[:, None, None]) & (top_val > -0.5 * FORCE)[..., None]
    s = jnp.einsum('bqgrd,bgqkjd->bgrqkj', qg, k_sel)
    s = jnp.where(smask[:, :, None], s, NEG_INF)
    p = jax.nn.softmax(s.reshape(B, G, R, Tq, -1), axis=-1).reshape(s.shape)
    o_slc = jnp.einsum('bgrqkj,bgqkjd->bqgrd', p, v_sel)
    dist = qpos[:, None] - kwpos[None, :]
    wmask = (dist >= 0) & (dist <= WINDOW) & (kwpos >= 0)[None, :]
    s = jnp.einsum('bqgrd,bkgd->bgrqk', qg, kw.astype(f32))
    p = jax.nn.softmax(jnp.where(wmask, s, NEG_INF), axis=-1)
    o_win = jnp.einsum('bgrqk,bkgd->bqgrd', p, vw.astype(f32))
    g = gate.reshape(B, Tq, G, R, N_BRANCH).astype(f32)
    o = g[..., 0:1] * o_cmp + g[..., 1:2] * o_slc + g[..., 2:3] * o_win
    return o.reshape(B, Tq, NSA_Q)


def nsa_mixer(q, kv, gate, lp, past):
    B, T = q.shape[:2]
    G = NSA_KV_HEADS
    new_paged = kv[:, :, :N_PAGED_SLOTS]
    new_win = kv[:, :, N_PAGED_SLOTS:]
    if past is None:
        full = new_paged
        win = jnp.pad(new_win, ((0, 0), (WINDOW, 0), (0, 0), (0, 0), (0, 0)))
    else:
        full = jnp.concatenate([past['nsa_kv'].astype(kv.dtype), new_paged], axis=1)
        win = jnp.concatenate([past['win_kv'].astype(kv.dtype), new_win], axis=1)
    L = full.shape[1]
    kc = compress(full[:, :, 0], lp['cmp_pe'][0], lp['cmp_w1'][0], lp['cmp_w2'][0])
    vc = compress(full[:, :, 1], lp['cmp_pe'][1], lp['cmp_w1'][1], lp['cmp_w2'][1])
    n_slc = -(-L // SLC_BLOCK)
    slc = jnp.pad(full[:, :, 2:4], ((0, 0), (0, n_slc * SLC_BLOCK - L), (0, 0), (0, 0), (0, 0)))
    slc = jnp.transpose(slc.reshape(B, n_slc, SLC_BLOCK, 2, G, HEAD_DIM), (3, 0, 4, 1, 2, 5))
    cover = block_cover(kc.shape[1], n_slc)
    if past is None:
        n_win = WINDOW + Q_BLOCK

        def one_block(i):
            s0 = i * Q_BLOCK
            qb = lax.dynamic_slice_in_dim(q, s0, Q_BLOCK, axis=1)
            gb = lax.dynamic_slice_in_dim(gate, s0, Q_BLOCK, axis=1)
            wb = lax.dynamic_slice_in_dim(win, s0, n_win, axis=1)
            return nsa_attend(qb, gb, s0 + jnp.arange(Q_BLOCK), kc, vc, cover, slc[0], slc[1],
                              wb[:, :, 0], wb[:, :, 1], s0 - WINDOW + jnp.arange(n_win))

        o = lax.map(one_block, jnp.arange(T // Q_BLOCK))
        o = jnp.moveaxis(o, 0, 1).reshape(B, T, NSA_Q)
        rows = new_paged.reshape(B, T // PAGE_SIZE, PAGE_SIZE, N_PAGED_SLOTS, G, HEAD_DIM)
        win_state = new_win[:, max(T - WINDOW, 0):]
    else:
        n_buf = past['win_kv'].shape[1]
        o = nsa_attend(q, gate, PAST_LEN + jnp.arange(T), kc, vc, cover, slc[0], slc[1],
                       win[:, :, 0], win[:, :, 1], PAST_LEN - n_buf + jnp.arange(n_buf + T))
        rows = new_paged
        win_state = win[:, T:]
    return o.astype(q.dtype), rows, win_state


def spatial_gating(u, v, lp):
    B, T, _ = v.shape
    v = layer_norm(v, lp['sgu_ln_g'], lp['sgu_ln_b'])
    n = min(T, GMLP_CHUNK)
    v5 = v.reshape(B, T // n, n, GMLP_GROUPS, HEAD_DIM)
    w = jnp.where(jnp.tril(jnp.ones((n, n), dtype=bool)), lp['sgu_w'][:, :n, :n], 0)
    mixed = jnp.einsum('gts,bcsgd->bctgd', w, v5) + lp['sgu_b'][:, :n].T[None, None, :, :, None]
    return u * mixed.reshape(B, T, GMLP_W), v


def decoder_layer(x, c, lp, past):
    B, T, _ = x.shape
    mod = (jax.nn.silu(c) @ lp['w_ada'] + lp['b_ada'])[:, None, :]
    sh1, sc1, g1, sh2, sc2, g2 = jnp.split(mod, 6, axis=-1)
    h = x * (1 + sc1) + sh1
    splits = np.cumsum(IN_SIZES)[:-1].tolist()
    a_b, a_c, a_h, q, kv, gate, gu, gv = jnp.split(h @ lp['w_in'], splits, axis=-1)
    conv_past = jnp.zeros((B, CONV_K - 1, CONV_MIX), x.dtype) if past is None else past['conv']
    y_a, conv_state = causal_dwconv(a_c * a_h, conv_past, lp['conv_a_w'], lp['conv_a_b'])
    o_a = a_b * y_a
    o_b, nsa_rows, win_state = nsa_mixer(
        q.reshape(B, T, NSA_HEADS, HEAD_DIM),
        kv.reshape(B, T, N_KV_SLOTS, NSA_KV_HEADS, HEAD_DIM),
        jax.nn.sigmoid(gate).reshape(B, T, NSA_HEADS, N_BRANCH), lp, past)
    o_c, v_rows = spatial_gating(jax.nn.gelu(gu), jax.nn.gelu(gv), lp)
    mix = jnp.concatenate([o_a, o_b, o_c], axis=-1) @ lp['w_o']
    x = layer_norm(ALPHA * x + (1 + g1) * mix, lp['ln_g'][0], lp['ln_b'][0])
    h = x * (1 + sc2) + sh2
    up_a, up_g = jnp.split(h @ lp['w_ffn_up'], 2, axis=-1)
    ffn_past = jnp.zeros((B, CONV_K - 1, D_FF), x.dtype) if past is None else past['ffn_conv']
    up_a, ffn_state = causal_dwconv(up_a, ffn_past, lp['conv_f_w'], lp['conv_f_b'])
    y = (jax.nn.silu(up_a) * up_g) @ lp['w_ffn_down']
    x = layer_norm(ALPHA * x + (1 + g2) * y, lp['ln_g'][1], lp['ln_b'][1])
    return x, (nsa_rows, win_state, conv_state, ffn_state, v_rows)


def setup_inputs(seed: int = 0) -> dict:
    key = jax.random.key(seed)
    k = jax.random.split(key, 28)
    f32 = jnp.float32

    def nrm(i, shape, scale):
        return jax.random.normal(k[i], shape, f32) * scale

    n_pages = PAST_LEN // PAGE_SIZE
    n_used = DEC_BATCH * n_pages
    n_phys = n_used + max(1, n_used // 4)
    win_buf = min(WINDOW, PAST_LEN)
    page_table = jax.random.permutation(k[0], n_phys)[:n_used].reshape(DEC_BATCH, n_pages).astype(jnp.int32)
    return {
        'x_prompt': nrm(1, (BATCH, SEQ, D_MODEL), 1.0),
        'x_sample': nrm(2, (DEC_BATCH, DEC_SEQ, D_MODEL), 1.0),
        'cache_nsa_kv': nrm(3, (DEPTH, n_phys, PAGE_SIZE, N_PAGED_SLOTS, NSA_KV_HEADS, HEAD_DIM), 1.0),
        'state_win_kv': nrm(4, (DEPTH, DEC_BATCH, win_buf, 2, NSA_KV_HEADS, HEAD_DIM), 1.0),
        'state_conv': nrm(5, (DEPTH, DEC_BATCH, CONV_K - 1, CONV_MIX), 1.0),
        'state_ffn_conv': nrm(6, (DEPTH, DEC_BATCH, CONV_K - 1, D_FF), 1.0),
        'page_table': page_table,
        'c_prompt': nrm(7, (BATCH, D_MODEL), 1.0),
        'c_sample': nrm(8, (DEC_BATCH, D_MODEL), 1.0),
        'w_ada': nrm(9, (DEPTH, D_MODEL, 6 * D_MODEL), 0.1 * D_MODEL ** -0.5),
        'b_ada': nrm(10, (DEPTH, 6 * D_MODEL), 0.01),
        'w_in': nrm(11, (DEPTH, D_MODEL, N_IN), D_MODEL ** -0.5),
        'conv_a_w': nrm(12, (DEPTH, CONV_K, CONV_MIX), CONV_K ** -0.5),
        'conv_a_b': nrm(13, (DEPTH, CONV_MIX), 0.01),
        'cmp_pe': nrm(14, (DEPTH, 2, CMP_LEN, HEAD_DIM), 0.1),
        'cmp_w1': nrm(15, (DEPTH, 2, CMP_LEN, HEAD_DIM, CMP_HID), (CMP_LEN * HEAD_DIM) ** -0.5),
        'cmp_w2': nrm(16, (DEPTH, 2, CMP_HID, HEAD_DIM), CMP_HID ** -0.5),
        'sgu_ln_g': 1.0 + nrm(17, (DEPTH, GMLP_W), 0.02),
        'sgu_ln_b': nrm(18, (DEPTH, GMLP_W), 0.02),
        'sgu_w': nrm(19, (DEPTH, GMLP_GROUPS, GMLP_CHUNK, GMLP_CHUNK), 0.5 * GMLP_CHUNK ** -0.5),
        'sgu_b': 1.0 + nrm(20, (DEPTH, GMLP_GROUPS, GMLP_CHUNK), 0.1),
        'w_o': nrm(21, (DEPTH, MIX_WIDTH, D_MODEL), BETA * MIX_WIDTH ** -0.5),
        'ln_g': 1.0 + nrm(22, (DEPTH, 2, D_MODEL), 0.02),
        'ln_b': nrm(23, (DEPTH, 2, D_MODEL), 0.02),
        'w_ffn_up': nrm(24, (DEPTH, D_MODEL, 2 * D_FF), D_MODEL ** -0.5),
        'conv_f_w': nrm(25, (DEPTH, CONV_K, D_FF), CONV_K ** -0.5),
        'conv_f_b': nrm(26, (DEPTH, D_FF), 0.01),
        'w_ffn_down': nrm(27, (DEPTH, D_FF, D_MODEL), BETA * D_FF ** -0.5),
    }


def reference(x_prompt, x_sample, cache_nsa_kv, state_win_kv, state_conv, state_ffn_conv, page_table,
              c_prompt, c_sample, w_ada, b_ada, w_in, conv_a_w, conv_a_b, cmp_pe, cmp_w1, cmp_w2,
              sgu_ln_g, sgu_ln_b, sgu_w, sgu_b, w_o, ln_g, ln_b, w_ffn_up, conv_f_w, conv_f_b, w_ffn_down):
    xp, xs = x_prompt, x_sample
    ps, ss = [], []
    for l in range(DEPTH):
        lp = {'w_ada': w_ada[l], 'b_ada': b_ada[l], 'w_in': w_in[l], 'conv_a_w': conv_a_w[l],
              'conv_a_b': conv_a_b[l], 'cmp_pe': cmp_pe[l], 'cmp_w1': cmp_w1[l], 'cmp_w2': cmp_w2[l],
              'sgu_ln_g': sgu_ln_g[l], 'sgu_ln_b': sgu_ln_b[l], 'sgu_w': sgu_w[l], 'sgu_b': sgu_b[l],
              'w_o': w_o[l], 'ln_g': ln_g[l], 'ln_b': ln_b[l], 'w_ffn_up': w_ffn_up[l],
              'conv_f_w': conv_f_w[l], 'conv_f_b': conv_f_b[l], 'w_ffn_down': w_ffn_down[l]}
        xp, st = decoder_layer(xp, c_prompt, lp, None)
        ps.append(st)
        past_kv = cache_nsa_kv[l][page_table]
        past = {'nsa_kv': past_kv.reshape(past_kv.shape[0], -1, N_PAGED_SLOTS, NSA_KV_HEADS, HEAD_DIM),
                'win_kv': state_win_kv[l], 'conv': state_conv[l], 'ffn_conv': state_ffn_conv[l]}
        xs, st = decoder_layer(xs, c_sample, lp, past)
        ss.append(st)
    return (xp, xs,
            jnp.stack([s[0] for s in ps]), jnp.stack([s[1] for s in ps]),
            jnp.stack([s[2] for s in ps]), jnp.stack([s[3] for s in ps]),
            jnp.stack([s[0] for s in ss]), jnp.stack([s[1] for s in ss]),
            jnp.stack([s[2] for s in ss]), jnp.stack([s[3] for s in ss]),
            jnp.stack([s[4] for s in ss]))
```

```python
import functools
import math

import numpy as np
import jax
import jax.numpy as jnp
from jax import lax
from jax.experimental import pallas as pl
from jax.experimental.pallas import tpu as pltpu

F32 = jnp.float32
BF16 = jnp.bfloat16

HEAD_DIM = 64
N_HEADS = 8
N_KV_HEADS = 2
GQA = N_HEADS // N_KV_HEADS
N_BRANCH = 3
CONV_K = 3
CMP_LEN = 32
CMP_STRIDE = 16
CMP_HID = 128
SLC_BLOCK = 64
N_SELECT = 16
WINDOW = 512
Q_BLOCK = 128
PAGE_SIZE = 128
GMLP_CHUNK = 128
GMLP_GROUPS = 4
LN_EPS = 1e-5
NEG_INF = -1e30
FORCE = 1e4
REMOVED = -3e38
LOG2E = 1.4426950408889634

LANES = 128
KEY_TILE = 512
BLOCKS_PER_TILE = KEY_TILE // SLC_BLOCK
VMEM_LIMIT = 56 * 1024 * 1024


def _dot(a, b):
    return jnp.dot(a, b, preferred_element_type=F32)


def _dot_nt(a, b):
    return lax.dot_general(a, b, (((1,), (1,)), ((), ())), preferred_element_type=F32)


def _split(a):
    hi = a.astype(BF16)
    lo = (a - hi.astype(F32)).astype(BF16)
    return hi, lo


def _dot3(a, b):
    ah, al = _split(a)
    bh, bl = _split(b)
    return _dot(ah, bh) + _dot(ah, bl) + _dot(al, bh)


def _sigmoid(x):
    return 1.0 / (1.0 + jnp.exp(-x))


def _gelu(x):
    c = math.sqrt(2.0 / math.pi)
    return 0.5 * x * (1.0 + jnp.tanh(c * (x + 0.044715 * (x * x * x))))


def _layer_norm(x, g, b):
    mu = jnp.mean(x, axis=-1, keepdims=True)
    xc = x - mu
    var = jnp.mean(xc * xc, axis=-1, keepdims=True)
    return xc * lax.rsqrt(var + LN_EPS) * g + b


def _masked_softmax(s, mask):
    sm = jnp.where(mask, s, NEG_INF)
    mx = jnp.max(sm, axis=-1, keepdims=True)
    e = jnp.where(mask, jnp.exp2(s - mx), 0.0)
    l = jnp.sum(e, axis=-1, keepdims=True)
    return e * (1.0 / jnp.where(l > 0.0, l, 1.0))


def _top_select(val, blk_f, n_lanes):
    sel = jnp.zeros_like(val)
    firsts, tops = [], []
    for _ in range(N_SELECT):
        mx = jnp.max(val, axis=-1, keepdims=True)
        first = jnp.min(jnp.where(val == mx, blk_f, float(n_lanes)), axis=-1, keepdims=True)
        hit = blk_f == first
        sel = jnp.where(hit, 1.0, sel)
        val = jnp.where(hit, REMOVED, val)
        firsts.append(first)
        tops.append(mx)
    return sel, firsts, tops


def _shifted_conv(src_ref, x, w_ref, b_ref, pad, shift, rows):
    x2 = src_ref[pl.ds(pad - 2 * shift, rows), :]
    x1 = src_ref[pl.ds(pad - shift, rows), :]
    return w_ref[0:1, :] * x2 + w_ref[1:2, :] * x1 + w_ref[2:3, :] * x + b_ref[...]


def _ada_kernel(c_ref, w_ref, b_ref, o_ref):
    c = c_ref[...]
    o_ref[0] = _dot3(c * _sigmoid(c), w_ref[0]) + b_ref[0]


def _ada_call(c_all, w_ada, b_ada):
    depth, d_model, n_mod = w_ada.shape
    rc = c_all.shape[0]
    tn = 1024
    return pl.pallas_call(
        _ada_kernel,
        grid=(depth, n_mod // tn),
        in_specs=[pl.BlockSpec((rc, d_model), lambda l, n: (0, 0)),
                  pl.BlockSpec((1, d_model, tn), lambda l, n: (l, 0, n)),
                  pl.BlockSpec((1, 1, tn), lambda l, n: (l, 0, n))],
        out_specs=pl.BlockSpec((1, rc, tn), lambda l, n: (l, 0, n)),
        out_shape=jax.ShapeDtypeStruct((depth, rc, n_mod), F32),
        compiler_params=pltpu.CompilerParams(dimension_semantics=("arbitrary", "arbitrary"),
                                             vmem_limit_bytes=VMEM_LIMIT),
        name="ada_mod",
    )(c_all, w_ada, b_ada.reshape(depth, 1, n_mod))


_C_AB, _C_AC, _C_AH = 0, 256, 512
_C_Q = 768
_C_KV = 1792
_C_GATE = 2560
_C_GU = 2688
_C_GV = 2944
_N_COL = 3200


def _in_proj_kernel(x_ref, sc_ref, sh_ref, w_ref, cw_ref, cb_ref, cpast_ref, lng_ref, lnb_ref, wm_ref, sb_ref,
                    oa_ref, oc_ref, q_ref, kvf_ref, kcr_ref, vcr_ref, ksl_ref, vsl_ref, kwn_ref, vwn_ref,
                    gate_ref, vrow_ref, cstate_ref, zs_ref, *, shift, tm, pad):
    i = pl.program_id(0)

    @pl.when(i == 0)
    def _():
        zs_ref[pl.ds(pad - 2 * shift, 2 * shift), :] = cpast_ref[...]

    h = (x_ref[...] * (1.0 + sc_ref[...]) + sh_ref[...]).astype(BF16)
    p = _dot(h, w_ref[...])

    z = p[:, _C_AC:_C_AC + 256] * p[:, _C_AH:_C_AH + 256]
    zs_ref[pl.ds(pad, tm), :] = z
    y = _shifted_conv(zs_ref, z, cw_ref, cb_ref, pad, shift, tm)
    oa_ref[...] = (p[:, _C_AB:_C_AB + 256] * y).astype(BF16)
    tail = zs_ref[pl.ds(pad + tm - 2 * shift, 2 * shift), :]
    cstate_ref[...] = tail
    zs_ref[pl.ds(pad - 2 * shift, 2 * shift), :] = tail

    q_ref[...] = p[:, _C_Q:_C_Q + 1024].astype(BF16)
    kv = p[:, _C_KV:_C_KV + 768]
    kvf_ref[...] = kv
    for k, ref in enumerate((kcr_ref, vcr_ref, ksl_ref, vsl_ref, kwn_ref, vwn_ref)):
        ref[...] = kv[:, k * LANES:(k + 1) * LANES].astype(BF16)
    gate_ref[...] = _sigmoid(p[:, _C_GATE:_C_GATE + LANES])

    u = _gelu(p[:, _C_GU:_C_GU + 256])
    v = _layer_norm(_gelu(p[:, _C_GV:_C_GV + 256]), lng_ref[...], lnb_ref[...])
    vrow_ref[...] = v
    lane = lax.broadcasted_iota(jnp.int32, (1, 256), 1)
    for c in range(tm // GMLP_CHUNK):
        vc = v[c * GMLP_CHUNK:(c + 1) * GMLP_CHUNK]
        mixed = sb_ref[...]
        for g in range(GMLP_GROUPS):
            vg = jnp.where((lane >= g * HEAD_DIM) & (lane < (g + 1) * HEAD_DIM), vc, 0.0).astype(BF16)
            mixed = mixed + _dot(wm_ref[g], vg)
        oc_ref[pl.ds(c * GMLP_CHUNK, GMLP_CHUNK), :] = (u[c * GMLP_CHUNK:(c + 1) * GMLP_CHUNK] * mixed).astype(BF16)


def _in_proj_call(x, sc, sh, w_all, cw, cb, cpast, lng, lnb, wm, sb, *, shift, tm, name):
    rows, d_model = x.shape
    pad = max(8, 2 * shift)
    mr = sc.shape[0]
    mod_spec = (pl.BlockSpec((1, d_model), lambda i: (0, 0)) if mr == 1
                else pl.BlockSpec((tm, d_model), lambda i: (i, 0)))

    def row_spec(n):
        return pl.BlockSpec((tm, n), lambda i: (i, 0))

    def full(a):
        nd = a.ndim
        return pl.BlockSpec(a.shape, lambda i: (0,) * nd)

    out_shape = ([jax.ShapeDtypeStruct((rows, 256), BF16), jax.ShapeDtypeStruct((rows, 256), BF16),
                  jax.ShapeDtypeStruct((rows, 1024), BF16), jax.ShapeDtypeStruct((rows, 768), F32)]
                 + [jax.ShapeDtypeStruct((rows, LANES), BF16)] * 6
                 + [jax.ShapeDtypeStruct((rows, LANES), F32), jax.ShapeDtypeStruct((rows, 256), F32),
                    jax.ShapeDtypeStruct((2 * shift, 256), F32)])
    out_specs = ([row_spec(256), row_spec(256), row_spec(1024), row_spec(768)] + [row_spec(LANES)] * 6
                 + [row_spec(LANES), row_spec(256), pl.BlockSpec((2 * shift, 256), lambda i: (0, 0))])
    return pl.pallas_call(
        functools.partial(_in_proj_kernel, shift=shift, tm=tm, pad=pad),
        grid=(rows // tm,),
        in_specs=[row_spec(d_model), mod_spec, mod_spec, full(w_all), full(cw), full(cb), full(cpast),
                  full(lng), full(lnb), full(wm), full(sb)],
        out_specs=out_specs,
        out_shape=out_shape,
        scratch_shapes=[pltpu.VMEM((pad + tm, 256), F32)],
        compiler_params=pltpu.CompilerParams(dimension_semantics=("arbitrary",), vmem_limit_bytes=VMEM_LIMIT),
        name=name,
    )(x, sc, sh, w_all, cw, cb, cpast, lng, lnb, wm, sb)


def _pe_term_kernel(pe_ref, w1_ref, o_ref):
    for s in range(2):
        t = _dot3(pe_ref[s], w1_ref[s])
        o_ref[s] = t


def _pe_term_call(pe_flat, w1_flat):
    return pl.pallas_call(
        _pe_term_kernel,
        out_shape=jax.ShapeDtypeStruct((2, 8, CMP_HID), F32),
        compiler_params=pltpu.CompilerParams(vmem_limit_bytes=VMEM_LIMIT),
        name="cmp_pe_term",
    )(pe_flat, w1_flat)


def _compress_prompt_kernel(kx_ref, vx_ref, w1_ref, pet_ref, w2_ref, kc_ref, vc_ref, sh_ref, *, nc):
    sh_ref[pl.ds(0, 8), :] = jnp.zeros((8, CMP_HID), F32)
    for s, (x_ref, o_ref) in enumerate(((kx_ref, kc_ref), (vx_ref, vc_ref))):
        parts = _dot(x_ref[...], w1_ref[s])
        acc = jnp.zeros((nc, LANES), F32)
        for g in range(N_KV_HEADS):
            p0 = parts[:, g * 256:g * 256 + CMP_HID]
            p1 = parts[:, g * 256 + CMP_HID:(g + 1) * 256]
            sh_ref[pl.ds(8, nc), :] = p0
            p0s = sh_ref[pl.ds(7, nc), :]
            hid = _gelu(p0s + p1 + pet_ref[s][0:1, :])
            acc = acc + _dot(hid.astype(BF16), w2_ref[s, g])
        o_ref[...] = acc.astype(BF16)


def _compress_prompt_call(kx, vx, w1p, pet, w2p):
    nc = kx.shape[0]
    return pl.pallas_call(
        functools.partial(_compress_prompt_kernel, nc=nc),
        out_shape=[jax.ShapeDtypeStruct((nc, LANES), BF16)] * 2,
        scratch_shapes=[pltpu.VMEM((nc + 8, CMP_HID), F32)],
        compiler_params=pltpu.CompilerParams(vmem_limit_bytes=VMEM_LIMIT),
        name="compress_prompt",
    )(kx, vx, w1p, pet, w2p)


def _attn_prompt_kernel(q_ref, gate_ref, kc_ref, vc_ref, cover_ref, ksl_ref, vsl_ref, kwn_ref, vwn_ref, eb_ref,
                        o_ref, *, nc, ns):
    qb = pl.program_id(0)
    s0 = qb * Q_BLOCK
    nrow = GQA * Q_BLOCK
    rows = lax.broadcasted_iota(jnp.int32, (nrow, 1), 0)
    qpos_r = s0 + (rows & (Q_BLOCK - 1))
    qp = s0 + lax.broadcasted_iota(jnp.int32, (Q_BLOCK, 1), 0)
    lane = lax.broadcasted_iota(jnp.int32, (1, LANES), 1)
    gate = gate_ref[...]
    m_idx = lax.broadcasted_iota(jnp.int32, (1, nc), 1)
    blk = lax.broadcasted_iota(jnp.int32, (1, ns), 1)
    blk_f = blk.astype(F32)
    blk_col = lax.broadcasted_iota(jnp.int32, (ns, 1), 0)
    key_col = lax.broadcasted_iota(jnp.int32, (1, KEY_TILE), 1)
    n_tiles = s0 // KEY_TILE + 1

    for g in range(N_KV_HEADS):
        qg = jnp.concatenate([q_ref[:, (GQA * g + h) * LANES:(GQA * g + h + 1) * LANES] for h in range(GQA)], axis=0)

        sc = _dot_nt(qg, kc_ref[...])
        cmask = (m_idx >= 1) & (CMP_STRIDE * (m_idx - 1) + CMP_LEN - 1 <= qpos_r)
        p = _masked_softmax(sc, cmask)
        o_cmp = _dot(p.astype(BF16), vc_ref[...])

        p4 = p[0:Q_BLOCK] + p[Q_BLOCK:2 * Q_BLOCK] + p[2 * Q_BLOCK:3 * Q_BLOCK] + p[3 * Q_BLOCK:4 * Q_BLOCK]
        hi, lo = _split(p4)
        imp = _dot(hi, cover_ref[...]) + _dot(lo, cover_ref[...])
        qblk = qp // SLC_BLOCK
        elig = blk * SLC_BLOCK <= qp
        forced = (blk == 0) | (blk == qblk) | (blk == qblk - 1)
        val = jnp.where(elig, jnp.where(forced, FORCE, imp), -FORCE)
        sel, _, _ = _top_select(val, blk_f, ns)
        notsel = jnp.where(elig & (sel > 0.5), 0.0, 1.0).astype(BF16)

        def tile_step(kt, carry):
            m_run, l_run, acc = carry
            k0 = pl.multiple_of(kt * KEY_TILE, KEY_TILE)
            k_t = ksl_ref[pl.ds(k0, KEY_TILE), :]
            v_t = vsl_ref[pl.ds(k0, KEY_TILE), :]
            pick = jnp.where((blk_col == kt * BLOCKS_PER_TILE + lane) & (lane < BLOCKS_PER_TILE), 1.0, 0.0)
            ns_t = _dot(notsel, pick.astype(BF16)).astype(BF16)
            lhs = jnp.concatenate([qg, jnp.concatenate([ns_t] * GQA, axis=0)], axis=1)
            rhs = jnp.concatenate([k_t, eb_ref[...]], axis=1)
            s = _dot_nt(lhs, rhs)
            s = jnp.where(k0 + key_col <= qpos_r, s, NEG_INF)
            m_new = jnp.maximum(m_run, jnp.max(s, axis=-1, keepdims=True))
            alpha = jnp.exp2(m_run - m_new)
            pe = jnp.exp2(s - m_new)
            l_new = alpha * l_run + jnp.sum(pe, axis=-1, keepdims=True)
            acc_new = alpha * acc + _dot(pe.astype(BF16), v_t)
            return m_new, l_new, acc_new

        init = (jnp.full((nrow, 1), NEG_INF, F32), jnp.zeros((nrow, 1), F32), jnp.zeros((nrow, LANES), F32))
        _, l_fin, acc = lax.fori_loop(0, n_tiles, tile_step, init)
        o_slc = acc * (1.0 / l_fin)

        w0 = pl.multiple_of(jnp.maximum(s0 - WINDOW, 0), Q_BLOCK)
        k_w = kwn_ref[pl.ds(w0, WINDOW + Q_BLOCK), :]
        v_w = vwn_ref[pl.ds(w0, WINDOW + Q_BLOCK), :]
        dist = qpos_r - (w0 + lax.broadcasted_iota(jnp.int32, (1, WINDOW + Q_BLOCK), 1))
        pw = _masked_softmax(_dot_nt(qg, k_w), (dist >= 0) & (dist <= WINDOW))
        o_win = _dot(pw.astype(BF16), v_w)

        keep = (lane >= g * HEAD_DIM) & (lane < (g + 1) * HEAD_DIM)
        for h in range(GQA):
            hh = GQA * g + h
            r = slice(h * Q_BLOCK, (h + 1) * Q_BLOCK)
            o = (gate[:, hh * 3 + 0:hh * 3 + 1] * o_cmp[r] + gate[:, hh * 3 + 1:hh * 3 + 2] * o_slc[r]
                 + gate[:, hh * 3 + 2:hh * 3 + 3] * o_win[r])
            o_ref[:, hh * LANES:(hh + 1) * LANES] = jnp.where(keep, o, 0.0).astype(BF16)


def _attn_prompt_call(q, gate, kc, vc, cover, ksl, vsl, kwn, vwn, ebias):
    t = q.shape[0]
    nc, ns = cover.shape
    vmem = pl.BlockSpec(memory_space=pltpu.VMEM)
    return pl.pallas_call(
        functools.partial(_attn_prompt_kernel, nc=nc, ns=ns),
        grid=(t // Q_BLOCK,),
        in_specs=[pl.BlockSpec((Q_BLOCK, 1024), lambda i: (i, 0)), pl.BlockSpec((Q_BLOCK, LANES), lambda i: (i, 0)),
                  vmem, vmem, vmem, vmem, vmem, vmem, vmem, vmem],
        out_specs=pl.BlockSpec((Q_BLOCK, 1024), lambda i: (i, 0)),
        out_shape=jax.ShapeDtypeStruct((t, 1024), BF16),
        compiler_params=pltpu.CompilerParams(dimension_semantics=("arbitrary",), vmem_limit_bytes=VMEM_LIMIT),
        name="attn_prompt",
    )(q, gate, kc, vc, cover, ksl, vsl, kwn, vwn, ebias)


def _out_ffn_kernel(x_ref, oa_ref, ob_ref, oc_ref, g1_ref, sc2_ref, sh2_ref, g2_ref, wo_ref, lng_ref, lnb_ref,
                    wup_ref, cfw_ref, cfb_ref, fpast_ref, wdn_ref, xo_ref, fstate_ref, us_ref,
                    *, shift, tm, pad, alpha, d_ff):
    i = pl.program_id(0)

    @pl.when(i == 0)
    def _():
        us_ref[pl.ds(pad - 2 * shift, 2 * shift), :] = fpast_ref[...]

    mix = _dot(jnp.concatenate([oa_ref[...], ob_ref[...], oc_ref[...]], axis=1), wo_ref[...])
    x1 = _layer_norm(alpha * x_ref[...] + (1.0 + g1_ref[...]) * mix, lng_ref[0:1, :], lnb_ref[0:1, :])
    h2 = (x1 * (1.0 + sc2_ref[...]) + sh2_ref[...]).astype(BF16)
    up = _dot(h2, wup_ref[...])
    ua = up[:, :d_ff]
    us_ref[pl.ds(pad, tm), :] = ua
    yc = _shifted_conv(us_ref, ua, cfw_ref, cfb_ref, pad, shift, tm)
    tail = us_ref[pl.ds(pad + tm - 2 * shift, 2 * shift), :]
    fstate_ref[...] = tail
    us_ref[pl.ds(pad - 2 * shift, 2 * shift), :] = tail
    act = (yc * _sigmoid(yc) * up[:, d_ff:]).astype(BF16)
    y = _dot(act, wdn_ref[...])
    xo_ref[...] = _layer_norm(alpha * x1 + (1.0 + g2_ref[...]) * y, lng_ref[1:2, :], lnb_ref[1:2, :])


def _out_ffn_call(x, oa, ob, oc, g1, sc2, sh2, g2, wo, lng, lnb, wup, cfw, cfb, fpast, wdn, *, shift, tm, alpha, name):
    rows, d_model = x.shape
    d_ff = wdn.shape[0]
    pad = max(8, 2 * shift)
    mr = g1.shape[0]
    mod_spec = (pl.BlockSpec((1, d_model), lambda i: (0, 0)) if mr == 1
                else pl.BlockSpec((tm, d_model), lambda i: (i, 0)))

    def row_spec(n):
        return pl.BlockSpec((tm, n), lambda i: (i, 0))

    vmem = pl.BlockSpec(memory_space=pltpu.VMEM)
    return pl.pallas_call(
        functools.partial(_out_ffn_kernel, shift=shift, tm=tm, pad=pad, alpha=alpha, d_ff=d_ff),
        grid=(rows // tm,),
        in_specs=[row_spec(d_model), row_spec(256), row_spec(1024), row_spec(256), mod_spec, mod_spec, mod_spec,
                  mod_spec, vmem, vmem, vmem, vmem, vmem, vmem, vmem, vmem],
        out_specs=[row_spec(d_model), pl.BlockSpec((2 * shift, d_ff), lambda i: (0, 0))],
        out_shape=[jax.ShapeDtypeStruct((rows, d_model), F32), jax.ShapeDtypeStruct((2 * shift, d_ff), F32)],
        scratch_shapes=[pltpu.VMEM((pad + tm, d_ff), F32)],
        compiler_params=pltpu.CompilerParams(dimension_semantics=("arbitrary",), vmem_limit_bytes=VMEM_LIMIT),
        name=name,
    )(x, oa, ob, oc, g1, sc2, sh2, g2, wo, lng, lnb, wup, cfw, cfb, fpast, wdn)


def _cmp_stream_kernel(pt_ref, cache_ref, w1_ref, pet_ref, w2_ref, kc_ref, vc_ref, buf, sem, carry,
                       *, page_base, n_pages, pg, n_groups, total):
    b = pl.program_id(0)
    gi = pl.program_id(1)
    step = b * n_groups + gi
    slot = lax.rem(step, 2)
    m = pg * (PAGE_SIZE // CMP_STRIDE)

    def page_copy(page, slt, i, s):
        return pltpu.make_async_copy(cache_ref.at[page, :, pl.ds(s * LANES, LANES)],
                                     buf.at[slt, s, pl.ds(i * PAGE_SIZE, PAGE_SIZE), :], sem.at[slt])

    def issue(stp, slt):
        base = lax.div(stp, n_groups) * n_pages + lax.rem(stp, n_groups) * pg
        for i in range(pg):
            page = page_base + pt_ref[base + i]
            for s in range(2):
                page_copy(page, slt, i, s).start()

    @pl.when(step == 0)
    def _():
        carry[...] = jnp.zeros(carry.shape, F32)
        issue(step, slot)

    @pl.when(step + 1 < total)
    def _():
        issue(step + 1, 1 - slot)

    for i in range(pg):
        for s in range(2):
            page_copy(0, slot, i, s).wait()

    lane = lax.broadcasted_iota(jnp.int32, (1, LANES), 1)
    first_half = lane < HEAD_DIM
    row0 = lax.broadcasted_iota(jnp.int32, (m, 1), 0) == 0
    pieces = [[[], []], [[], []]]
    for pr in range(CMP_STRIDE // 2):
        for s in range(2):
            a = buf[slot, s, pl.ds(2 * pr, m, stride=CMP_STRIDE), :]
            bb = buf[slot, s, pl.ds(2 * pr + 1, m, stride=CMP_STRIDE), :]
            pieces[s][0].append(jnp.where(first_half, a, pltpu.roll(bb, HEAD_DIM, 1)).astype(BF16))
            pieces[s][1].append(jnp.where(first_half, pltpu.roll(a, HEAD_DIM, 1), bb).astype(BF16))
    for s, o_ref in enumerate((kc_ref, vc_ref)):
        acc = jnp.zeros((m, LANES), F32)
        for g in range(N_KV_HEADS):
            parts = _dot(jnp.concatenate(pieces[s][g], axis=1), w1_ref[s])
            p0 = parts[:, :CMP_HID]
            p1 = parts[:, CMP_HID:]
            prev = jnp.where(gi == 0, 0.0, carry[s * 2 + g][0:1, :])
            p0s = jnp.where(row0, prev, pltpu.roll(p0, 1, 0))
            carry[s * 2 + g] = jnp.broadcast_to(p0[m - 1:m, :], (8, CMP_HID))
            hid = _gelu(p0s + p1 + pet_ref[s][0:1, :])
            acc = acc + _dot(hid.astype(BF16), w2_ref[s, g])
        o_ref[0] = acc.astype(BF16)


def _cmp_stream_call(pt_flat, cache3, w1s, pet, w2p, *, layer, n_phys, batch, n_pages, pg):
    n_groups = n_pages // pg
    m = pg * (PAGE_SIZE // CMP_STRIDE)
    nc = n_pages * (PAGE_SIZE // CMP_STRIDE)
    total = batch * n_groups

    def full(a):
        nd = a.ndim
        return pl.BlockSpec(a.shape, lambda b, g, pt: (0,) * nd)

    grid_spec = pltpu.PrefetchScalarGridSpec(
        num_scalar_prefetch=1,
        grid=(batch, n_groups),
        in_specs=[pl.BlockSpec(memory_space=pl.ANY), full(w1s), full(pet), full(w2p)],
        out_specs=[pl.BlockSpec((1, m, LANES), lambda b, g, pt: (b, g, 0))] * 2,
        scratch_shapes=[pltpu.VMEM((2, 2, pg * PAGE_SIZE, LANES), F32), pltpu.SemaphoreType.DMA((2,)),
                        pltpu.VMEM((4, 8, CMP_HID), F32)],
    )
    return pl.pallas_call(
        functools.partial(_cmp_stream_kernel, page_base=layer * n_phys, n_pages=n_pages, pg=pg,
                          n_groups=n_groups, total=total),
        grid_spec=grid_spec,
        out_shape=[jax.ShapeDtypeStruct((batch, nc, LANES), BF16)] * 2,
        compiler_params=pltpu.CompilerParams(dimension_semantics=("arbitrary", "arbitrary"),
                                             vmem_limit_bytes=VMEM_LIMIT),
        name="cmp_stream_sample",
    )(pt_flat, cache3, w1s, pet, w2p)


def _cmp_attn_sample_kernel(q_ref, kc_ref, vc_ref, cover_ref, ocmp_ref, idx_ref, bias_ref,
                            *, past, nc, ns, nsp, n_cache_blocks):
    rows = lax.broadcasted_iota(jnp.int32, (32, 1), 0)
    qpos_r = past + (rows & 7)
    qp = past + lax.broadcasted_iota(jnp.int32, (8, 1), 0)
    m_idx = lax.broadcasted_iota(jnp.int32, (1, nc), 1)
    blk = lax.broadcasted_iota(jnp.int32, (1, nsp), 1)
    blk_f = blk.astype(F32)
    lane = lax.broadcasted_iota(jnp.int32, (1, LANES), 1)
    cmask = (m_idx >= 1) & (CMP_STRIDE * (m_idx - 1) + CMP_LEN - 1 <= qpos_r)
    for g in range(N_KV_HEADS):
        p = _masked_softmax(_dot_nt(q_ref[0, g], kc_ref[0]), cmask)
        ocmp_ref[0, g] = _dot(p.astype(BF16), vc_ref[0])
        p4 = p[0:8] + p[8:16] + p[16:24] + p[24:32]
        hi, lo = _split(p4)
        imp = _dot(hi, cover_ref[...]) + _dot(lo, cover_ref[...])
        qblk = qp // SLC_BLOCK
        elig = blk * SLC_BLOCK <= qp
        forced = (blk == 0) | (blk == qblk) | (blk == qblk - 1)
        val = jnp.where(blk < ns, jnp.where(elig, jnp.where(forced, FORCE, imp), -FORCE), REMOVED)
        _, firsts, tops = _top_select(val, blk_f, nsp)
        idx = jnp.zeros((8, LANES), F32)
        bias = jnp.zeros((8, LANES), F32)
        for t in range(N_SELECT):
            ok = (tops[t] > -0.5 * FORCE) & (firsts[t] < float(n_cache_blocks))
            idx = jnp.where(lane == t, firsts[t], idx)
            bias = jnp.where(lane == t, jnp.where(ok, 0.0, NEG_INF), bias)
        idx_ref[0, g] = idx.astype(jnp.int32)
        bias_ref[0, g] = bias


def _cmp_attn_sample_call(q_hq, kc, vc, cover, *, past, ns, n_cache_blocks):
    batch, nc, _ = kc.shape
    nsp = cover.shape[1]
    blk4 = lambda r: pl.BlockSpec((1, N_KV_HEADS, r, LANES), lambda b: (b, 0, 0, 0))
    return pl.pallas_call(
        functools.partial(_cmp_attn_sample_kernel, past=past, nc=nc, ns=ns, nsp=nsp, n_cache_blocks=n_cache_blocks),
        grid=(batch,),
        in_specs=[blk4(32), pl.BlockSpec((1, nc, LANES), lambda b: (b, 0, 0)),
                  pl.BlockSpec((1, nc, LANES), lambda b: (b, 0, 0)), pl.BlockSpec(cover.shape, lambda b: (0, 0))],
        out_specs=[blk4(32), blk4(8), blk4(8)],
        out_shape=[jax.ShapeDtypeStruct((batch, N_KV_HEADS, 32, LANES), F32),
                   jax.ShapeDtypeStruct((batch, N_KV_HEADS, 8, LANES), jnp.int32),
                   jax.ShapeDtypeStruct((batch, N_KV_HEADS, 8, LANES), F32)],
        compiler_params=pltpu.CompilerParams(dimension_semantics=("arbitrary",), vmem_limit_bytes=VMEM_LIMIT),
        name="cmp_attn_sample",
    )(q_hq, kc, vc, cover)


def _sel_attn_sample_kernel(idx_ref, pt_ref, cache_ref, q_ref, bias_ref, ex_ref, knew_ref, vnew_ref, win_ref,
                            wnew_ref, ocmp_ref, gate_ref, o_ref, wout_ref, buf, sem,
                            *, page_base, n_pages, tq, n_cache_blocks, batch):
    b = pl.program_id(0)
    slot = lax.rem(b, 2)
    n_ent = N_KV_HEADS * tq * N_SELECT
    half_pages = PAGE_SIZE // SLC_BLOCK

    def blk_copy(page, half, slt, e):
        return pltpu.make_async_copy(cache_ref.at[page, pl.ds(half * SLC_BLOCK, SLC_BLOCK), pl.ds(256, 256)],
                                     buf.at[slt, pl.ds(e * SLC_BLOCK, SLC_BLOCK), :], sem.at[slt])

    def issue(bb, slt):
        def body(e, _):
            g = e // (tq * N_SELECT)
            qk = e - g * (tq * N_SELECT)
            qi = qk // N_SELECT
            k = qk - qi * N_SELECT
            blk = idx_ref[((bb * N_KV_HEADS + g) * tq + qi) * N_SELECT + k]
            blk = jnp.minimum(blk, n_cache_blocks - 1)
            page = pt_ref[bb * n_pages + blk // half_pages]
            blk_copy(page_base + page, lax.rem(blk, half_pages), slt, e).start()
            return 0
        lax.fori_loop(0, n_ent, body, 0)

    @pl.when(b == 0)
    def _():
        issue(b, slot)

    @pl.when(b + 1 < batch)
    def _():
        issue(b + 1, 1 - slot)

    def wait_body(e, _):
        blk_copy(0, 0, slot, e).wait()
        return 0
    lax.fori_loop(0, n_ent, wait_body, 0)

    lane = lax.broadcasted_iota(jnp.int32, (1, LANES), 1)
    col8 = lax.broadcasted_iota(jnp.int32, (1, 8), 1)
    wcol = lax.broadcasted_iota(jnp.int32, (1, WINDOW), 1)
    win = win_ref[0]
    k_w = win[:, 0:LANES].astype(BF16)
    v_w = win[:, LANES:2 * LANES].astype(BF16)
    wnew = wnew_ref[0]
    k_wn = wnew[:, 0:LANES].astype(BF16)
    v_wn = wnew[:, LANES:2 * LANES].astype(BF16)
    n_keys = N_SELECT * SLC_BLOCK
    for g in range(N_KV_HEADS):
        bias_all = _dot(bias_ref[0, g].astype(BF16), ex_ref[...])
        keep = (lane >= g * HEAD_DIM) & (lane < (g + 1) * HEAD_DIM)
        for qi in range(tq):
            q4 = q_ref[0, g, qi]
            new_ok = (col8 <= qi) & (col8 < tq)
            e0 = (g * tq + qi) * n_keys
            kv_sel = buf[slot, pl.ds(e0, n_keys), :]
            s = _dot_nt(q4, kv_sel[:, 0:LANES].astype(BF16)) + bias_all[qi:qi + 1, :]
            s_n = jnp.where(new_ok, _dot_nt(q4, knew_ref[0]), NEG_INF)
            mx = jnp.maximum(jnp.max(s, axis=-1, keepdims=True), jnp.max(s_n, axis=-1, keepdims=True))
            pe = jnp.exp2(s - mx)
            pn = jnp.exp2(s_n - mx)
            l = jnp.sum(pe, axis=-1, keepdims=True) + jnp.sum(pn, axis=-1, keepdims=True)
            o_slc = (_dot(pe.astype(BF16), kv_sel[:, LANES:2 * LANES].astype(BF16))
                     + _dot(pn.astype(BF16), vnew_ref[0])) * (1.0 / l)
            sw = jnp.where(wcol >= qi, _dot_nt(q4, k_w), NEG_INF)
            sw_n = jnp.where(new_ok, _dot_nt(q4, k_wn), NEG_INF)
            mw = jnp.maximum(jnp.max(sw, axis=-1, keepdims=True), jnp.max(sw_n, axis=-1, keepdims=True))
            pw = jnp.exp2(sw - mw)
            pwn = jnp.exp2(sw_n - mw)
            lw = jnp.sum(pw, axis=-1, keepdims=True) + jnp.sum(pwn, axis=-1, keepdims=True)
            o_win = (_dot(pw.astype(BF16), v_w) + _dot(pwn.astype(BF16), v_wn)) * (1.0 / lw)
            gt = gate_ref[0, g, qi]
            o = gt[:, 0:1] * ocmp_ref[0, g, qi] + gt[:, 1:2] * o_slc + gt[:, 2:3] * o_win
            o_ref[0, g, qi] = jnp.where(keep, o, 0.0)

    wout_ref[0, pl.ds(0, WINDOW - 8), :] = win_ref[0, pl.ds(tq, WINDOW - 8), :]
    row8 = lax.broadcasted_iota(jnp.int32, (8, 1), 0)
    last_old = pltpu.roll(win_ref[0, pl.ds(WINDOW - 8, 8), :], 8 - tq, 0)
    wout_ref[0, pl.ds(WINDOW - 8, 8), :] = jnp.where(row8 < 8 - tq, last_old, pltpu.roll(wnew, 8 - tq, 0))


def _sel_attn_sample_call(idx_flat, pt_flat, cache3, q_qh, bias, expand, knew, vnew, win, wnew, ocmp_qh, gate_qh,
                          *, layer, n_phys, n_pages, tq, n_cache_blocks):
    batch = q_qh.shape[0]
    n_ent = N_KV_HEADS * tq * N_SELECT
    b5 = lambda: pl.BlockSpec((1, N_KV_HEADS, tq, 8, LANES), lambda b, i, p: (b, 0, 0, 0, 0))
    b3 = lambda r, c: pl.BlockSpec((1, r, c), lambda b, i, p: (b, 0, 0))
    grid_spec = pltpu.PrefetchScalarGridSpec(
        num_scalar_prefetch=2,
        grid=(batch,),
        in_specs=[pl.BlockSpec(memory_space=pl.ANY), b5(),
                  pl.BlockSpec((1, N_KV_HEADS, 8, LANES), lambda b, i, p: (b, 0, 0, 0)),
                  pl.BlockSpec(expand.shape, lambda b, i, p: (0, 0)),
                  b3(8, LANES), b3(8, LANES), b3(WINDOW, 256), b3(8, 256), b5(), b5()],
        out_specs=[b5(), b3(WINDOW, 256)],
        scratch_shapes=[pltpu.VMEM((2, n_ent * SLC_BLOCK, 256), F32), pltpu.SemaphoreType.DMA((2,))],
    )
    return pl.pallas_call(
        functools.partial(_sel_attn_sample_kernel, page_base=layer * n_phys, n_pages=n_pages, tq=tq,
                          n_cache_blocks=n_cache_blocks, batch=batch),
        grid_spec=grid_spec,
        out_shape=[jax.ShapeDtypeStruct((batch, N_KV_HEADS, tq, 8, LANES), F32),
                   jax.ShapeDtypeStruct((batch, WINDOW, 256), F32)],
        compiler_params=pltpu.CompilerParams(dimension_semantics=("arbitrary",), vmem_limit_bytes=VMEM_LIMIT),
        name="sel_attn_sample",
    )(idx_flat, pt_flat, cache3, q_qh, bias, expand, knew, vnew, win, wnew, ocmp_qh, gate_qh)


def _prep_w_in(w):
    d = w.shape[0]
    wq = (w[:, 768:1280] * (HEAD_DIM ** -0.5 * LOG2E)).reshape(d, N_HEADS, HEAD_DIM)
    z = jnp.zeros_like(wq)
    grp = (jnp.arange(N_HEADS) // GQA)[None, :, None]
    wq = jnp.concatenate([jnp.where(grp == 0, wq, z), jnp.where(grp == 1, wq, z)], axis=-1).reshape(d, N_HEADS * LANES)
    gate = jnp.pad(w[:, 2048:2072], ((0, 0), (0, LANES - N_HEADS * N_BRANCH)))
    return jnp.concatenate([w[:, :768], wq, w[:, 1280:2048], gate, w[:, 2072:]], axis=1).astype(BF16)


def _prep_w_o(w):
    d = w.shape[1]
    wb = w[256:768].reshape(N_HEADS, HEAD_DIM, d)
    z = jnp.zeros_like(wb)
    grp = (jnp.arange(N_HEADS) // GQA)[:, None, None]
    wb = jnp.concatenate([jnp.where(grp == 0, wb, z), jnp.where(grp == 1, wb, z)], axis=1).reshape(N_HEADS * LANES, d)
    return jnp.concatenate([w[:256], wb, w[768:]], axis=0).astype(BF16)


def _prep_w1(w1):
    w = w1.reshape(2, 2, CMP_STRIDE, HEAD_DIM, CMP_HID)
    return jnp.transpose(w, (0, 2, 3, 1, 4)).reshape(2, CMP_STRIDE * HEAD_DIM, 2 * CMP_HID)


def _prep_w1_grouped(w1s):
    w = w1s.reshape(2, CMP_STRIDE, 1, HEAD_DIM, 1, 2 * CMP_HID)
    eye = jnp.eye(N_KV_HEADS, dtype=w.dtype).reshape(1, 1, N_KV_HEADS, 1, N_KV_HEADS, 1)
    return (w * eye).reshape(2, CMP_STRIDE * N_KV_HEADS * HEAD_DIM, N_KV_HEADS * 2 * CMP_HID)


def _prep_w2(w2):
    z = jnp.zeros_like(w2)
    return jnp.stack([jnp.concatenate([w2, z], axis=-1), jnp.concatenate([z, w2], axis=-1)], axis=1).astype(BF16)


def _cover_matrix(nc, ns_real, ns_pad):
    m = np.arange(nc)[:, None]
    b = np.arange(ns_pad)[None, :]
    c = (m >= 4 * b) & (m <= 4 * b + 4) & (m >= 1) & (b < ns_real)
    return jnp.asarray(c.astype(np.float32), dtype=BF16)


def _block_bias_matrix():
    k = np.arange(KEY_TILE)[:, None]
    b = np.arange(LANES)[None, :]
    return jnp.asarray(np.where(k // SLC_BLOCK == b, NEG_INF, 0.0).astype(np.float32), dtype=BF16)


def _expand_matrix():
    k = np.arange(LANES)[:, None]
    c = np.arange(N_SELECT * SLC_BLOCK)[None, :]
    return jnp.asarray((c // SLC_BLOCK == k).astype(np.float32), dtype=BF16)


def kernel(x_prompt, x_sample, cache_nsa_kv, state_win_kv, state_conv, state_ffn_conv, page_table, c_prompt, c_sample, w_ada, b_ada, w_in, conv_a_w, conv_a_b, cmp_pe, cmp_w1, cmp_w2, sgu_ln_g, sgu_ln_b, sgu_w, sgu_b, w_o, ln_g, ln_b, w_ffn_up, conv_f_w, conv_f_b, w_ffn_down):
    depth = w_in.shape[0]
    _, t, d_model = x_prompt.shape
    nb, tq, _ = x_sample.shape
    n_phys = cache_nsa_kv.shape[1]
    n_pages = page_table.shape[1]
    past = n_pages * PAGE_SIZE
    d_ff = w_ffn_down.shape[1]
    alpha = (2 * depth) ** 0.25
    rs = nb * tq
    assert x_prompt.shape[0] == 1 and c_prompt.shape[0] == 1
    assert d_model == 1024 and t % KEY_TILE == 0 and t >= WINDOW + Q_BLOCK
    assert tq == 4 and rs == GMLP_CHUNK and past % KEY_TILE == 0 and past >= WINDOW
    assert state_win_kv.shape[2] == WINDOW

    rc = -(-(1 + nb) // 8) * 8
    c_all = jnp.pad(jnp.concatenate([c_prompt, c_sample], axis=0), ((0, rc - 1 - nb), (0, 0)))
    mods = _ada_call(c_all, w_ada, b_ada)

    nc_p, ns_p = t // CMP_STRIDE, t // SLC_BLOCK
    cover_p = _cover_matrix(nc_p, ns_p, ns_p)
    nc_s = (past + tq) // CMP_STRIDE
    ns_s = -(-(past + tq) // SLC_BLOCK)
    ns_s_pad = -(-ns_s // LANES) * LANES
    n_cache_blocks = past // SLC_BLOCK
    cover_s = _cover_matrix(nc_s, ns_s, ns_s_pad)
    ebias = _block_bias_matrix()
    expand = _expand_matrix()
    pt_flat = page_table.reshape(-1)
    cache3 = cache_nsa_kv.reshape(depth * n_phys, PAGE_SIZE, 4 * N_KV_HEADS * HEAD_DIM)

    xp = x_prompt[0]
    xs = jnp.transpose(x_sample, (1, 0, 2)).reshape(rs, d_model)
    tril_full = jnp.tril(jnp.ones((GMLP_CHUNK, GMLP_CHUNK), F32))
    tril_tq = jnp.tril(jnp.ones((tq, tq), F32))

    ps, ss = [], []
    for l in range(depth):
        w_all = _prep_w_in(w_in[l])
        wo_p = _prep_w_o(w_o[l])
        wup = w_ffn_up[l].astype(BF16)
        wdn = w_ffn_down[l].astype(BF16)
        w1s = _prep_w1(cmp_w1[l])
        w1g = _prep_w1_grouped(w1s).astype(BF16)
        w1s = w1s.astype(BF16)
        w2p = _prep_w2(cmp_w2[l])
        pe_flat = jnp.broadcast_to(cmp_pe[l].reshape(2, 1, CMP_LEN * HEAD_DIM), (2, 8, CMP_LEN * HEAD_DIM))
        pet = _pe_term_call(pe_flat, cmp_w1[l].reshape(2, CMP_LEN * HEAD_DIM, CMP_HID))
        cw, cb = conv_a_w[l], conv_a_b[l].reshape(1, -1)
        cfw, cfb = conv_f_w[l], conv_f_b[l].reshape(1, -1)
        lng, lnb = sgu_ln_g[l].reshape(1, -1), sgu_ln_b[l].reshape(1, -1)

        def mod_rows(r0, r1, rep):
            parts = [mods[l, r0:r1, k * d_model:(k + 1) * d_model] for k in range(6)]
            return [jnp.tile(p_, (rep, 1)) if rep > 1 else p_ for p_ in parts]

        sh1, sc1, g1, sh2, sc2, g2 = mod_rows(0, 1, 1)
        wm_p = (sgu_w[l] * tril_full).astype(BF16)
        sb_p = jnp.repeat(sgu_b[l].T, HEAD_DIM, axis=1)
        (oa, oc, q, kvf, kcr, vcr, ksl, vsl, kwn, vwn, gate, _, cst) = _in_proj_call(
            xp, sc1, sh1, w_all, cw, cb, jnp.zeros((2, 256), F32), lng, lnb, wm_p, sb_p,
            shift=1, tm=512, name="in_proj_prompt")
        kc, vc = _compress_prompt_call(kcr.reshape(nc_p, CMP_STRIDE * LANES), vcr.reshape(nc_p, CMP_STRIDE * LANES),
                                       w1g, pet, w2p)
        ob = _attn_prompt_call(q, gate, kc, vc, cover_p, ksl, vsl, kwn, vwn, ebias)
        xp, fst = _out_ffn_call(xp, oa, ob, oc, g1, sc2, sh2, g2, wo_p, ln_g[l], ln_b[l], wup, cfw, cfb,
                                jnp.zeros((2, d_ff), F32), wdn, shift=1, tm=256, alpha=alpha, name="out_ffn_prompt")
        ps.append((kvf[:, :512].reshape(1, t // PAGE_SIZE, PAGE_SIZE, 4, N_KV_HEADS, HEAD_DIM),
                   kvf[t - WINDOW:, 512:].reshape(1, WINDOW, 2, N_KV_HEADS, HEAD_DIM),
                   cst[None], fst[None]))

        sh1, sc1, g1, sh2, sc2, g2 = mod_rows(1, 1 + nb, tq)
        eye_b = jnp.eye(nb, dtype=F32)
        wm_s = jax.vmap(lambda w: jnp.kron(w[:tq, :tq] * tril_tq, eye_b))(sgu_w[l]).astype(BF16)
        sb_s = jnp.repeat(jnp.repeat(sgu_b[l][:, :tq].T, nb, axis=0), HEAD_DIM, axis=1)
        cpast = jnp.transpose(state_conv[l], (1, 0, 2)).reshape(2 * nb, -1)
        (oa, oc, q, kvf, _, _, ksl, vsl, _, _, gate, vrow, cst) = _in_proj_call(
            xs, sc1, sh1, w_all, cw, cb, cpast, lng, lnb, wm_s, sb_s, shift=nb, tm=rs, name="in_proj_sample")
        kc, vc = _cmp_stream_call(pt_flat, cache3, w1s, pet, w2p, layer=l, n_phys=n_phys, batch=nb,
                                  n_pages=n_pages, pg=min(32, n_pages))

        def by_batch(a):
            return jnp.transpose(a.reshape(tq, nb, -1), (1, 0, 2))

        qb5 = by_batch(q).reshape(nb, tq, N_KV_HEADS, GQA, LANES)
        q_hq = jnp.pad(jnp.transpose(qb5, (0, 2, 3, 1, 4)), ((0, 0), (0, 0), (0, 0), (0, 8 - tq), (0, 0)))
        q_hq = q_hq.reshape(nb, N_KV_HEADS, 32, LANES)
        q_qh = jnp.pad(jnp.transpose(qb5, (0, 2, 1, 3, 4)), ((0, 0), (0, 0), (0, 0), (0, 8 - GQA), (0, 0)))
        ocmp, idx, bias = _cmp_attn_sample_call(q_hq, kc, vc, cover_s, past=past, ns=ns_s,
                                                n_cache_blocks=n_cache_blocks)
        ocmp_qh = jnp.transpose(ocmp.reshape(nb, N_KV_HEADS, GQA, 8, LANES)[:, :, :, :tq], (0, 1, 3, 2, 4))
        ocmp_qh = jnp.pad(ocmp_qh, ((0, 0), (0, 0), (0, 0), (0, 8 - GQA), (0, 0)))
        g5 = by_batch(gate)[:, :, :N_HEADS * N_BRANCH].reshape(nb, tq, N_KV_HEADS, GQA, N_BRANCH)
        gate_qh = jnp.pad(jnp.transpose(g5, (0, 2, 1, 3, 4)),
                          ((0, 0), (0, 0), (0, 0), (0, 8 - GQA), (0, LANES - N_BRANCH)))
        pad8 = lambda a: jnp.pad(by_batch(a), ((0, 0), (0, 8 - tq), (0, 0)))
        o5, wout = _sel_attn_sample_call(
            idx[:, :, :tq, :N_SELECT].reshape(-1), pt_flat, cache3, q_qh, bias, expand, pad8(ksl), pad8(vsl),
            state_win_kv[l].reshape(nb, WINDOW, 256), pad8(kvf[:, 512:]), ocmp_qh, gate_qh,
            layer=l, n_phys=n_phys, n_pages=n_pages, tq=tq, n_cache_blocks=n_cache_blocks)
        ob = jnp.transpose(o5[:, :, :, :GQA], (2, 0, 1, 3, 4)).reshape(rs, N_HEADS * LANES).astype(BF16)
        fpast = jnp.transpose(state_ffn_conv[l], (1, 0, 2)).reshape(2 * nb, -1)
        xs, fst = _out_ffn_call(xs, oa, ob, oc, g1, sc2, sh2, g2, wo_p, ln_g[l], ln_b[l], wup, cfw, cfb, fpast, wdn,
                                shift=nb, tm=rs, alpha=alpha, name="out_ffn_sample")
        ss.append((by_batch(kvf[:, :512]).reshape(nb, tq, 4, N_KV_HEADS, HEAD_DIM),
                   wout.reshape(nb, WINDOW, 2, N_KV_HEADS, HEAD_DIM),
                   jnp.transpose(cst.reshape(2, nb, -1), (1, 0, 2)),
                   jnp.transpose(fst.reshape(2, nb, -1), (1, 0, 2)),
                   by_batch(vrow)))

    ys = jnp.transpose(xs.reshape(tq, nb, d_model), (1, 0, 2))
    return (xp[None], ys,
            jnp.stack([s[0] for s in ps]), jnp.stack([s[1] for s in ps]),
            jnp.stack([s[2] for s in ps]), jnp.stack([s[3] for s in ps]),
            jnp.stack([s[0] for s in ss]), jnp.stack([s[1] for s in ss]),
            jnp.stack([s[2] for s in ss]), jnp.stack([s[3] for s in ss]),
            jnp.stack([s[4] for s in ss]))
```

```python
import functools
import math

import numpy as np
import jax
import jax.numpy as jnp
from jax import lax
from jax.experimental import pallas as pl
from jax.experimental.pallas import tpu as pltpu

F32 = jnp.float32
BF16 = jnp.bfloat16

HEAD_DIM = 64
N_HEADS = 8
N_KV_HEADS = 2
GQA = N_HEADS // N_KV_HEADS
N_BRANCH = 3
CONV_K = 3
CMP_LEN = 32
CMP_STRIDE = 16
CMP_HID = 128
SLC_BLOCK = 64
N_SELECT = 16
WINDOW = 512
Q_BLOCK = 128
PAGE_SIZE = 128
GMLP_CHUNK = 128
GMLP_GROUPS = 4
LN_EPS = 1e-5
NEG_INF = -1e30
FORCE = 1e4
REMOVED = -3e38
LOG2E = 1.4426950408889634

LANES = 128
KEY_TILE = 512
BLOCKS_PER_TILE = KEY_TILE // SLC_BLOCK
NS_ROWS = 16
VMEM_LIMIT = 56 * 1024 * 1024


def _dot(a, b):
    return jnp.dot(a, b, preferred_element_type=F32)


def _dot_nt(a, b):
    return lax.dot_general(a, b, (((1,), (1,)), ((), ())), preferred_element_type=F32)


def _split(a):
    hi = a.astype(BF16)
    lo = (a - hi.astype(F32)).astype(BF16)
    return hi, lo


def _dot3(a, b):
    ah, al = _split(a)
    bh, bl = _split(b)
    return _dot(ah, bh) + _dot(ah, bl) + _dot(al, bh)


def _sigmoid(x):
    return 1.0 / (1.0 + jnp.exp(-x))


def _gelu(x):
    c = math.sqrt(2.0 / math.pi)
    return 0.5 * x * (1.0 + jnp.tanh(c * (x + 0.044715 * (x * x * x))))


def _layer_norm(x, g, b):
    mu = jnp.mean(x, axis=-1, keepdims=True)
    xc = x - mu
    var = jnp.mean(xc * xc, axis=-1, keepdims=True)
    return xc * lax.rsqrt(var + LN_EPS) * g + b


def _masked_softmax(s, mask, axis):
    sm = jnp.where(mask, s, NEG_INF)
    mx = jnp.max(sm, axis=axis, keepdims=True)
    e = jnp.where(mask, jnp.exp2(s - mx), 0.0)
    l = jnp.sum(e, axis=axis, keepdims=True)
    return e * (1.0 / jnp.where(l > 0.0, l, 1.0))


def _top_select(val, blk_f, n_blk, axis):
    sel = jnp.zeros_like(val)
    firsts, tops = [], []
    for _ in range(N_SELECT):
        mx = jnp.max(val, axis=axis, keepdims=True)
        first = jnp.min(jnp.where(val == mx, blk_f, float(n_blk)), axis=axis, keepdims=True)
        hit = blk_f == first
        sel = jnp.where(hit, 1.0, sel)
        val = jnp.where(hit, REMOVED, val)
        firsts.append(first)
        tops.append(mx)
    return sel, firsts, tops


def _shifted_conv(src_ref, x, w_ref, b_ref, pad, shift, rows):
    x2 = src_ref[pl.ds(pad - 2 * shift, rows), :]
    x1 = src_ref[pl.ds(pad - shift, rows), :]
    return w_ref[0:1, :] * x2 + w_ref[1:2, :] * x1 + w_ref[2:3, :] * x + b_ref[...]


def _ada_kernel(c_ref, w_ref, b_ref, o_ref):
    c = c_ref[...]
    o_ref[0] = _dot3(c * _sigmoid(c), w_ref[0]) + b_ref[0]


def _ada_call(c_all, w_ada, b_ada):
    depth, d_model, n_mod = w_ada.shape
    rc = c_all.shape[0]
    tn = 1024
    return pl.pallas_call(
        _ada_kernel,
        grid=(depth, n_mod // tn),
        in_specs=[pl.BlockSpec((rc, d_model), lambda l, n: (0, 0)),
                  pl.BlockSpec((1, d_model, tn), lambda l, n: (l, 0, n)),
                  pl.BlockSpec((1, 1, tn), lambda l, n: (l, 0, n))],
        out_specs=pl.BlockSpec((1, rc, tn), lambda l, n: (l, 0, n)),
        out_shape=jax.ShapeDtypeStruct((depth, rc, n_mod), F32),
        compiler_params=pltpu.CompilerParams(dimension_semantics=("arbitrary", "arbitrary"),
                                             vmem_limit_bytes=VMEM_LIMIT),
        name="ada_mod",
    )(c_all, w_ada, b_ada.reshape(depth, 1, n_mod))


_C_AB, _C_AC, _C_AH = 0, 256, 512
_C_Q = 768
_C_KV = 1792
_C_GATE = 2560
_C_GU = 2688
_C_GV = 2944
_N_COL = 3200
_N_SLOT = 6


def _in_proj_kernel(x_ref, sc_ref, sh_ref, w_ref, cw_ref, cb_ref, cpast_ref, lng_ref, lnb_ref, wm_ref, sb_ref,
                    *rest, shift, tm, pad, transposed):
    if transposed:
        (oa_ref, oc_ref, cstate_ref, qt_ref, kvt_ref, kcr_ref, vcr_ref, ksl_ref, kwn_ref, vslt_ref, vwnt_ref,
         gatet_ref, zs_ref) = rest
    else:
        (oa_ref, oc_ref, cstate_ref, q_ref, kvf_ref, ksl_ref, vsl_ref, gate_ref, vrow_ref, zs_ref) = rest
    i = pl.program_id(0)

    @pl.when(i == 0)
    def _():
        zs_ref[pl.ds(pad - 2 * shift, 2 * shift), :] = cpast_ref[...]

    h = (x_ref[...] * (1.0 + sc_ref[...]) + sh_ref[...]).astype(BF16)
    p = _dot(h, w_ref[...])

    z = p[:, _C_AC:_C_AC + 256] * p[:, _C_AH:_C_AH + 256]
    zs_ref[pl.ds(pad, tm), :] = z
    y = _shifted_conv(zs_ref, z, cw_ref, cb_ref, pad, shift, tm)
    oa_ref[...] = (p[:, _C_AB:_C_AB + 256] * y).astype(BF16)
    tail = zs_ref[pl.ds(pad + tm - 2 * shift, 2 * shift), :]
    cstate_ref[...] = tail
    zs_ref[pl.ds(pad - 2 * shift, 2 * shift), :] = tail

    kv = [p[:, _C_KV + k * LANES:_C_KV + (k + 1) * LANES] for k in range(_N_SLOT)]
    gate = _sigmoid(p[:, _C_GATE:_C_GATE + LANES])
    if transposed:
        for hh in range(N_HEADS):
            qt_ref[pl.ds(hh * LANES, LANES), :] = p[:, _C_Q + hh * LANES:_C_Q + (hh + 1) * LANES].T.astype(BF16)
        kvt = [a.T for a in kv]
        for k in range(_N_SLOT):
            kvt_ref[pl.ds(k * LANES, LANES), :] = kvt[k]
        kcr_ref[...] = kv[0].astype(BF16)
        vcr_ref[...] = kv[1].astype(BF16)
        ksl_ref[...] = kv[2].astype(BF16)
        kwn_ref[...] = kv[4].astype(BF16)
        vslt_ref[...] = kvt[3].astype(BF16)
        vwnt_ref[...] = kvt[5].astype(BF16)
        gatet_ref[...] = gate.T
    else:
        q_ref[...] = p[:, _C_Q:_C_Q + 1024].astype(BF16)
        kvf_ref[...] = p[:, _C_KV:_C_KV + _N_SLOT * LANES]
        ksl_ref[...] = kv[2].astype(BF16)
        vsl_ref[...] = kv[3].astype(BF16)
        gate_ref[...] = gate

    u = _gelu(p[:, _C_GU:_C_GU + 256])
    v = _layer_norm(_gelu(p[:, _C_GV:_C_GV + 256]), lng_ref[...], lnb_ref[...])
    if not transposed:
        vrow_ref[...] = v
    lane = lax.broadcasted_iota(jnp.int32, (1, 256), 1)
    for c in range(tm // GMLP_CHUNK):
        vc = v[c * GMLP_CHUNK:(c + 1) * GMLP_CHUNK]
        mixed = sb_ref[...]
        for g in range(GMLP_GROUPS):
            vg = jnp.where((lane >= g * HEAD_DIM) & (lane < (g + 1) * HEAD_DIM), vc, 0.0).astype(BF16)
            mixed = mixed + _dot(wm_ref[g], vg)
        oc_ref[pl.ds(c * GMLP_CHUNK, GMLP_CHUNK), :] = (u[c * GMLP_CHUNK:(c + 1) * GMLP_CHUNK] * mixed).astype(BF16)


def _in_proj_call(x, sc, sh, w_all, cw, cb, cpast, lng, lnb, wm, sb, *, shift, tm, transposed, name):
    rows, d_model = x.shape
    pad = max(8, 2 * shift)
    mr = sc.shape[0]
    mod_spec = (pl.BlockSpec((1, d_model), lambda i: (0, 0)) if mr == 1
                else pl.BlockSpec((tm, d_model), lambda i: (i, 0)))

    def row_spec(n):
        return pl.BlockSpec((tm, n), lambda i: (i, 0))

    def col_spec(n):
        return pl.BlockSpec((n, tm), lambda i: (0, i))

    def full(a):
        nd = a.ndim
        return pl.BlockSpec(a.shape, lambda i: (0,) * nd)

    sds = jax.ShapeDtypeStruct
    out_shape = [sds((rows, 256), BF16), sds((rows, 256), BF16), sds((2 * shift, 256), F32)]
    out_specs = [row_spec(256), row_spec(256), pl.BlockSpec((2 * shift, 256), lambda i: (0, 0))]
    if transposed:
        out_shape += [sds((N_HEADS * LANES, rows), BF16), sds((_N_SLOT * LANES, rows), F32)]
        out_specs += [col_spec(N_HEADS * LANES), col_spec(_N_SLOT * LANES)]
        out_shape += [sds((rows, LANES), BF16)] * 4 + [sds((LANES, rows), BF16)] * 2 + [sds((LANES, rows), F32)]
        out_specs += [row_spec(LANES)] * 4 + [col_spec(LANES)] * 3
    else:
        out_shape += [sds((rows, 1024), BF16), sds((rows, _N_SLOT * LANES), F32), sds((rows, LANES), BF16),
                      sds((rows, LANES), BF16), sds((rows, LANES), F32), sds((rows, 256), F32)]
        out_specs += [row_spec(1024), row_spec(_N_SLOT * LANES), row_spec(LANES), row_spec(LANES), row_spec(LANES),
                      row_spec(256)]
    return pl.pallas_call(
        functools.partial(_in_proj_kernel, shift=shift, tm=tm, pad=pad, transposed=transposed),
        grid=(rows // tm,),
        in_specs=[row_spec(d_model), mod_spec, mod_spec, full(w_all), full(cw), full(cb), full(cpast),
                  full(lng), full(lnb), full(wm), full(sb)],
        out_specs=out_specs,
        out_shape=out_shape,
        scratch_shapes=[pltpu.VMEM((pad + tm, 256), F32)],
        compiler_params=pltpu.CompilerParams(dimension_semantics=("arbitrary",), vmem_limit_bytes=VMEM_LIMIT),
        name=name,
    )(x, sc, sh, w_all, cw, cb, cpast, lng, lnb, wm, sb)


def _pe_term_kernel(pe_ref, w1_ref, o_ref):
    for s in range(2):
        o_ref[s] = _dot3(pe_ref[s], w1_ref[s])


def _pe_term_call(pe_flat, w1_flat):
    return pl.pallas_call(
        _pe_term_kernel,
        out_shape=jax.ShapeDtypeStruct((2, 8, CMP_HID), F32),
        compiler_params=pltpu.CompilerParams(vmem_limit_bytes=VMEM_LIMIT),
        name="cmp_pe_term",
    )(pe_flat, w1_flat)


def _compress_prompt_kernel(kx_ref, vx_ref, w1_ref, pet_ref, w2_ref, kc_ref, vct_ref, sh_ref, *, nc):
    sh_ref[pl.ds(0, 8), :] = jnp.zeros((8, CMP_HID), F32)
    for s, x_ref in enumerate((kx_ref, vx_ref)):
        parts = _dot(x_ref[...], w1_ref[s])
        acc = jnp.zeros((nc, LANES), F32)
        for g in range(N_KV_HEADS):
            p0 = parts[:, g * 256:g * 256 + CMP_HID]
            p1 = parts[:, g * 256 + CMP_HID:(g + 1) * 256]
            sh_ref[pl.ds(8, nc), :] = p0
            p0s = sh_ref[pl.ds(7, nc), :]
            hid = _gelu(p0s + p1 + pet_ref[s][0:1, :])
            acc = acc + _dot(hid.astype(BF16), w2_ref[s, g])
        if s == 0:
            kc_ref[...] = acc.astype(BF16)
        else:
            vct_ref[...] = acc.T.astype(BF16)


def _compress_prompt_call(kx, vx, w1p, pet, w2p):
    nc = kx.shape[0]
    return pl.pallas_call(
        functools.partial(_compress_prompt_kernel, nc=nc),
        out_shape=[jax.ShapeDtypeStruct((nc, LANES), BF16), jax.ShapeDtypeStruct((LANES, nc), BF16)],
        scratch_shapes=[pltpu.VMEM((nc + 8, CMP_HID), F32)],
        compiler_params=pltpu.CompilerParams(vmem_limit_bytes=VMEM_LIMIT),
        name="compress_prompt",
    )(kx, vx, w1p, pet, w2p)


def _attn_prompt_kernel(qt_ref, gatet_ref, kc_ref, vct_ref, covert_ref, ksl_ref, vslt_ref, kwn_ref, vwnt_ref, eb_ref,
                        o_ref, ns_ref, rhs_ref, sa_ref, sb_ref, *, nc, ns):
    qb = pl.program_id(0)
    s0 = qb * Q_BLOCK
    ncol = GQA * Q_BLOCK
    pair = 2 * Q_BLOCK
    col = lax.broadcasted_iota(jnp.int32, (1, ncol), 1)
    qpos_c = s0 + (col & (Q_BLOCK - 1))
    qp = s0 + lax.broadcasted_iota(jnp.int32, (1, Q_BLOCK), 1)
    drow = lax.broadcasted_iota(jnp.int32, (LANES, 1), 0)
    m_idx = lax.broadcasted_iota(jnp.int32, (nc, 1), 0)
    blk = lax.broadcasted_iota(jnp.int32, (ns, 1), 0)
    blk_f = blk.astype(F32)
    key_row = lax.broadcasted_iota(jnp.int32, (KEY_TILE, 1), 0)
    n_tiles = s0 // KEY_TILE + 1

    ns_ref[pl.ds(ns, 8), :] = jnp.zeros((8, Q_BLOCK), F32)
    rhs_ref[pl.ds(LANES + NS_ROWS, 2 * LANES - LANES - NS_ROWS), :] = jnp.zeros((LANES - NS_ROWS, ncol), BF16)

    for g in (pl.program_id(1),):
        rq = jnp.concatenate([qt_ref[pl.ds(h * LANES, LANES), :] for h in range(GQA)], axis=1)
        rhs_ref[pl.ds(0, LANES), :] = rq

        sc = _dot(kc_ref[...], rq)
        cmask = (m_idx >= 1) & (CMP_STRIDE * (m_idx - 1) + CMP_LEN - 1 <= qpos_c)
        p = _masked_softmax(sc, cmask, 0)
        o_cmp = _dot(vct_ref[...], p.astype(BF16))

        p4 = (p[:, 0:Q_BLOCK] + p[:, Q_BLOCK:2 * Q_BLOCK] + p[:, 2 * Q_BLOCK:3 * Q_BLOCK]
              + p[:, 3 * Q_BLOCK:4 * Q_BLOCK])
        hi, lo = _split(p4)
        imp = _dot(covert_ref[...], hi) + _dot(covert_ref[...], lo)
        qblk = qp // SLC_BLOCK
        elig = blk * SLC_BLOCK <= qp
        forced = (blk == 0) | (blk == qblk) | (blk == qblk - 1)
        val = jnp.where(elig, jnp.where(forced, FORCE, imp), -FORCE)
        sel, _, _ = _top_select(val, blk_f, ns, 0)
        ns_ref[pl.ds(0, ns), :] = jnp.where(elig & (sel > 0.5), 0.0, 1.0)

        def tile_scores(kt):
            k0 = pl.multiple_of(kt * KEY_TILE, KEY_TILE)
            nsf = ns_ref[pl.ds(pl.multiple_of(kt * BLOCKS_PER_TILE, BLOCKS_PER_TILE), NS_ROWS), :].astype(BF16)
            rhs_ref[pl.ds(LANES, NS_ROWS), :] = jnp.concatenate([nsf] * GQA, axis=1)
            lhs = jnp.concatenate([ksl_ref[pl.ds(k0, KEY_TILE), :], eb_ref[...]], axis=1)
            return _dot(lhs, rhs_ref[...])

        def flash_update(s, k0, carry):
            m_run, l_run, acc = carry
            m_new = jnp.maximum(m_run, jnp.max(s, axis=0, keepdims=True))
            alpha = jnp.exp2(m_run - m_new)
            pe = jnp.exp2(s - m_new)
            l_new = alpha * l_run + jnp.sum(pe, axis=0, keepdims=True)
            acc_new = alpha * acc + _dot(vslt_ref[:, pl.ds(k0, KEY_TILE)], pe.astype(BF16))
            return m_new, l_new, acc_new

        def update_from(src_ref, kt, carry):
            k0 = pl.multiple_of(kt * KEY_TILE, KEY_TILE)
            return tuple(flash_update(src_ref[:, pl.ds(hp * pair, pair)], k0, carry[hp]) for hp in range(GQA // 2))

        sa_ref[...] = tile_scores(0)

        def pair_step(j, carry):
            kt = 2 * j
            sb_ref[...] = tile_scores(kt + 1)
            carry = update_from(sa_ref, kt, carry)
            sa_ref[...] = tile_scores(kt + 2)
            return update_from(sb_ref, kt + 1, carry)

        init = (jnp.full((1, pair), NEG_INF, F32), jnp.zeros((1, pair), F32), jnp.zeros((LANES, pair), F32))
        last = n_tiles - 1
        carry = lax.fori_loop(0, last // 2, pair_step, (init,) * (GQA // 2))

        def odd_step(carry):
            carry = update_from(sa_ref, last - 1, carry)
            sa_ref[...] = tile_scores(last)
            return carry

        carry = lax.cond(lax.rem(last, 2) == 1, odd_step, lambda c: c, carry)
        k0 = pl.multiple_of(last * KEY_TILE, KEY_TILE)
        o_slc_parts = []
        for hp in range(GQA // 2):
            s = sa_ref[:, pl.ds(hp * pair, pair)]
            s = jnp.where(k0 + key_row <= qpos_c[:, hp * pair:(hp + 1) * pair], s, NEG_INF)
            _, l_fin, acc = flash_update(s, k0, carry[hp])
            o_slc_parts.append(acc * (1.0 / l_fin))
        o_slc = jnp.concatenate(o_slc_parts, axis=1)

        w0 = pl.multiple_of(jnp.maximum(s0 - WINDOW, 0), Q_BLOCK)
        dist = qpos_c - (w0 + lax.broadcasted_iota(jnp.int32, (WINDOW + Q_BLOCK, 1), 0))
        pw = _masked_softmax(_dot(kwn_ref[pl.ds(w0, WINDOW + Q_BLOCK), :], rq), (dist >= 0) & (dist <= WINDOW), 0)
        o_win = _dot(vwnt_ref[:, pl.ds(w0, WINDOW + Q_BLOCK)], pw.astype(BF16))

        keep = (drow >= g * HEAD_DIM) & (drow < (g + 1) * HEAD_DIM)
        for h in range(GQA):
            c = slice(h * Q_BLOCK, (h + 1) * Q_BLOCK)
            gr = (GQA * g + h) * N_BRANCH
            o = (gatet_ref[pl.ds(gr, 1), :] * o_cmp[:, c] + gatet_ref[pl.ds(gr + 1, 1), :] * o_slc[:, c]
                 + gatet_ref[pl.ds(gr + 2, 1), :] * o_win[:, c])
            o_ref[:, h * LANES:(h + 1) * LANES] = jnp.where(keep, o, 0.0).T.astype(BF16)


def _attn_prompt_call(qt, gatet, kc, vct, covert, ksl, vslt, kwn, vwnt, ebias):
    t = qt.shape[1]
    ns, nc = covert.shape
    vmem = pl.BlockSpec(memory_space=pltpu.VMEM)
    return pl.pallas_call(
        functools.partial(_attn_prompt_kernel, nc=nc, ns=ns),
        grid=(t // Q_BLOCK, N_KV_HEADS),
        in_specs=[pl.BlockSpec((GQA * LANES, Q_BLOCK), lambda i, g: (g, i)),
                  pl.BlockSpec((LANES, Q_BLOCK), lambda i, g: (0, i)),
                  vmem, vmem, vmem, vmem, vmem, vmem, vmem, vmem],
        out_specs=pl.BlockSpec((Q_BLOCK, GQA * LANES), lambda i, g: (i, g)),
        out_shape=jax.ShapeDtypeStruct((t, N_HEADS * LANES), BF16),
        scratch_shapes=[pltpu.VMEM((ns + 8, Q_BLOCK), F32), pltpu.VMEM((2 * LANES, GQA * Q_BLOCK), BF16),
                        pltpu.VMEM((KEY_TILE, GQA * Q_BLOCK), F32), pltpu.VMEM((KEY_TILE, GQA * Q_BLOCK), F32)],
        compiler_params=pltpu.CompilerParams(dimension_semantics=("arbitrary", "arbitrary"),
                                             vmem_limit_bytes=VMEM_LIMIT),
        name="attn_prompt",
    )(qt, gatet, kc, vct, covert, ksl, vslt, kwn, vwnt, ebias)


def _out_ffn_kernel(x_ref, oa_ref, ob_ref, oc_ref, g1_ref, sc2_ref, sh2_ref, g2_ref, wo_ref, lng_ref, lnb_ref,
                    wup_ref, cfw_ref, cfb_ref, fpast_ref, wdn_ref, xo_ref, fstate_ref, us_ref,
                    *, shift, tm, pad, alpha, d_ff):
    i = pl.program_id(0)

    @pl.when(i == 0)
    def _():
        us_ref[pl.ds(pad - 2 * shift, 2 * shift), :] = fpast_ref[...]

    mix = _dot(jnp.concatenate([oa_ref[...], ob_ref[...], oc_ref[...]], axis=1), wo_ref[...])
    x1 = _layer_norm(alpha * x_ref[...] + (1.0 + g1_ref[...]) * mix, lng_ref[0:1, :], lnb_ref[0:1, :])
    h2 = (x1 * (1.0 + sc2_ref[...]) + sh2_ref[...]).astype(BF16)
    up = _dot(h2, wup_ref[...])
    ua = up[:, :d_ff]
    us_ref[pl.ds(pad, tm), :] = ua
    yc = _shifted_conv(us_ref, ua, cfw_ref, cfb_ref, pad, shift, tm)
    tail = us_ref[pl.ds(pad + tm - 2 * shift, 2 * shift), :]
    fstate_ref[...] = tail
    us_ref[pl.ds(pad - 2 * shift, 2 * shift), :] = tail
    act = (yc * _sigmoid(yc) * up[:, d_ff:]).astype(BF16)
    y = _dot(act, wdn_ref[...])
    xo_ref[...] = _layer_norm(alpha * x1 + (1.0 + g2_ref[...]) * y, lng_ref[1:2, :], lnb_ref[1:2, :])


def _out_ffn_call(x, oa, ob, oc, g1, sc2, sh2, g2, wo, lng, lnb, wup, cfw, cfb, fpast, wdn, *, shift, tm, alpha, name):
    rows, d_model = x.shape
    d_ff = wdn.shape[0]
    pad = max(8, 2 * shift)
    mr = g1.shape[0]
    mod_spec = (pl.BlockSpec((1, d_model), lambda i: (0, 0)) if mr == 1
                else pl.BlockSpec((tm, d_model), lambda i: (i, 0)))

    def row_spec(n):
        return pl.BlockSpec((tm, n), lambda i: (i, 0))

    vmem = pl.BlockSpec(memory_space=pltpu.VMEM)
    return pl.pallas_call(
        functools.partial(_out_ffn_kernel, shift=shift, tm=tm, pad=pad, alpha=alpha, d_ff=d_ff),
        grid=(rows // tm,),
        in_specs=[row_spec(d_model), row_spec(256), row_spec(1024), row_spec(256), mod_spec, mod_spec, mod_spec,
                  mod_spec, vmem, vmem, vmem, vmem, vmem, vmem, vmem, vmem],
        out_specs=[row_spec(d_model), pl.BlockSpec((2 * shift, d_ff), lambda i: (0, 0))],
        out_shape=[jax.ShapeDtypeStruct((rows, d_model), F32), jax.ShapeDtypeStruct((2 * shift, d_ff), F32)],
        scratch_shapes=[pltpu.VMEM((pad + tm, d_ff), F32)],
        compiler_params=pltpu.CompilerParams(dimension_semantics=("arbitrary",), vmem_limit_bytes=VMEM_LIMIT),
        name=name,
    )(x, oa, ob, oc, g1, sc2, sh2, g2, wo, lng, lnb, wup, cfw, cfb, fpast, wdn)


def _cmp_stream_kernel(pt_ref, cache_ref, w1_ref, pet_ref, w2_ref, kc_ref, vc_ref, buf, rbuf, sem, carry,
                       *, page_base, n_pages, pg, n_groups, total):
    b = pl.program_id(0)
    gi = pl.program_id(1)
    step = b * n_groups + gi
    slot = lax.rem(step, 2)
    m = pg * (PAGE_SIZE // CMP_STRIDE)

    def page_copy(page, slt, i):
        return pltpu.make_async_copy(cache_ref.at[page, pl.ds(0, 2 * LANES), :], buf.at[slt, i], sem.at[slt])

    def issue(stp, slt):
        base = lax.div(stp, n_groups) * n_pages + lax.rem(stp, n_groups) * pg
        for i in range(pg):
            page_copy(page_base + pt_ref[base + i], slt, i).start()

    @pl.when(step == 0)
    def _():
        carry[...] = jnp.zeros(carry.shape, F32)
        issue(step, slot)

    @pl.when(step + 1 < total)
    def _():
        issue(step + 1, 1 - slot)

    for i in range(pg):
        page_copy(0, slot, i).wait()

    for i in range(pg):
        for s in range(2):
            rbuf[s, pl.ds(i * PAGE_SIZE, PAGE_SIZE), :] = buf[slot, i, pl.ds(s * LANES, LANES), :].T

    lane = lax.broadcasted_iota(jnp.int32, (1, LANES), 1)
    first_half = lane < HEAD_DIM
    row0 = lax.broadcasted_iota(jnp.int32, (m, 1), 0) == 0
    pieces = [[[], []], [[], []]]
    for pr in range(CMP_STRIDE // 2):
        for s in range(2):
            a = rbuf[s, pl.ds(2 * pr, m, stride=CMP_STRIDE), :]
            bb = rbuf[s, pl.ds(2 * pr + 1, m, stride=CMP_STRIDE), :]
            pieces[s][0].append(jnp.where(first_half, a, pltpu.roll(bb, HEAD_DIM, 1)).astype(BF16))
            pieces[s][1].append(jnp.where(first_half, pltpu.roll(a, HEAD_DIM, 1), bb).astype(BF16))
    for s, o_ref in enumerate((kc_ref, vc_ref)):
        acc = jnp.zeros((m, LANES), F32)
        for g in range(N_KV_HEADS):
            parts = _dot(jnp.concatenate(pieces[s][g], axis=1), w1_ref[s])
            p0 = parts[:, :CMP_HID]
            p1 = parts[:, CMP_HID:]
            prev = jnp.where(gi == 0, 0.0, carry[s * 2 + g][0:1, :])
            p0s = jnp.where(row0, prev, pltpu.roll(p0, 1, 0))
            carry[s * 2 + g] = jnp.broadcast_to(p0[m - 1:m, :], (8, CMP_HID))
            hid = _gelu(p0s + p1 + pet_ref[s][0:1, :])
            acc = acc + _dot(hid.astype(BF16), w2_ref[s, g])
        o_ref[0] = acc.astype(BF16)


def _cmp_stream_call(pt_flat, cache_t, w1s, pet, w2p, *, layer, n_phys, batch, n_pages, pg):
    n_groups = n_pages // pg
    m = pg * (PAGE_SIZE // CMP_STRIDE)
    nc = n_pages * (PAGE_SIZE // CMP_STRIDE)
    total = batch * n_groups

    def full(a):
        nd = a.ndim
        return pl.BlockSpec(a.shape, lambda b, g, pt: (0,) * nd)

    grid_spec = pltpu.PrefetchScalarGridSpec(
        num_scalar_prefetch=1,
        grid=(batch, n_groups),
        in_specs=[pl.BlockSpec(memory_space=pl.ANY), full(w1s), full(pet), full(w2p)],
        out_specs=[pl.BlockSpec((1, m, LANES), lambda b, g, pt: (b, g, 0))] * 2,
        scratch_shapes=[pltpu.VMEM((2, pg, 2 * LANES, PAGE_SIZE), F32), pltpu.VMEM((2, pg * PAGE_SIZE, LANES), F32),
                        pltpu.SemaphoreType.DMA((2,)), pltpu.VMEM((4, 8, CMP_HID), F32)],
    )
    return pl.pallas_call(
        functools.partial(_cmp_stream_kernel, page_base=layer * n_phys, n_pages=n_pages, pg=pg,
                          n_groups=n_groups, total=total),
        grid_spec=grid_spec,
        out_shape=[jax.ShapeDtypeStruct((batch, nc, LANES), BF16)] * 2,
        compiler_params=pltpu.CompilerParams(dimension_semantics=("arbitrary", "arbitrary"),
                                             vmem_limit_bytes=VMEM_LIMIT),
        name="cmp_stream_sample",
    )(pt_flat, cache_t, w1s, pet, w2p)


def _cmp_attn_sample_kernel(q_ref, kc_ref, vc_ref, cover_ref, ocmp_ref, idx_ref, bias_ref,
                            *, past, nc, ns, nsp, n_cache_blocks):
    rows = lax.broadcasted_iota(jnp.int32, (32, 1), 0)
    qpos_r = past + (rows & 7)
    qp = past + lax.broadcasted_iota(jnp.int32, (8, 1), 0)
    m_idx = lax.broadcasted_iota(jnp.int32, (1, nc), 1)
    blk = lax.broadcasted_iota(jnp.int32, (1, nsp), 1)
    blk_f = blk.astype(F32)
    lane = lax.broadcasted_iota(jnp.int32, (1, LANES), 1)
    cmask = (m_idx >= 1) & (CMP_STRIDE * (m_idx - 1) + CMP_LEN - 1 <= qpos_r)
    for g in range(N_KV_HEADS):
        p = _masked_softmax(_dot_nt(q_ref[0, g], kc_ref[0]), cmask, -1)
        ocmp_ref[0, g] = _dot(p.astype(BF16), vc_ref[0])
        p4 = p[0:8] + p[8:16] + p[16:24] + p[24:32]
        hi, lo = _split(p4)
        imp = _dot(hi, cover_ref[...]) + _dot(lo, cover_ref[...])
        qblk = qp // SLC_BLOCK
        elig = blk * SLC_BLOCK <= qp
        forced = (blk == 0) | (blk == qblk) | (blk == qblk - 1)
        val = jnp.where(blk < ns, jnp.where(elig, jnp.where(forced, FORCE, imp), -FORCE), REMOVED)
        _, firsts, tops = _top_select(val, blk_f, nsp, -1)
        idx = jnp.zeros((8, LANES), F32)
        bias = jnp.zeros((8, LANES), F32)
        for t in range(N_SELECT):
            ok = (tops[t] > -0.5 * FORCE) & (firsts[t] < float(n_cache_blocks))
            odd = firsts[t] - 2.0 * jnp.floor(firsts[t] * 0.5)
            idx = jnp.where(lane == t, firsts[t], idx)
            for hf in range(2):
                bias = jnp.where(lane == 2 * t + hf, jnp.where(ok & (odd == float(hf)), 0.0, NEG_INF), bias)
        idx_ref[0, g] = idx.astype(jnp.int32)
        bias_ref[0, g] = bias


def _cmp_attn_sample_call(q_hq, kc, vc, cover, *, past, ns, n_cache_blocks):
    batch, nc, _ = kc.shape
    nsp = cover.shape[1]
    blk4 = lambda r: pl.BlockSpec((1, N_KV_HEADS, r, LANES), lambda b: (b, 0, 0, 0))
    return pl.pallas_call(
        functools.partial(_cmp_attn_sample_kernel, past=past, nc=nc, ns=ns, nsp=nsp, n_cache_blocks=n_cache_blocks),
        grid=(batch,),
        in_specs=[blk4(32), pl.BlockSpec((1, nc, LANES), lambda b: (b, 0, 0)),
                  pl.BlockSpec((1, nc, LANES), lambda b: (b, 0, 0)), pl.BlockSpec(cover.shape, lambda b: (0, 0))],
        out_specs=[blk4(32), blk4(8), blk4(8)],
        out_shape=[jax.ShapeDtypeStruct((batch, N_KV_HEADS, 32, LANES), F32),
                   jax.ShapeDtypeStruct((batch, N_KV_HEADS, 8, LANES), jnp.int32),
                   jax.ShapeDtypeStruct((batch, N_KV_HEADS, 8, LANES), F32)],
        compiler_params=pltpu.CompilerParams(dimension_semantics=("arbitrary",), vmem_limit_bytes=VMEM_LIMIT),
        name="cmp_attn_sample",
    )(q_hq, kc, vc, cover)


def _sel_attn_sample_kernel(idx_ref, pt_ref, cache_ref, q_ref, bias_ref, ex_ref, knew_ref, vnew_ref, wint_ref,
                            wnew_ref, wnewt_ref, ocmp_ref, gate_ref, o_ref, wout_ref, buf, sem,
                            *, page_base, n_pages, tq, n_cache_blocks, total):
    b = pl.program_id(0)
    g = pl.program_id(1)
    step = b * N_KV_HEADS + g
    slot = lax.rem(step, 2)
    n_ent = tq * N_SELECT
    blocks_per_page = PAGE_SIZE // SLC_BLOCK

    def page_copy(page, slt, e):
        return pltpu.make_async_copy(cache_ref.at[page, pl.ds(2 * LANES, 2 * LANES), :], buf.at[slt, e], sem.at[slt])

    def issue(stp, slt):
        bb = lax.div(stp, N_KV_HEADS)

        def body(e, _):
            blk = jnp.minimum(idx_ref[stp * n_ent + e], n_cache_blocks - 1)
            page = pt_ref[bb * n_pages + lax.div(blk, blocks_per_page)]
            page_copy(page_base + page, slt, e).start()
            return 0
        lax.fori_loop(0, n_ent, body, 0)

    @pl.when(step == 0)
    def _():
        issue(step, slot)

    @pl.when(step + 1 < total)
    def _():
        issue(step + 1, 1 - slot)

    def wait_body(e, _):
        page_copy(0, slot, e).wait()
        return 0
    lax.fori_loop(0, n_ent, wait_body, 0)

    lane = lax.broadcasted_iota(jnp.int32, (1, LANES), 1)
    col8 = lax.broadcasted_iota(jnp.int32, (1, 8), 1)
    wcol = lax.broadcasted_iota(jnp.int32, (1, WINDOW), 1)
    k_wt = wint_ref[0, pl.ds(0, LANES), :].astype(BF16)
    v_wt = wint_ref[0, pl.ds(LANES, LANES), :].astype(BF16)
    wnew = wnew_ref[0]
    k_wn = wnew[:, 0:LANES].astype(BF16)
    v_wn = wnew[:, LANES:2 * LANES].astype(BF16)
    bias_all = _dot(bias_ref[0, 0].astype(BF16), ex_ref[...])
    keep = (lane >= g * HEAD_DIM) & (lane < (g + 1) * HEAD_DIM)
    for qi in range(tq):
        q4 = q_ref[0, 0, qi]
        new_ok = (col8 <= qi) & (col8 < tq)
        k_t = jnp.concatenate([buf[slot, qi * N_SELECT + k, pl.ds(0, LANES), :] for k in range(N_SELECT)], axis=1)
        v_t = jnp.concatenate([buf[slot, qi * N_SELECT + k, pl.ds(LANES, LANES), :] for k in range(N_SELECT)], axis=1)
        s = _dot(q4, k_t.astype(BF16)) + bias_all[qi:qi + 1, :]
        s_n = jnp.where(new_ok, _dot_nt(q4, knew_ref[0]), NEG_INF)
        mx = jnp.maximum(jnp.max(s, axis=-1, keepdims=True), jnp.max(s_n, axis=-1, keepdims=True))
        pe = jnp.exp2(s - mx)
        pn = jnp.exp2(s_n - mx)
        l = jnp.sum(pe, axis=-1, keepdims=True) + jnp.sum(pn, axis=-1, keepdims=True)
        o_slc = (_dot_nt(pe.astype(BF16), v_t.astype(BF16)) + _dot(pn.astype(BF16), vnew_ref[0])) * (1.0 / l)
        sw = jnp.where(wcol >= qi, _dot(q4, k_wt), NEG_INF)
        sw_n = jnp.where(new_ok, _dot_nt(q4, k_wn), NEG_INF)
        mw = jnp.maximum(jnp.max(sw, axis=-1, keepdims=True), jnp.max(sw_n, axis=-1, keepdims=True))
        pw = jnp.exp2(sw - mw)
        pwn = jnp.exp2(sw_n - mw)
        lw = jnp.sum(pw, axis=-1, keepdims=True) + jnp.sum(pwn, axis=-1, keepdims=True)
        o_win = (_dot_nt(pw.astype(BF16), v_wt) + _dot(pwn.astype(BF16), v_wn)) * (1.0 / lw)
        gt = gate_ref[0, 0, qi]
        o = gt[:, 0:1] * ocmp_ref[0, 0, qi] + gt[:, 1:2] * o_slc + gt[:, 2:3] * o_win
        o_ref[0, 0, qi] = jnp.where(keep, o, 0.0)

    @pl.when(g == 0)
    def _():
        shifted = pltpu.roll(wint_ref[0], WINDOW - tq, 1)
        wout_ref[0, :, pl.ds(0, WINDOW - LANES)] = shifted[:, 0:WINDOW - LANES]
        wout_ref[0, :, pl.ds(WINDOW - LANES, LANES)] = jnp.where(lane >= LANES - tq, wnewt_ref[0],
                                                                  shifted[:, WINDOW - LANES:WINDOW])


def _sel_attn_sample_call(idx_flat, pt_flat, cache_t, q_qh, bias, expand, knew, vnew, win_t, wnew, wnew_t, ocmp_qh,
                          gate_qh, *, layer, n_phys, n_pages, tq, n_cache_blocks):
    batch = q_qh.shape[0]
    n_ent = tq * N_SELECT
    b5 = lambda: pl.BlockSpec((1, 1, tq, 8, LANES), lambda b, g, i, p: (b, g, 0, 0, 0))
    b3 = lambda r, c: pl.BlockSpec((1, r, c), lambda b, g, i, p: (b, 0, 0))
    grid_spec = pltpu.PrefetchScalarGridSpec(
        num_scalar_prefetch=2,
        grid=(batch, N_KV_HEADS),
        in_specs=[pl.BlockSpec(memory_space=pl.ANY), b5(),
                  pl.BlockSpec((1, 1, 8, LANES), lambda b, g, i, p: (b, g, 0, 0)),
                  pl.BlockSpec(expand.shape, lambda b, g, i, p: (0, 0)),
                  b3(8, LANES), b3(8, LANES), b3(2 * LANES, WINDOW), b3(8, 2 * LANES), b3(2 * LANES, LANES),
                  b5(), b5()],
        out_specs=[b5(), b3(2 * LANES, WINDOW)],
        scratch_shapes=[pltpu.VMEM((2, n_ent, 2 * LANES, PAGE_SIZE), F32), pltpu.SemaphoreType.DMA((2,))],
    )
    return pl.pallas_call(
        functools.partial(_sel_attn_sample_kernel, page_base=layer * n_phys, n_pages=n_pages, tq=tq,
                          n_cache_blocks=n_cache_blocks, total=batch * N_KV_HEADS),
        grid_spec=grid_spec,
        out_shape=[jax.ShapeDtypeStruct((batch, N_KV_HEADS, tq, 8, LANES), F32),
                   jax.ShapeDtypeStruct((batch, 2 * LANES, WINDOW), F32)],
        compiler_params=pltpu.CompilerParams(dimension_semantics=("arbitrary", "arbitrary"),
                                             vmem_limit_bytes=VMEM_LIMIT),
        name="sel_attn_sample",
    )(idx_flat, pt_flat, cache_t, q_qh, bias, expand, knew, vnew, win_t, wnew, wnew_t, ocmp_qh, gate_qh)


def _prep_w_in(w):
    d = w.shape[0]
    wq = (w[:, 768:1280] * (HEAD_DIM ** -0.5 * LOG2E)).reshape(d, N_HEADS, HEAD_DIM)
    z = jnp.zeros_like(wq)
    grp = (jnp.arange(N_HEADS) // GQA)[None, :, None]
    wq = jnp.concatenate([jnp.where(grp == 0, wq, z), jnp.where(grp == 1, wq, z)], axis=-1).reshape(d, N_HEADS * LANES)
    gate = jnp.pad(w[:, 2048:2072], ((0, 0), (0, LANES - N_HEADS * N_BRANCH)))
    return jnp.concatenate([w[:, :768], wq, w[:, 1280:2048], gate, w[:, 2072:]], axis=1).astype(BF16)


def _prep_w_o(w):
    d = w.shape[1]
    wb = w[256:768].reshape(N_HEADS, HEAD_DIM, d)
    z = jnp.zeros_like(wb)
    grp = (jnp.arange(N_HEADS) // GQA)[:, None, None]
    wb = jnp.concatenate([jnp.where(grp == 0, wb, z), jnp.where(grp == 1, wb, z)], axis=1).reshape(N_HEADS * LANES, d)
    return jnp.concatenate([w[:256], wb, w[768:]], axis=0).astype(BF16)


def _prep_w1(w1):
    w = w1.reshape(2, 2, CMP_STRIDE, HEAD_DIM, CMP_HID)
    return jnp.transpose(w, (0, 2, 3, 1, 4)).reshape(2, CMP_STRIDE * HEAD_DIM, 2 * CMP_HID)


def _prep_w1_grouped(w1s):
    w = w1s.reshape(2, CMP_STRIDE, 1, HEAD_DIM, 1, 2 * CMP_HID)
    eye = jnp.eye(N_KV_HEADS, dtype=w.dtype).reshape(1, 1, N_KV_HEADS, 1, N_KV_HEADS, 1)
    return (w * eye).reshape(2, CMP_STRIDE * N_KV_HEADS * HEAD_DIM, N_KV_HEADS * 2 * CMP_HID)


def _prep_w2(w2):
    z = jnp.zeros_like(w2)
    return jnp.stack([jnp.concatenate([w2, z], axis=-1), jnp.concatenate([z, w2], axis=-1)], axis=1).astype(BF16)


def _cover_matrix(nc, ns_real, ns_pad):
    m = np.arange(nc)[:, None]
    b = np.arange(ns_pad)[None, :]
    return ((m >= 4 * b) & (m <= 4 * b + 4) & (m >= 1) & (b < ns_real)).astype(np.float32)


def _block_bias_matrix():
    k = np.arange(KEY_TILE)[:, None]
    b = np.arange(LANES)[None, :]
    return jnp.asarray(np.where(k // SLC_BLOCK == b, NEG_INF, 0.0).astype(np.float32), dtype=BF16)


def _expand_matrix():
    r = np.arange(LANES)[:, None]
    c = np.arange(N_SELECT * PAGE_SIZE)[None, :]
    return jnp.asarray((c // SLC_BLOCK == r).astype(np.float32), dtype=BF16)


def kernel(x_prompt, x_sample, cache_nsa_kv, state_win_kv, state_conv, state_ffn_conv, page_table, c_prompt, c_sample, w_ada, b_ada, w_in, conv_a_w, conv_a_b, cmp_pe, cmp_w1, cmp_w2, sgu_ln_g, sgu_ln_b, sgu_w, sgu_b, w_o, ln_g, ln_b, w_ffn_up, conv_f_w, conv_f_b, w_ffn_down):
    depth = w_in.shape[0]
    _, t, d_model = x_prompt.shape
    nb, tq, _ = x_sample.shape
    n_phys = cache_nsa_kv.shape[1]
    n_pages = page_table.shape[1]
    past = n_pages * PAGE_SIZE
    d_ff = w_ffn_down.shape[1]
    alpha = (2 * depth) ** 0.25
    rs = nb * tq
    kvw = 4 * N_KV_HEADS * HEAD_DIM
    assert x_prompt.shape[0] == 1 and c_prompt.shape[0] == 1
    assert d_model == 1024 and t % KEY_TILE == 0 and t >= WINDOW + Q_BLOCK
    assert tq == 4 and rs == GMLP_CHUNK and past % KEY_TILE == 0 and past >= WINDOW
    assert state_win_kv.shape[2] == WINDOW

    rc = -(-(1 + nb) // 8) * 8
    c_all = jnp.pad(jnp.concatenate([c_prompt, c_sample], axis=0), ((0, rc - 1 - nb), (0, 0)))
    mods = _ada_call(c_all, w_ada, b_ada)

    nc_p, ns_p = t // CMP_STRIDE, t // SLC_BLOCK
    covert_p = jnp.asarray(_cover_matrix(nc_p, ns_p, ns_p).T, dtype=BF16)
    nc_s = (past + tq) // CMP_STRIDE
    ns_s = -(-(past + tq) // SLC_BLOCK)
    ns_s_pad = -(-ns_s // LANES) * LANES
    n_cache_blocks = past // SLC_BLOCK
    cover_s = jnp.asarray(_cover_matrix(nc_s, ns_s, ns_s_pad), dtype=BF16)
    ebias = _block_bias_matrix()
    expand = _expand_matrix()
    pt_flat = page_table.reshape(-1)
    cache_t = jnp.transpose(cache_nsa_kv.reshape(depth * n_phys, PAGE_SIZE, kvw), (0, 2, 1))
    win_t_all = jnp.transpose(state_win_kv.reshape(depth, nb, WINDOW, 2 * LANES), (0, 1, 3, 2))

    xp = x_prompt[0]
    xs = jnp.transpose(x_sample, (1, 0, 2)).reshape(rs, d_model)
    tril_full = jnp.tril(jnp.ones((GMLP_CHUNK, GMLP_CHUNK), F32))
    tril_tq = jnp.tril(jnp.ones((tq, tq), F32))

    ps, ss = [], []
    for l in range(depth):
        w_all = _prep_w_in(w_in[l])
        wo_p = _prep_w_o(w_o[l])
        wup = w_ffn_up[l].astype(BF16)
        wdn = w_ffn_down[l].astype(BF16)
        w1s = _prep_w1(cmp_w1[l])
        w1g = _prep_w1_grouped(w1s).astype(BF16)
        w1s = w1s.astype(BF16)
        w2p = _prep_w2(cmp_w2[l])
        pe_flat = jnp.broadcast_to(cmp_pe[l].reshape(2, 1, CMP_LEN * HEAD_DIM), (2, 8, CMP_LEN * HEAD_DIM))
        pet = _pe_term_call(pe_flat, cmp_w1[l].reshape(2, CMP_LEN * HEAD_DIM, CMP_HID))
        cw, cb = conv_a_w[l], conv_a_b[l].reshape(1, -1)
        cfw, cfb = conv_f_w[l], conv_f_b[l].reshape(1, -1)
        lng, lnb = sgu_ln_g[l].reshape(1, -1), sgu_ln_b[l].reshape(1, -1)

        def mod_rows(r0, r1, rep):
            parts = [mods[l, r0:r1, k * d_model:(k + 1) * d_model] for k in range(6)]
            return [jnp.tile(p_, (rep, 1)) if rep > 1 else p_ for p_ in parts]

        sh1, sc1, g1, sh2, sc2, g2 = mod_rows(0, 1, 1)
        wm_p = (sgu_w[l] * tril_full).astype(BF16)
        sb_p = jnp.repeat(sgu_b[l].T, HEAD_DIM, axis=1)
        (oa, oc, cst, qt, kvt, kcr, vcr, ksl, kwn, vslt, vwnt, gatet) = _in_proj_call(
            xp, sc1, sh1, w_all, cw, cb, jnp.zeros((2, 256), F32), lng, lnb, wm_p, sb_p,
            shift=1, tm=512, transposed=True, name="in_proj_prompt")
        kc, vct = _compress_prompt_call(kcr.reshape(nc_p, CMP_STRIDE * LANES), vcr.reshape(nc_p, CMP_STRIDE * LANES),
                                        w1g, pet, w2p)
        ob = _attn_prompt_call(qt, gatet, kc, vct, covert_p, ksl, vslt, kwn, vwnt, ebias)
        xp, fst = _out_ffn_call(xp, oa, ob, oc, g1, sc2, sh2, g2, wo_p, ln_g[l], ln_b[l], wup, cfw, cfb,
                                jnp.zeros((2, d_ff), F32), wdn, shift=1, tm=256, alpha=alpha, name="out_ffn_prompt")
        paged = kvt[:kvw].reshape(4, N_KV_HEADS, HEAD_DIM, t // PAGE_SIZE, PAGE_SIZE)
        winr = kvt[kvw:, t - WINDOW:].reshape(2, N_KV_HEADS, HEAD_DIM, WINDOW)
        ps.append((jnp.transpose(paged, (3, 4, 0, 1, 2))[None], jnp.transpose(winr, (3, 0, 1, 2))[None],
                   cst[None], fst[None]))

        sh1, sc1, g1, sh2, sc2, g2 = mod_rows(1, 1 + nb, tq)
        eye_b = jnp.eye(nb, dtype=F32)
        wm_s = jax.vmap(lambda w: jnp.kron(w[:tq, :tq] * tril_tq, eye_b))(sgu_w[l]).astype(BF16)
        sb_s = jnp.repeat(jnp.repeat(sgu_b[l][:, :tq].T, nb, axis=0), HEAD_DIM, axis=1)
        cpast = jnp.transpose(state_conv[l], (1, 0, 2)).reshape(2 * nb, -1)
        (oa, oc, cst, q, kvf, ksl, vsl, gate, vrow) = _in_proj_call(
            xs, sc1, sh1, w_all, cw, cb, cpast, lng, lnb, wm_s, sb_s, shift=nb, tm=rs, transposed=False,
            name="in_proj_sample")
        kc, vc = _cmp_stream_call(pt_flat, cache_t, w1s, pet, w2p, layer=l, n_phys=n_phys, batch=nb,
                                  n_pages=n_pages, pg=min(32, n_pages))

        def by_batch(a):
            return jnp.transpose(a.reshape(tq, nb, -1), (1, 0, 2))

        qb5 = by_batch(q).reshape(nb, tq, N_KV_HEADS, GQA, LANES)
        q_hq = jnp.pad(jnp.transpose(qb5, (0, 2, 3, 1, 4)), ((0, 0), (0, 0), (0, 0), (0, 8 - tq), (0, 0)))
        q_hq = q_hq.reshape(nb, N_KV_HEADS, 32, LANES)
        q_qh = jnp.pad(jnp.transpose(qb5, (0, 2, 1, 3, 4)), ((0, 0), (0, 0), (0, 0), (0, 8 - GQA), (0, 0)))
        ocmp, idx, bias = _cmp_attn_sample_call(q_hq, kc, vc, cover_s, past=past, ns=ns_s,
                                                n_cache_blocks=n_cache_blocks)
        ocmp_qh = jnp.transpose(ocmp.reshape(nb, N_KV_HEADS, GQA, 8, LANES)[:, :, :, :tq], (0, 1, 3, 2, 4))
        ocmp_qh = jnp.pad(ocmp_qh, ((0, 0), (0, 0), (0, 0), (0, 8 - GQA), (0, 0)))
        g5 = by_batch(gate)[:, :, :N_HEADS * N_BRANCH].reshape(nb, tq, N_KV_HEADS, GQA, N_BRANCH)
        gate_qh = jnp.pad(jnp.transpose(g5, (0, 2, 1, 3, 4)),
                          ((0, 0), (0, 0), (0, 0), (0, 8 - GQA), (0, LANES - N_BRANCH)))
        pad8 = lambda a: jnp.pad(by_batch(a), ((0, 0), (0, 8 - tq), (0, 0)))
        wnew_rows = by_batch(kvf[:, kvw:])
        wnew_t = jnp.pad(jnp.transpose(wnew_rows, (0, 2, 1)), ((0, 0), (0, 0), (LANES - tq, 0)))
        o5, wout_t = _sel_attn_sample_call(
            idx[:, :, :tq, :N_SELECT].reshape(-1), pt_flat, cache_t, q_qh, bias, expand, pad8(ksl), pad8(vsl),
            win_t_all[l], jnp.pad(wnew_rows, ((0, 0), (0, 8 - tq), (0, 0))), wnew_t, ocmp_qh, gate_qh,
            layer=l, n_phys=n_phys, n_pages=n_pages, tq=tq, n_cache_blocks=n_cache_blocks)
        ob = jnp.transpose(o5[:, :, :, :GQA], (2, 0, 1, 3, 4)).reshape(rs, N_HEADS * LANES).astype(BF16)
        fpast = jnp.transpose(state_ffn_conv[l], (1, 0, 2)).reshape(2 * nb, -1)
        xs, fst = _out_ffn_call(xs, oa, ob, oc, g1, sc2, sh2, g2, wo_p, ln_g[l], ln_b[l], wup, cfw, cfb, fpast, wdn,
                                shift=nb, tm=rs, alpha=alpha, name="out_ffn_sample")
        ss.append((by_batch(kvf[:, :kvw]).reshape(nb, tq, 4, N_KV_HEADS, HEAD_DIM),
                   jnp.transpose(wout_t, (0, 2, 1)).reshape(nb, WINDOW, 2, N_KV_HEADS, HEAD_DIM),
                   jnp.transpose(cst.reshape(2, nb, -1), (1, 0, 2)),
                   jnp.transpose(fst.reshape(2, nb, -1), (1, 0, 2)),
                   by_batch(vrow)))

    ys = jnp.transpose(xs.reshape(tq, nb, d_model), (1, 0, 2))
    return (xp[None], ys,
            jnp.stack([s[0] for s in ps]), jnp.stack([s[1] for s in ps]),
            jnp.stack([s[2] for s in ps]), jnp.stack([s[3] for s in ps]),
            jnp.stack([s[0] for s in ss]), jnp.stack([s[1] for s in ss]),
            jnp.stack([s[2] for s in ss]), jnp.stack([s[3] for s in ss]),
            jnp.stack([s[4] for s in ss]))
```

```python
import functools
import math

import numpy as np
import jax
import jax.numpy as jnp
from jax import lax
from jax.experimental import pallas as pl
from jax.experimental.pallas import tpu as pltpu

F32 = jnp.float32
BF16 = jnp.bfloat16

HEAD_DIM = 64
N_HEADS = 8
N_KV_HEADS = 2
GQA = N_HEADS // N_KV_HEADS
N_BRANCH = 3
CONV_K = 3
CMP_LEN = 32
CMP_STRIDE = 16
CMP_HID = 128
SLC_BLOCK = 64
N_SELECT = 16
WINDOW = 512
Q_BLOCK = 128
PAGE_SIZE = 128
GMLP_CHUNK = 128
GMLP_GROUPS = 4
LN_EPS = 1e-5
NEG_INF = -1e30
FORCE = 1e4
REMOVED = -3e38
LOG2E = 1.4426950408889634

LANES = 128
KEY_TILE = 512
BLOCKS_PER_TILE = KEY_TILE // SLC_BLOCK
NS_ROWS = 16
VMEM_LIMIT = 56 * 1024 * 1024


def _dot(a, b):
    return jnp.dot(a, b, preferred_element_type=F32)


def _dot_nt(a, b):
    return lax.dot_general(a, b, (((1,), (1,)), ((), ())), preferred_element_type=F32)


def _split(a):
    hi = a.astype(BF16)
    lo = (a - hi.astype(F32)).astype(BF16)
    return hi, lo


def _dot3(a, b):
    ah, al = _split(a)
    bh, bl = _split(b)
    return _dot(ah, bh) + _dot(ah, bl) + _dot(al, bh)


def _sigmoid(x):
    return 1.0 / (1.0 + jnp.exp(-x))


def _gelu(x):
    c = math.sqrt(2.0 / math.pi)
    return 0.5 * x * (1.0 + jnp.tanh(c * (x + 0.044715 * (x * x * x))))


def _layer_norm(x, g, b):
    mu = jnp.mean(x, axis=-1, keepdims=True)
    xc = x - mu
    var = jnp.mean(xc * xc, axis=-1, keepdims=True)
    return xc * lax.rsqrt(var + LN_EPS) * g + b


def _masked_softmax(s, mask, axis):
    sm = jnp.where(mask, s, NEG_INF)
    mx = jnp.max(sm, axis=axis, keepdims=True)
    e = jnp.where(mask, jnp.exp2(s - mx), 0.0)
    l = jnp.sum(e, axis=axis, keepdims=True)
    return e * (1.0 / jnp.where(l > 0.0, l, 1.0))


def _softmax_rows(sm, col_valid):
    e = jnp.exp2(sm - jnp.max(sm, axis=0, keepdims=True))
    inv = 1.0 / jnp.sum(e, axis=0, keepdims=True)
    if col_valid is not None:
        inv = jnp.where(col_valid, inv, 0.0)
    return e * inv


def _top_select(val, blk_f, n_blk, axis):
    sel = jnp.zeros_like(val)
    firsts, tops = [], []
    for _ in range(N_SELECT):
        mx = jnp.max(val, axis=axis, keepdims=True)
        first = jnp.min(jnp.where(val == mx, blk_f, float(n_blk)), axis=axis, keepdims=True)
        hit = blk_f == first
        sel = jnp.where(hit, 1.0, sel)
        val = jnp.where(hit, REMOVED, val)
        firsts.append(first)
        tops.append(mx)
    return sel, firsts, tops


def _shifted_conv(src_ref, x, w_ref, b_ref, pad, shift, rows):
    x2 = src_ref[pl.ds(pad - 2 * shift, rows), :]
    x1 = src_ref[pl.ds(pad - shift, rows), :]
    return w_ref[0:1, :] * x2 + w_ref[1:2, :] * x1 + w_ref[2:3, :] * x + b_ref[...]


def _ada_kernel(c_ref, w_ref, b_ref, o_ref):
    c = c_ref[...]
    o_ref[0] = _dot3(c * _sigmoid(c), w_ref[0]) + b_ref[0]


def _ada_call(c_all, w_ada, b_ada):
    depth, d_model, n_mod = w_ada.shape
    rc = c_all.shape[0]
    tn = 1024
    return pl.pallas_call(
        _ada_kernel,
        grid=(depth, n_mod // tn),
        in_specs=[pl.BlockSpec((rc, d_model), lambda l, n: (0, 0)),
                  pl.BlockSpec((1, d_model, tn), lambda l, n: (l, 0, n)),
                  pl.BlockSpec((1, 1, tn), lambda l, n: (l, 0, n))],
        out_specs=pl.BlockSpec((1, rc, tn), lambda l, n: (l, 0, n)),
        out_shape=jax.ShapeDtypeStruct((depth, rc, n_mod), F32),
        compiler_params=pltpu.CompilerParams(dimension_semantics=("arbitrary", "arbitrary"),
                                             vmem_limit_bytes=VMEM_LIMIT),
        name="ada_mod",
    )(c_all, w_ada, b_ada.reshape(depth, 1, n_mod))


_C_AB, _C_AC, _C_AH = 0, 256, 512
_C_Q = 768
_C_KV = 1792
_C_GATE = 2560
_C_GU = 2688
_C_GV = 2944
_N_COL = 3200
_N_SLOT = 6


def _in_proj_kernel(x_ref, sc_ref, sh_ref, w_ref, cw_ref, cb_ref, cpast_ref, lng_ref, lnb_ref, wm_ref, sb_ref,
                    *rest, shift, tm, pad, transposed):
    if transposed:
        (oa_ref, oc_ref, cstate_ref, qt_ref, kvt_ref, kcr_ref, vcr_ref, ksl_ref, kwn_ref, vslt_ref, vwnt_ref,
         gatet_ref, zs_ref) = rest
    else:
        (oa_ref, oc_ref, cstate_ref, q_ref, kvf_ref, ksl_ref, vsl_ref, gate_ref, vrow_ref, zs_ref) = rest
    i = pl.program_id(0)

    @pl.when(i == 0)
    def _():
        zs_ref[pl.ds(pad - 2 * shift, 2 * shift), :] = cpast_ref[...]

    h = (x_ref[...] * (1.0 + sc_ref[...]) + sh_ref[...]).astype(BF16)
    p = _dot(h, w_ref[...])

    z = p[:, _C_AC:_C_AC + 256] * p[:, _C_AH:_C_AH + 256]
    zs_ref[pl.ds(pad, tm), :] = z
    y = _shifted_conv(zs_ref, z, cw_ref, cb_ref, pad, shift, tm)
    oa_ref[...] = (p[:, _C_AB:_C_AB + 256] * y).astype(BF16)
    tail = zs_ref[pl.ds(pad + tm - 2 * shift, 2 * shift), :]
    cstate_ref[...] = tail
    zs_ref[pl.ds(pad - 2 * shift, 2 * shift), :] = tail

    kv = [p[:, _C_KV + k * LANES:_C_KV + (k + 1) * LANES] for k in range(_N_SLOT)]
    gate = _sigmoid(p[:, _C_GATE:_C_GATE + LANES])
    if transposed:
        for hh in range(N_HEADS):
            qt_ref[pl.ds(hh * LANES, LANES), :] = p[:, _C_Q + hh * LANES:_C_Q + (hh + 1) * LANES].T.astype(BF16)
        kvt = [a.T for a in kv]
        for k in range(_N_SLOT):
            kvt_ref[pl.ds(k * LANES, LANES), :] = kvt[k]
        kcr_ref[...] = kv[0].astype(BF16)
        vcr_ref[...] = kv[1].astype(BF16)
        ksl_ref[...] = kv[2].astype(BF16)
        kwn_ref[...] = kv[4].astype(BF16)
        vslt_ref[...] = kvt[3].astype(BF16)
        vwnt_ref[...] = kvt[5].astype(BF16)
        gatet_ref[...] = gate.T
    else:
        q_ref[...] = p[:, _C_Q:_C_Q + 1024].astype(BF16)
        kvf_ref[...] = p[:, _C_KV:_C_KV + _N_SLOT * LANES]
        ksl_ref[...] = kv[2].astype(BF16)
        vsl_ref[...] = kv[3].astype(BF16)
        gate_ref[...] = gate

    u = _gelu(p[:, _C_GU:_C_GU + 256])
    v = _layer_norm(_gelu(p[:, _C_GV:_C_GV + 256]), lng_ref[...], lnb_ref[...])
    if not transposed:
        vrow_ref[...] = v
    lane = lax.broadcasted_iota(jnp.int32, (1, 256), 1)
    for c in range(tm // GMLP_CHUNK):
        vc = v[c * GMLP_CHUNK:(c + 1) * GMLP_CHUNK]
        mixed = sb_ref[...]
        for g in range(GMLP_GROUPS):
            vg = jnp.where((lane >= g * HEAD_DIM) & (lane < (g + 1) * HEAD_DIM), vc, 0.0).astype(BF16)
            mixed = mixed + _dot(wm_ref[g], vg)
        oc_ref[pl.ds(c * GMLP_CHUNK, GMLP_CHUNK), :] = (u[c * GMLP_CHUNK:(c + 1) * GMLP_CHUNK] * mixed).astype(BF16)


def _in_proj_call(x, sc, sh, w_all, cw, cb, cpast, lng, lnb, wm, sb, *, shift, tm, transposed, name):
    rows, d_model = x.shape
    pad = max(8, 2 * shift)
    mr = sc.shape[0]
    mod_spec = (pl.BlockSpec((1, d_model), lambda i: (0, 0)) if mr == 1
                else pl.BlockSpec((tm, d_model), lambda i: (i, 0)))

    def row_spec(n):
        return pl.BlockSpec((tm, n), lambda i: (i, 0))

    def col_spec(n):
        return pl.BlockSpec((n, tm), lambda i: (0, i))

    def full(a):
        nd = a.ndim
        return pl.BlockSpec(a.shape, lambda i: (0,) * nd)

    sds = jax.ShapeDtypeStruct
    out_shape = [sds((rows, 256), BF16), sds((rows, 256), BF16), sds((2 * shift, 256), F32)]
    out_specs = [row_spec(256), row_spec(256), pl.BlockSpec((2 * shift, 256), lambda i: (0, 0))]
    if transposed:
        out_shape += [sds((N_HEADS * LANES, rows), BF16), sds((_N_SLOT * LANES, rows), F32)]
        out_specs += [col_spec(N_HEADS * LANES), col_spec(_N_SLOT * LANES)]
        out_shape += [sds((rows, LANES), BF16)] * 4 + [sds((LANES, rows), BF16)] * 2 + [sds((LANES, rows), F32)]
        out_specs += [row_spec(LANES)] * 4 + [col_spec(LANES)] * 3
    else:
        out_shape += [sds((rows, 1024), BF16), sds((rows, _N_SLOT * LANES), F32), sds((rows, LANES), BF16),
                      sds((rows, LANES), BF16), sds((rows, LANES), F32), sds((rows, 256), F32)]
        out_specs += [row_spec(1024), row_spec(_N_SLOT * LANES), row_spec(LANES), row_spec(LANES), row_spec(LANES),
                      row_spec(256)]
    return pl.pallas_call(
        functools.partial(_in_proj_kernel, shift=shift, tm=tm, pad=pad, transposed=transposed),
        grid=(rows // tm,),
        in_specs=[row_spec(d_model), mod_spec, mod_spec, full(w_all), full(cw), full(cb), full(cpast),
                  full(lng), full(lnb), full(wm), full(sb)],
        out_specs=out_specs,
        out_shape=out_shape,
        scratch_shapes=[pltpu.VMEM((pad + tm, 256), F32)],
        compiler_params=pltpu.CompilerParams(dimension_semantics=("arbitrary",), vmem_limit_bytes=VMEM_LIMIT),
        name=name,
    )(x, sc, sh, w_all, cw, cb, cpast, lng, lnb, wm, sb)


def _pe_term_kernel(pe_ref, w1_ref, o_ref):
    for s in range(2):
        o_ref[s] = _dot3(pe_ref[s], w1_ref[s])


def _pe_term_call(pe_flat, w1_flat):
    return pl.pallas_call(
        _pe_term_kernel,
        out_shape=jax.ShapeDtypeStruct((2, 8, CMP_HID), F32),
        compiler_params=pltpu.CompilerParams(vmem_limit_bytes=VMEM_LIMIT),
        name="cmp_pe_term",
    )(pe_flat, w1_flat)


def _compress_prompt_kernel(kx_ref, vx_ref, w1_ref, pet_ref, w2_ref, kc_ref, vct_ref, sh_ref, *, nc):
    sh_ref[pl.ds(0, 8), :] = jnp.zeros((8, CMP_HID), F32)
    for s, x_ref in enumerate((kx_ref, vx_ref)):
        parts = _dot(x_ref[...], w1_ref[s])
        acc = jnp.zeros((nc, LANES), F32)
        for g in range(N_KV_HEADS):
            p0 = parts[:, g * 256:g * 256 + CMP_HID]
            p1 = parts[:, g * 256 + CMP_HID:(g + 1) * 256]
            sh_ref[pl.ds(8, nc), :] = p0
            p0s = sh_ref[pl.ds(7, nc), :]
            hid = _gelu(p0s + p1 + pet_ref[s][0:1, :])
            acc = acc + _dot(hid.astype(BF16), w2_ref[s, g])
        if s == 0:
            kc_ref[...] = acc.astype(BF16)
        else:
            vct_ref[...] = acc.T.astype(BF16)


def _compress_prompt_call(kx, vx, w1p, pet, w2p):
    nc = kx.shape[0]
    return pl.pallas_call(
        functools.partial(_compress_prompt_kernel, nc=nc),
        out_shape=[jax.ShapeDtypeStruct((nc, LANES), BF16), jax.ShapeDtypeStruct((LANES, nc), BF16)],
        scratch_shapes=[pltpu.VMEM((nc + 8, CMP_HID), F32)],
        compiler_params=pltpu.CompilerParams(vmem_limit_bytes=VMEM_LIMIT),
        name="compress_prompt",
    )(kx, vx, w1p, pet, w2p)


def _attn_prompt_kernel(qt_ref, gatet_ref, kc_ref, vct_ref, covert_ref, ksl_ref, vslt_ref, kwn_ref, vwnt_ref, eb_ref,
                        o_ref, ns_ref, rhs_ref, sa_ref, sb_ref, ma_ref, mb_ref, *, nc, ns):
    qb = pl.program_id(0)
    s0 = qb * Q_BLOCK
    ncol = GQA * Q_BLOCK
    pair = 2 * Q_BLOCK
    col = lax.broadcasted_iota(jnp.int32, (1, ncol), 1)
    qpos_c = s0 + (col & (Q_BLOCK - 1))
    qp = s0 + lax.broadcasted_iota(jnp.int32, (1, Q_BLOCK), 1)
    m_idx = lax.broadcasted_iota(jnp.int32, (nc, 1), 0)
    blk = lax.broadcasted_iota(jnp.int32, (ns, 1), 0)
    blk_f = blk.astype(F32)
    key_row = lax.broadcasted_iota(jnp.int32, (KEY_TILE, 1), 0)
    n_tiles = s0 // KEY_TILE + 1

    ns_ref[pl.ds(ns, 8), :] = jnp.zeros((8, Q_BLOCK), F32)
    rhs_ref[pl.ds(LANES + NS_ROWS, 2 * LANES - LANES - NS_ROWS), :] = jnp.zeros((LANES - NS_ROWS, ncol), BF16)

    for g in (pl.program_id(1),):
        rq = jnp.concatenate([qt_ref[pl.ds(h * LANES, LANES), :] for h in range(GQA)], axis=1)
        rhs_ref[pl.ds(0, LANES), :] = rq

        vrows = pl.ds(pl.multiple_of(g * HEAD_DIM, HEAD_DIM), HEAD_DIM)

        last_pos = jnp.where(m_idx >= 1, CMP_STRIDE * (m_idx - 1) + CMP_LEN - 1, 2 ** 30)
        sc = jnp.where(last_pos <= qpos_c, _dot(kc_ref[...], rq), NEG_INF)
        p = _softmax_rows(sc, qpos_c >= CMP_LEN - 1)
        o_cmp = _dot(vct_ref[vrows, :], p.astype(BF16))

        p4 = (p[:, 0:Q_BLOCK] + p[:, Q_BLOCK:2 * Q_BLOCK] + p[:, 2 * Q_BLOCK:3 * Q_BLOCK]
              + p[:, 3 * Q_BLOCK:4 * Q_BLOCK])
        hi, lo = _split(p4)
        imp = _dot(covert_ref[...], hi) + _dot(covert_ref[...], lo)
        qblk = qp // SLC_BLOCK
        elig = blk * SLC_BLOCK <= qp
        forced = (blk == 0) | (blk == qblk) | (blk == qblk - 1)
        val = jnp.where(elig, jnp.where(forced, REMOVED, imp), -FORCE)
        for _ in range(N_SELECT - 3):
            mx = jnp.max(val, axis=0, keepdims=True)
            first = jnp.min(jnp.where(val == mx, blk_f, float(ns)), axis=0, keepdims=True)
            val = jnp.where(blk_f == first, REMOVED, val)
        ns_ref[pl.ds(0, ns), :] = jnp.where(elig & (val == REMOVED), 0.0, 1.0)

        def tile_scores(kt):
            k0 = pl.multiple_of(kt * KEY_TILE, KEY_TILE)
            nsf = ns_ref[pl.ds(pl.multiple_of(kt * BLOCKS_PER_TILE, BLOCKS_PER_TILE), NS_ROWS), :].astype(BF16)
            rhs_ref[pl.ds(LANES, NS_ROWS), :] = jnp.concatenate([nsf] * GQA, axis=1)
            lhs = jnp.concatenate([ksl_ref[pl.ds(k0, KEY_TILE), :], eb_ref[...]], axis=1)
            return _dot(lhs, rhs_ref[...])

        ones_rows = jnp.ones((NS_ROWS, KEY_TILE), BF16)

        def flash_update(s, s_max, k0, carry):
            m_run, acc = carry
            m_new = jnp.maximum(m_run, s_max)
            pe = jnp.exp2(s - m_new).astype(BF16)
            v_aug = jnp.concatenate([vslt_ref[vrows, pl.ds(k0, KEY_TILE)], ones_rows], axis=0)
            return m_new, jnp.exp2(m_run - m_new) * acc + _dot(v_aug, pe)

        def produce(dst_ref, dmx_ref, kt):
            s = tile_scores(kt)
            dst_ref[...] = s
            dmx_ref[...] = jnp.max(s, axis=0, keepdims=True)

        def update_from(src_ref, smx_ref, kt, carry):
            k0 = pl.multiple_of(kt * KEY_TILE, KEY_TILE)
            return tuple(flash_update(src_ref[:, pl.ds(hp * pair, pair)], smx_ref[:, pl.ds(hp * pair, pair)], k0,
                                      carry[hp]) for hp in range(GQA // 2))

        produce(sa_ref, ma_ref, 0)

        def pair_step(j, carry):
            kt = 2 * j
            produce(sb_ref, mb_ref, kt + 1)
            carry = update_from(sa_ref, ma_ref, kt, carry)
            produce(sa_ref, ma_ref, kt + 2)
            return update_from(sb_ref, mb_ref, kt + 1, carry)

        init = (jnp.full((1, pair), NEG_INF, F32), jnp.zeros((HEAD_DIM + NS_ROWS, pair), F32))
        last = n_tiles - 1
        carry = lax.fori_loop(0, last // 2, pair_step, (init,) * (GQA // 2))

        def odd_step(carry):
            carry = update_from(sa_ref, ma_ref, last - 1, carry)
            produce(sa_ref, ma_ref, last)
            return carry

        carry = lax.cond(lax.rem(last, 2) == 1, odd_step, lambda c: c, carry)
        k0 = pl.multiple_of(last * KEY_TILE, KEY_TILE)
        o_slc_parts = []
        for hp in range(GQA // 2):
            s = sa_ref[:, pl.ds(hp * pair, pair)]
            s = jnp.where(k0 + key_row <= qpos_c[:, hp * pair:(hp + 1) * pair], s, NEG_INF)
            _, acc = flash_update(s, jnp.max(s, axis=0, keepdims=True), k0, carry[hp])
            o_slc_parts.append(acc[0:HEAD_DIM] * (1.0 / acc[HEAD_DIM:HEAD_DIM + 1]))
        o_slc = jnp.concatenate(o_slc_parts, axis=1)

        w0 = pl.multiple_of(jnp.maximum(s0 - WINDOW, 0), Q_BLOCK)
        dist = qp - (w0 + lax.broadcasted_iota(jnp.int32, (WINDOW + Q_BLOCK, 1), 0))
        wbias = jnp.where(lax.bitcast_convert_type(dist, jnp.uint32) <= WINDOW, 0.0, NEG_INF)
        sw = _dot(kwn_ref[pl.ds(w0, WINDOW + Q_BLOCK), :], rq) + jnp.concatenate([wbias] * GQA, axis=1)
        pw = _softmax_rows(sw, None)
        o_win = _dot(vwnt_ref[vrows, pl.ds(w0, WINDOW + Q_BLOCK)], pw.astype(BF16))

        outs = []
        for h in range(GQA):
            c = slice(h * Q_BLOCK, (h + 1) * Q_BLOCK)
            gr = (GQA * g + h) * N_BRANCH
            outs.append(gatet_ref[pl.ds(gr, 1), :] * o_cmp[:, c] + gatet_ref[pl.ds(gr + 1, 1), :] * o_slc[:, c]
                        + gatet_ref[pl.ds(gr + 2, 1), :] * o_win[:, c])
        for hp in range(GQA // 2):
            o2 = jnp.concatenate(outs[2 * hp:2 * hp + 2], axis=0)
            o_ref[:, hp * LANES:(hp + 1) * LANES] = o2.T.astype(BF16)


def _attn_prompt_call(qt, gatet, kc, vct, covert, ksl, vslt, kwn, vwnt, ebias):
    t = qt.shape[1]
    ns, nc = covert.shape
    vmem = pl.BlockSpec(memory_space=pltpu.VMEM)
    return pl.pallas_call(
        functools.partial(_attn_prompt_kernel, nc=nc, ns=ns),
        grid=(t // Q_BLOCK, N_KV_HEADS),
        in_specs=[pl.BlockSpec((GQA * LANES, Q_BLOCK), lambda i, g: (g, i)),
                  pl.BlockSpec((LANES, Q_BLOCK), lambda i, g: (0, i)),
                  vmem, vmem, vmem, vmem, vmem, vmem, vmem, vmem],
        out_specs=pl.BlockSpec((Q_BLOCK, GQA * HEAD_DIM), lambda i, g: (i, g)),
        out_shape=jax.ShapeDtypeStruct((t, N_HEADS * HEAD_DIM), BF16),
        scratch_shapes=[pltpu.VMEM((ns + 8, Q_BLOCK), F32), pltpu.VMEM((2 * LANES, GQA * Q_BLOCK), BF16),
                        pltpu.VMEM((KEY_TILE, GQA * Q_BLOCK), F32), pltpu.VMEM((KEY_TILE, GQA * Q_BLOCK), F32),
                        pltpu.VMEM((1, GQA * Q_BLOCK), F32), pltpu.VMEM((1, GQA * Q_BLOCK), F32)],
        compiler_params=pltpu.CompilerParams(dimension_semantics=("arbitrary", "arbitrary"),
                                             vmem_limit_bytes=VMEM_LIMIT),
        name="attn_prompt",
    )(qt, gatet, kc, vct, covert, ksl, vslt, kwn, vwnt, ebias)


def _out_ffn_kernel(x_ref, oa_ref, ob_ref, oc_ref, g1_ref, sc2_ref, sh2_ref, g2_ref, wo_ref, lng_ref, lnb_ref,
                    wup_ref, cfw_ref, cfb_ref, fpast_ref, wdn_ref, xo_ref, fstate_ref, us_ref,
                    *, shift, tm, pad, alpha, d_ff):
    i = pl.program_id(0)

    @pl.when(i == 0)
    def _():
        us_ref[pl.ds(pad - 2 * shift, 2 * shift), :] = fpast_ref[...]

    mix = _dot(jnp.concatenate([oa_ref[...], ob_ref[...], oc_ref[...]], axis=1), wo_ref[...])
    x1 = _layer_norm(alpha * x_ref[...] + (1.0 + g1_ref[...]) * mix, lng_ref[0:1, :], lnb_ref[0:1, :])
    h2 = (x1 * (1.0 + sc2_ref[...]) + sh2_ref[...]).astype(BF16)
    up = _dot(h2, wup_ref[...])
    ua = up[:, :d_ff]
    us_ref[pl.ds(pad, tm), :] = ua
    yc = _shifted_conv(us_ref, ua, cfw_ref, cfb_ref, pad, shift, tm)
    tail = us_ref[pl.ds(pad + tm - 2 * shift, 2 * shift), :]
    fstate_ref[...] = tail
    us_ref[pl.ds(pad - 2 * shift, 2 * shift), :] = tail
    act = (yc * _sigmoid(yc) * up[:, d_ff:]).astype(BF16)
    y = _dot(act, wdn_ref[...])
    xo_ref[...] = _layer_norm(alpha * x1 + (1.0 + g2_ref[...]) * y, lng_ref[1:2, :], lnb_ref[1:2, :])


def _out_ffn_call(x, oa, ob, oc, g1, sc2, sh2, g2, wo, lng, lnb, wup, cfw, cfb, fpast, wdn, *, shift, tm, alpha, name):
    rows, d_model = x.shape
    d_ff = wdn.shape[0]
    pad = max(8, 2 * shift)
    mr = g1.shape[0]
    mod_spec = (pl.BlockSpec((1, d_model), lambda i: (0, 0)) if mr == 1
                else pl.BlockSpec((tm, d_model), lambda i: (i, 0)))

    def row_spec(n):
        return pl.BlockSpec((tm, n), lambda i: (i, 0))

    vmem = pl.BlockSpec(memory_space=pltpu.VMEM)
    return pl.pallas_call(
        functools.partial(_out_ffn_kernel, shift=shift, tm=tm, pad=pad, alpha=alpha, d_ff=d_ff),
        grid=(rows // tm,),
        in_specs=[row_spec(d_model), row_spec(256), row_spec(N_HEADS * HEAD_DIM), row_spec(256), mod_spec, mod_spec, mod_spec,
                  mod_spec, vmem, vmem, vmem, vmem, vmem, vmem, vmem, vmem],
        out_specs=[row_spec(d_model), pl.BlockSpec((2 * shift, d_ff), lambda i: (0, 0))],
        out_shape=[jax.ShapeDtypeStruct((rows, d_model), F32), jax.ShapeDtypeStruct((2 * shift, d_ff), F32)],
        scratch_shapes=[pltpu.VMEM((pad + tm, d_ff), F32)],
        compiler_params=pltpu.CompilerParams(dimension_semantics=("arbitrary",), vmem_limit_bytes=VMEM_LIMIT),
        name=name,
    )(x, oa, ob, oc, g1, sc2, sh2, g2, wo, lng, lnb, wup, cfw, cfb, fpast, wdn)


def _cmp_stream_kernel(pt_ref, cache_ref, w1_ref, pet_ref, w2_ref, kc_ref, vc_ref, buf, rbuf, sem, carry,
                       *, page_base, n_pages, pg, n_groups, total):
    b = pl.program_id(0)
    gi = pl.program_id(1)
    step = b * n_groups + gi
    slot = lax.rem(step, 2)
    m = pg * (PAGE_SIZE // CMP_STRIDE)

    def page_copy(page, slt, i):
        return pltpu.make_async_copy(cache_ref.at[page, pl.ds(0, 2 * LANES), :], buf.at[slt, i], sem.at[slt])

    def issue(stp, slt):
        base = lax.div(stp, n_groups) * n_pages + lax.rem(stp, n_groups) * pg
        for i in range(pg):
            page_copy(page_base + pt_ref[base + i], slt, i).start()

    @pl.when(step == 0)
    def _():
        carry[...] = jnp.zeros(carry.shape, F32)
        issue(step, slot)

    @pl.when(step + 1 < total)
    def _():
        issue(step + 1, 1 - slot)

    for i in range(pg):
        page_copy(0, slot, i).wait()

    for i in range(pg):
        for s in range(2):
            rbuf[s, pl.ds(i * PAGE_SIZE, PAGE_SIZE), :] = buf[slot, i, pl.ds(s * LANES, LANES), :].T

    lane = lax.broadcasted_iota(jnp.int32, (1, LANES), 1)
    first_half = lane < HEAD_DIM
    row0 = lax.broadcasted_iota(jnp.int32, (m, 1), 0) == 0
    pieces = [[[], []], [[], []]]
    for pr in range(CMP_STRIDE // 2):
        for s in range(2):
            a = rbuf[s, pl.ds(2 * pr, m, stride=CMP_STRIDE), :]
            bb = rbuf[s, pl.ds(2 * pr + 1, m, stride=CMP_STRIDE), :]
            pieces[s][0].append(jnp.where(first_half, a, pltpu.roll(bb, HEAD_DIM, 1)).astype(BF16))
            pieces[s][1].append(jnp.where(first_half, pltpu.roll(a, HEAD_DIM, 1), bb).astype(BF16))
    for s, o_ref in enumerate((kc_ref, vc_ref)):
        acc = jnp.zeros((m, LANES), F32)
        for g in range(N_KV_HEADS):
            parts = _dot(jnp.concatenate(pieces[s][g], axis=1), w1_ref[s])
            p0 = parts[:, :CMP_HID]
            p1 = parts[:, CMP_HID:]
            prev = jnp.where(gi == 0, 0.0, carry[s * 2 + g][0:1, :])
            p0s = jnp.where(row0, prev, pltpu.roll(p0, 1, 0))
            carry[s * 2 + g] = jnp.broadcast_to(p0[m - 1:m, :], (8, CMP_HID))
            hid = _gelu(p0s + p1 + pet_ref[s][0:1, :])
            acc = acc + _dot(hid.astype(BF16), w2_ref[s, g])
        o_ref[0] = acc.astype(BF16)


def _cmp_stream_call(pt_flat, cache_t, w1s, pet, w2p, *, layer, n_phys, batch, n_pages, pg):
    n_groups = n_pages // pg
    m = pg * (PAGE_SIZE // CMP_STRIDE)
    nc = n_pages * (PAGE_SIZE // CMP_STRIDE)
    total = batch * n_groups

    def full(a):
        nd = a.ndim
        return pl.BlockSpec(a.shape, lambda b, g, pt: (0,) * nd)

    grid_spec = pltpu.PrefetchScalarGridSpec(
        num_scalar_prefetch=1,
        grid=(batch, n_groups),
        in_specs=[pl.BlockSpec(memory_space=pl.ANY), full(w1s), full(pet), full(w2p)],
        out_specs=[pl.BlockSpec((1, m, LANES), lambda b, g, pt: (b, g, 0))] * 2,
        scratch_shapes=[pltpu.VMEM((2, pg, 2 * LANES, PAGE_SIZE), F32), pltpu.VMEM((2, pg * PAGE_SIZE, LANES), F32),
                        pltpu.SemaphoreType.DMA((2,)), pltpu.VMEM((4, 8, CMP_HID), F32)],
    )
    return pl.pallas_call(
        functools.partial(_cmp_stream_kernel, page_base=layer * n_phys, n_pages=n_pages, pg=pg,
                          n_groups=n_groups, total=total),
        grid_spec=grid_spec,
        out_shape=[jax.ShapeDtypeStruct((batch, nc, LANES), BF16)] * 2,
        compiler_params=pltpu.CompilerParams(dimension_semantics=("arbitrary", "arbitrary"),
                                             vmem_limit_bytes=VMEM_LIMIT),
        name="cmp_stream_sample",
    )(pt_flat, cache_t, w1s, pet, w2p)


def _cmp_attn_sample_kernel(q_ref, kc_ref, vc_ref, cover_ref, ocmp_ref, idx_ref, bias_ref,
                            *, past, nc, ns, nsp, n_cache_blocks):
    rows = lax.broadcasted_iota(jnp.int32, (32, 1), 0)
    qpos_r = past + (rows & 7)
    qp = past + lax.broadcasted_iota(jnp.int32, (8, 1), 0)
    m_idx = lax.broadcasted_iota(jnp.int32, (1, nc), 1)
    blk = lax.broadcasted_iota(jnp.int32, (1, nsp), 1)
    blk_f = blk.astype(F32)
    lane = lax.broadcasted_iota(jnp.int32, (1, LANES), 1)
    cmask = (m_idx >= 1) & (CMP_STRIDE * (m_idx - 1) + CMP_LEN - 1 <= qpos_r)
    for g in range(N_KV_HEADS):
        p = _masked_softmax(_dot_nt(q_ref[0, g], kc_ref[0]), cmask, -1)
        ocmp_ref[0, g] = _dot(p.astype(BF16), vc_ref[0])
        p4 = p[0:8] + p[8:16] + p[16:24] + p[24:32]
        hi, lo = _split(p4)
        imp = _dot(hi, cover_ref[...]) + _dot(lo, cover_ref[...])
        qblk = qp // SLC_BLOCK
        elig = blk * SLC_BLOCK <= qp
        forced = (blk == 0) | (blk == qblk) | (blk == qblk - 1)
        val = jnp.where(blk < ns, jnp.where(elig, jnp.where(forced, FORCE, imp), -FORCE), REMOVED)
        _, firsts, tops = _top_select(val, blk_f, nsp, -1)
        idx = jnp.zeros((8, LANES), F32)
        bias = jnp.zeros((8, LANES), F32)
        for t in range(N_SELECT):
            ok = (tops[t] > -0.5 * FORCE) & (firsts[t] < float(n_cache_blocks))
            odd = firsts[t] - 2.0 * jnp.floor(firsts[t] * 0.5)
            idx = jnp.where(lane == t, firsts[t], idx)
            for hf in range(2):
                bias = jnp.where(lane == 2 * t + hf, jnp.where(ok & (odd == float(hf)), 0.0, NEG_INF), bias)
        idx_ref[0, g] = idx.astype(jnp.int32)
        bias_ref[0, g] = bias


def _cmp_attn_sample_call(q_hq, kc, vc, cover, *, past, ns, n_cache_blocks):
    batch, nc, _ = kc.shape
    nsp = cover.shape[1]
    blk4 = lambda r: pl.BlockSpec((1, N_KV_HEADS, r, LANES), lambda b: (b, 0, 0, 0))
    return pl.pallas_call(
        functools.partial(_cmp_attn_sample_kernel, past=past, nc=nc, ns=ns, nsp=nsp, n_cache_blocks=n_cache_blocks),
        grid=(batch,),
        in_specs=[blk4(32), pl.BlockSpec((1, nc, LANES), lambda b: (b, 0, 0)),
                  pl.BlockSpec((1, nc, LANES), lambda b: (b, 0, 0)), pl.BlockSpec(cover.shape, lambda b: (0, 0))],
        out_specs=[blk4(32), blk4(8), blk4(8)],
        out_shape=[jax.ShapeDtypeStruct((batch, N_KV_HEADS, 32, LANES), F32),
                   jax.ShapeDtypeStruct((batch, N_KV_HEADS, 8, LANES), jnp.int32),
                   jax.ShapeDtypeStruct((batch, N_KV_HEADS, 8, LANES), F32)],
        compiler_params=pltpu.CompilerParams(dimension_semantics=("arbitrary",), vmem_limit_bytes=VMEM_LIMIT),
        name="cmp_attn_sample",
    )(q_hq, kc, vc, cover)


def _sel_attn_sample_kernel(idx_ref, pt_ref, cache_ref, q_ref, bias_ref, ex_ref, knew_ref, vnew_ref, wint_ref,
                            wnew_ref, wnewt_ref, ocmp_ref, gate_ref, o_ref, wout_ref, buf, sem,
                            *, page_base, n_pages, tq, n_cache_blocks, total):
    b = pl.program_id(0)
    g = pl.program_id(1)
    step = b * N_KV_HEADS + g
    slot = lax.rem(step, 2)
    n_ent = tq * N_SELECT
    blocks_per_page = PAGE_SIZE // SLC_BLOCK

    def page_copy(page, slt, e):
        return pltpu.make_async_copy(cache_ref.at[page, pl.ds(2 * LANES, 2 * LANES), :], buf.at[slt, e], sem.at[slt])

    def issue(stp, slt):
        bb = lax.div(stp, N_KV_HEADS)

        def body(e, _):
            blk = jnp.minimum(idx_ref[stp * n_ent + e], n_cache_blocks - 1)
            page = pt_ref[bb * n_pages + lax.div(blk, blocks_per_page)]
            page_copy(page_base + page, slt, e).start()
            return 0
        lax.fori_loop(0, n_ent, body, 0)

    @pl.when(step == 0)
    def _():
        issue(step, slot)

    @pl.when(step + 1 < total)
    def _():
        issue(step + 1, 1 - slot)

    def wait_body(e, _):
        page_copy(0, slot, e).wait()
        return 0
    lax.fori_loop(0, n_ent, wait_body, 0)

    lane = lax.broadcasted_iota(jnp.int32, (1, LANES), 1)
    col8 = lax.broadcasted_iota(jnp.int32, (1, 8), 1)
    wcol = lax.broadcasted_iota(jnp.int32, (1, WINDOW), 1)
    k_wt = wint_ref[0, pl.ds(0, LANES), :].astype(BF16)
    v_wt = wint_ref[0, pl.ds(LANES, LANES), :].astype(BF16)
    wnew = wnew_ref[0]
    k_wn = wnew[:, 0:LANES].astype(BF16)
    v_wn = wnew[:, LANES:2 * LANES].astype(BF16)
    bias_all = _dot(bias_ref[0, 0].astype(BF16), ex_ref[...])
    keep = (lane >= g * HEAD_DIM) & (lane < (g + 1) * HEAD_DIM)
    for qi in range(tq):
        q4 = q_ref[0, 0, qi]
        new_ok = (col8 <= qi) & (col8 < tq)
        k_t = jnp.concatenate([buf[slot, qi * N_SELECT + k, pl.ds(0, LANES), :] for k in range(N_SELECT)], axis=1)
        v_t = jnp.concatenate([buf[slot, qi * N_SELECT + k, pl.ds(LANES, LANES), :] for k in range(N_SELECT)], axis=1)
        s = _dot(q4, k_t.astype(BF16)) + bias_all[qi:qi + 1, :]
        s_n = jnp.where(new_ok, _dot_nt(q4, knew_ref[0]), NEG_INF)
        mx = jnp.maximum(jnp.max(s, axis=-1, keepdims=True), jnp.max(s_n, axis=-1, keepdims=True))
        pe = jnp.exp2(s - mx)
        pn = jnp.exp2(s_n - mx)
        l = jnp.sum(pe, axis=-1, keepdims=True) + jnp.sum(pn, axis=-1, keepdims=True)
        o_slc = (_dot_nt(pe.astype(BF16), v_t.astype(BF16)) + _dot(pn.astype(BF16), vnew_ref[0])) * (1.0 / l)
        sw = jnp.where(wcol >= qi, _dot(q4, k_wt), NEG_INF)
        sw_n = jnp.where(new_ok, _dot_nt(q4, k_wn), NEG_INF)
        mw = jnp.maximum(jnp.max(sw, axis=-1, keepdims=True), jnp.max(sw_n, axis=-1, keepdims=True))
        pw = jnp.exp2(sw - mw)
        pwn = jnp.exp2(sw_n - mw)
        lw = jnp.sum(pw, axis=-1, keepdims=True) + jnp.sum(pwn, axis=-1, keepdims=True)
        o_win = (_dot_nt(pw.astype(BF16), v_wt) + _dot(pwn.astype(BF16), v_wn)) * (1.0 / lw)
        gt = gate_ref[0, 0, qi]
        o = gt[:, 0:1] * ocmp_ref[0, 0, qi] + gt[:, 1:2] * o_slc + gt[:, 2:3] * o_win
        o_ref[0, 0, qi] = jnp.where(keep, o, 0.0)

    @pl.when(g == 0)
    def _():
        shifted = pltpu.roll(wint_ref[0], WINDOW - tq, 1)
        wout_ref[0, :, pl.ds(0, WINDOW - LANES)] = shifted[:, 0:WINDOW - LANES]
        wout_ref[0, :, pl.ds(WINDOW - LANES, LANES)] = jnp.where(lane >= LANES - tq, wnewt_ref[0],
                                                                  shifted[:, WINDOW - LANES:WINDOW])


def _sel_attn_sample_call(idx_flat, pt_flat, cache_t, q_qh, bias, expand, knew, vnew, win_t, wnew, wnew_t, ocmp_qh,
                          gate_qh, *, layer, n_phys, n_pages, tq, n_cache_blocks):
    batch = q_qh.shape[0]
    n_ent = tq * N_SELECT
    b5 = lambda: pl.BlockSpec((1, 1, tq, 8, LANES), lambda b, g, i, p: (b, g, 0, 0, 0))
    b3 = lambda r, c: pl.BlockSpec((1, r, c), lambda b, g, i, p: (b, 0, 0))
    grid_spec = pltpu.PrefetchScalarGridSpec(
        num_scalar_prefetch=2,
        grid=(batch, N_KV_HEADS),
        in_specs=[pl.BlockSpec(memory_space=pl.ANY), b5(),
                  pl.BlockSpec((1, 1, 8, LANES), lambda b, g, i, p: (b, g, 0, 0)),
                  pl.BlockSpec(expand.shape, lambda b, g, i, p: (0, 0)),
                  b3(8, LANES), b3(8, LANES), b3(2 * LANES, WINDOW), b3(8, 2 * LANES), b3(2 * LANES, LANES),
                  b5(), b5()],
        out_specs=[b5(), b3(2 * LANES, WINDOW)],
        scratch_shapes=[pltpu.VMEM((2, n_ent, 2 * LANES, PAGE_SIZE), F32), pltpu.SemaphoreType.DMA((2,))],
    )
    return pl.pallas_call(
        functools.partial(_sel_attn_sample_kernel, page_base=layer * n_phys, n_pages=n_pages, tq=tq,
                          n_cache_blocks=n_cache_blocks, total=batch * N_KV_HEADS),
        grid_spec=grid_spec,
        out_shape=[jax.ShapeDtypeStruct((batch, N_KV_HEADS, tq, 8, LANES), F32),
                   jax.ShapeDtypeStruct((batch, 2 * LANES, WINDOW), F32)],
        compiler_params=pltpu.CompilerParams(dimension_semantics=("arbitrary", "arbitrary"),
                                             vmem_limit_bytes=VMEM_LIMIT),
        name="sel_attn_sample",
    )(idx_flat, pt_flat, cache_t, q_qh, bias, expand, knew, vnew, win_t, wnew, wnew_t, ocmp_qh, gate_qh)


def _prep_w_in(w):
    d = w.shape[0]
    wq = (w[:, 768:1280] * (HEAD_DIM ** -0.5 * LOG2E)).reshape(d, N_HEADS, HEAD_DIM)
    z = jnp.zeros_like(wq)
    grp = (jnp.arange(N_HEADS) // GQA)[None, :, None]
    wq = jnp.concatenate([jnp.where(grp == 0, wq, z), jnp.where(grp == 1, wq, z)], axis=-1).reshape(d, N_HEADS * LANES)
    gate = jnp.pad(w[:, 2048:2072], ((0, 0), (0, LANES - N_HEADS * N_BRANCH)))
    return jnp.concatenate([w[:, :768], wq, w[:, 1280:2048], gate, w[:, 2072:]], axis=1).astype(BF16)


def _prep_w1(w1):
    w = w1.reshape(2, 2, CMP_STRIDE, HEAD_DIM, CMP_HID)
    return jnp.transpose(w, (0, 2, 3, 1, 4)).reshape(2, CMP_STRIDE * HEAD_DIM, 2 * CMP_HID)


def _prep_w1_grouped(w1s):
    w = w1s.reshape(2, CMP_STRIDE, 1, HEAD_DIM, 1, 2 * CMP_HID)
    eye = jnp.eye(N_KV_HEADS, dtype=w.dtype).reshape(1, 1, N_KV_HEADS, 1, N_KV_HEADS, 1)
    return (w * eye).reshape(2, CMP_STRIDE * N_KV_HEADS * HEAD_DIM, N_KV_HEADS * 2 * CMP_HID)


def _prep_w2(w2):
    z = jnp.zeros_like(w2)
    return jnp.stack([jnp.concatenate([w2, z], axis=-1), jnp.concatenate([z, w2], axis=-1)], axis=1).astype(BF16)


def _cover_matrix(nc, ns_real, ns_pad):
    m = np.arange(nc)[:, None]
    b = np.arange(ns_pad)[None, :]
    return ((m >= 4 * b) & (m <= 4 * b + 4) & (m >= 1) & (b < ns_real)).astype(np.float32)


def _block_bias_matrix():
    k = np.arange(KEY_TILE)[:, None]
    b = np.arange(LANES)[None, :]
    return jnp.asarray(np.where(k // SLC_BLOCK == b, NEG_INF, 0.0).astype(np.float32), dtype=BF16)


def _expand_matrix():
    r = np.arange(LANES)[:, None]
    c = np.arange(N_SELECT * PAGE_SIZE)[None, :]
    return jnp.asarray((c // SLC_BLOCK == r).astype(np.float32), dtype=BF16)


def kernel(x_prompt, x_sample, cache_nsa_kv, state_win_kv, state_conv, state_ffn_conv, page_table, c_prompt, c_sample, w_ada, b_ada, w_in, conv_a_w, conv_a_b, cmp_pe, cmp_w1, cmp_w2, sgu_ln_g, sgu_ln_b, sgu_w, sgu_b, w_o, ln_g, ln_b, w_ffn_up, conv_f_w, conv_f_b, w_ffn_down):
    depth = w_in.shape[0]
    _, t, d_model = x_prompt.shape
    nb, tq, _ = x_sample.shape
    n_phys = cache_nsa_kv.shape[1]
    n_pages = page_table.shape[1]
    past = n_pages * PAGE_SIZE
    d_ff = w_ffn_down.shape[1]
    alpha = (2 * depth) ** 0.25
    rs = nb * tq
    kvw = 4 * N_KV_HEADS * HEAD_DIM
    assert x_prompt.shape[0] == 1 and c_prompt.shape[0] == 1
    assert d_model == 1024 and t % KEY_TILE == 0 and t >= WINDOW + Q_BLOCK
    assert tq == 4 and rs == GMLP_CHUNK and past % KEY_TILE == 0 and past >= WINDOW
    assert state_win_kv.shape[2] == WINDOW

    rc = -(-(1 + nb) // 8) * 8
    c_all = jnp.pad(jnp.concatenate([c_prompt, c_sample], axis=0), ((0, rc - 1 - nb), (0, 0)))
    mods = _ada_call(c_all, w_ada, b_ada)

    nc_p, ns_p = t // CMP_STRIDE, t // SLC_BLOCK
    covert_p = jnp.asarray(_cover_matrix(nc_p, ns_p, ns_p).T, dtype=BF16)
    nc_s = (past + tq) // CMP_STRIDE
    ns_s = -(-(past + tq) // SLC_BLOCK)
    ns_s_pad = -(-ns_s // LANES) * LANES
    n_cache_blocks = past // SLC_BLOCK
    cover_s = jnp.asarray(_cover_matrix(nc_s, ns_s, ns_s_pad), dtype=BF16)
    ebias = _block_bias_matrix()
    expand = _expand_matrix()
    pt_flat = page_table.reshape(-1)
    cache_t = jnp.transpose(cache_nsa_kv.reshape(depth * n_phys, PAGE_SIZE, kvw), (0, 2, 1))
    win_t_all = jnp.transpose(state_win_kv.reshape(depth, nb, WINDOW, 2 * LANES), (0, 1, 3, 2))

    xp = x_prompt[0]
    xs = jnp.transpose(x_sample, (1, 0, 2)).reshape(rs, d_model)
    tril_full = jnp.tril(jnp.ones((GMLP_CHUNK, GMLP_CHUNK), F32))
    tril_tq = jnp.tril(jnp.ones((tq, tq), F32))

    ps, ss = [], []
    for l in range(depth):
        w_all = _prep_w_in(w_in[l])
        wo_p = w_o[l].astype(BF16)
        wup = w_ffn_up[l].astype(BF16)
        wdn = w_ffn_down[l].astype(BF16)
        w1s = _prep_w1(cmp_w1[l])
        w1g = _prep_w1_grouped(w1s).astype(BF16)
        w1s = w1s.astype(BF16)
        w2p = _prep_w2(cmp_w2[l])
        pe_flat = jnp.broadcast_to(cmp_pe[l].reshape(2, 1, CMP_LEN * HEAD_DIM), (2, 8, CMP_LEN * HEAD_DIM))
        pet = _pe_term_call(pe_flat, cmp_w1[l].reshape(2, CMP_LEN * HEAD_DIM, CMP_HID))
        cw, cb = conv_a_w[l], conv_a_b[l].reshape(1, -1)
        cfw, cfb = conv_f_w[l], conv_f_b[l].reshape(1, -1)
        lng, lnb = sgu_ln_g[l].reshape(1, -1), sgu_ln_b[l].reshape(1, -1)

        def mod_rows(r0, r1, rep):
            parts = [mods[l, r0:r1, k * d_model:(k + 1) * d_model] for k in range(6)]
            return [jnp.tile(p_, (rep, 1)) if rep > 1 else p_ for p_ in parts]

        sh1, sc1, g1, sh2, sc2, g2 = mod_rows(0, 1, 1)
        wm_p = (sgu_w[l] * tril_full).astype(BF16)
        sb_p = jnp.repeat(sgu_b[l].T, HEAD_DIM, axis=1)
        (oa, oc, cst, qt, kvt, kcr, vcr, ksl, kwn, vslt, vwnt, gatet) = _in_proj_call(
            xp, sc1, sh1, w_all, cw, cb, jnp.zeros((2, 256), F32), lng, lnb, wm_p, sb_p,
            shift=1, tm=512, transposed=True, name="in_proj_prompt")
        kc, vct = _compress_prompt_call(kcr.reshape(nc_p, CMP_STRIDE * LANES), vcr.reshape(nc_p, CMP_STRIDE * LANES),
                                        w1g, pet, w2p)
        ob = _attn_prompt_call(qt, gatet, kc, vct, covert_p, ksl, vslt, kwn, vwnt, ebias)
        xp, fst = _out_ffn_call(xp, oa, ob, oc, g1, sc2, sh2, g2, wo_p, ln_g[l], ln_b[l], wup, cfw, cfb,
                                jnp.zeros((2, d_ff), F32), wdn, shift=1, tm=256, alpha=alpha, name="out_ffn_prompt")
        paged = kvt[:kvw].reshape(4, N_KV_HEADS, HEAD_DIM, t // PAGE_SIZE, PAGE_SIZE)
        winr = kvt[kvw:, t - WINDOW:].reshape(2, N_KV_HEADS, HEAD_DIM, WINDOW)
        ps.append((jnp.transpose(paged, (3, 4, 0, 1, 2))[None], jnp.transpose(winr, (3, 0, 1, 2))[None],
                   cst[None], fst[None]))

        sh1, sc1, g1, sh2, sc2, g2 = mod_rows(1, 1 + nb, tq)
        eye_b = jnp.eye(nb, dtype=F32)
        wm_s = jax.vmap(lambda w: jnp.kron(w[:tq, :tq] * tril_tq, eye_b))(sgu_w[l]).astype(BF16)
        sb_s = jnp.repeat(jnp.repeat(sgu_b[l][:, :tq].T, nb, axis=0), HEAD_DIM, axis=1)
        cpast = jnp.transpose(state_conv[l], (1, 0, 2)).reshape(2 * nb, -1)
        (oa, oc, cst, q, kvf, ksl, vsl, gate, vrow) = _in_proj_call(
            xs, sc1, sh1, w_all, cw, cb, cpast, lng, lnb, wm_s, sb_s, shift=nb, tm=rs, transposed=False,
            name="in_proj_sample")
        kc, vc = _cmp_stream_call(pt_flat, cache_t, w1s, pet, w2p, layer=l, n_phys=n_phys, batch=nb,
                                  n_pages=n_pages, pg=min(32, n_pages))

        def by_batch(a):
            return jnp.transpose(a.reshape(tq, nb, -1), (1, 0, 2))

        qb5 = by_batch(q).reshape(nb, tq, N_KV_HEADS, GQA, LANES)
        q_hq = jnp.pad(jnp.transpose(qb5, (0, 2, 3, 1, 4)), ((0, 0), (0, 0), (0, 0), (0, 8 - tq), (0, 0)))
        q_hq = q_hq.reshape(nb, N_KV_HEADS, 32, LANES)
        q_qh = jnp.pad(jnp.transpose(qb5, (0, 2, 1, 3, 4)), ((0, 0), (0, 0), (0, 0), (0, 8 - GQA), (0, 0)))
        ocmp, idx, bias = _cmp_attn_sample_call(q_hq, kc, vc, cover_s, past=past, ns=ns_s,
                                                n_cache_blocks=n_cache_blocks)
        ocmp_qh = jnp.transpose(ocmp.reshape(nb, N_KV_HEADS, GQA, 8, LANES)[:, :, :, :tq], (0, 1, 3, 2, 4))
        ocmp_qh = jnp.pad(ocmp_qh, ((0, 0), (0, 0), (0, 0), (0, 8 - GQA), (0, 0)))
        g5 = by_batch(gate)[:, :, :N_HEADS * N_BRANCH].reshape(nb, tq, N_KV_HEADS, GQA, N_BRANCH)
        gate_qh = jnp.pad(jnp.transpose(g5, (0, 2, 1, 3, 4)),
                          ((0, 0), (0, 0), (0, 0), (0, 8 - GQA), (0, LANES - N_BRANCH)))
        pad8 = lambda a: jnp.pad(by_batch(a), ((0, 0), (0, 8 - tq), (0, 0)))
        wnew_rows = by_batch(kvf[:, kvw:])
        wnew_t = jnp.pad(jnp.transpose(wnew_rows, (0, 2, 1)), ((0, 0), (0, 0), (LANES - tq, 0)))
        o5, wout_t = _sel_attn_sample_call(
            idx[:, :, :tq, :N_SELECT].reshape(-1), pt_flat, cache_t, q_qh, bias, expand, pad8(ksl), pad8(vsl),
            win_t_all[l], jnp.pad(wnew_rows, ((0, 0), (0, 8 - tq), (0, 0))), wnew_t, ocmp_qh, gate_qh,
            layer=l, n_phys=n_phys, n_pages=n_pages, tq=tq, n_cache_blocks=n_cache_blocks)
        o5 = o5[:, :, :, :GQA, :HEAD_DIM] + o5[:, :, :, :GQA, HEAD_DIM:]
        ob = jnp.transpose(o5, (2, 0, 1, 3, 4)).reshape(rs, N_HEADS * HEAD_DIM).astype(BF16)
        fpast = jnp.transpose(state_ffn_conv[l], (1, 0, 2)).reshape(2 * nb, -1)
        xs, fst = _out_ffn_call(xs, oa, ob, oc, g1, sc2, sh2, g2, wo_p, ln_g[l], ln_b[l], wup, cfw, cfb, fpast, wdn,
                                shift=nb, tm=rs, alpha=alpha, name="out_ffn_sample")
        ss.append((by_batch(kvf[:, :kvw]).reshape(nb, tq, 4, N_KV_HEADS, HEAD_DIM),
                   jnp.transpose(wout_t, (0, 2, 1)).reshape(nb, WINDOW, 2, N_KV_HEADS, HEAD_DIM),
                   jnp.transpose(cst.reshape(2, nb, -1), (1, 0, 2)),
                   jnp.transpose(fst.reshape(2, nb, -1), (1, 0, 2)),
                   by_batch(vrow)))

    ys = jnp.transpose(xs.reshape(tq, nb, d_model), (1, 0, 2))
    return (xp[None], ys,
            jnp.stack([s[0] for s in ps]), jnp.stack([s[1] for s in ps]),
            jnp.stack([s[2] for s in ps]), jnp.stack([s[3] for s in ps]),
            jnp.stack([s[0] for s in ss]), jnp.stack([s[1] for s in ss]),
            jnp.stack([s[2] for s in ss]), jnp.stack([s[3] for s in ss]),
            jnp.stack([s[4] for s in ss]))
```

```python
import functools
import math

import numpy as np
import jax
import jax.numpy as jnp
from jax import lax
from jax.experimental import pallas as pl
from jax.experimental.pallas import tpu as pltpu

F32 = jnp.float32
BF16 = jnp.bfloat16

HEAD_DIM = 64
N_HEADS = 8
N_KV_HEADS = 2
GQA = N_HEADS // N_KV_HEADS
N_BRANCH = 3
CONV_K = 3
CMP_LEN = 32
CMP_STRIDE = 16
CMP_HID = 128
SLC_BLOCK = 64
N_SELECT = 16
WINDOW = 512
Q_BLOCK = 128
PAGE_SIZE = 128
GMLP_CHUNK = 128
GMLP_GROUPS = 4
LN_EPS = 1e-5
NEG_INF = -1e30
FORCE = 1e4
REMOVED = -3e38
LOG2E = 1.4426950408889634

LANES = 128
KEY_TILE = 512
BLOCKS_PER_TILE = KEY_TILE // SLC_BLOCK
NS_ROWS = 16
TILE_UNROLL = 2
CAUSAL_VARIANTS = 4
VMEM_LIMIT = 56 * 1024 * 1024


def _dot(a, b):
    return jnp.dot(a, b, preferred_element_type=F32)


def _dot_nt(a, b):
    return lax.dot_general(a, b, (((1,), (1,)), ((), ())), preferred_element_type=F32)


def _split(a):
    hi = a.astype(BF16)
    lo = (a - hi.astype(F32)).astype(BF16)
    return hi, lo


def _dot3(a, b):
    ah, al = _split(a)
    bh, bl = _split(b)
    return _dot(ah, bh) + _dot(ah, bl) + _dot(al, bh)


def _sigmoid(x):
    return 1.0 / (1.0 + jnp.exp(-x))


def _gelu(x):
    c = math.sqrt(2.0 / math.pi)
    return 0.5 * x * (1.0 + jnp.tanh(c * (x + 0.044715 * (x * x * x))))


def _layer_norm(x, g, b):
    mu = jnp.mean(x, axis=-1, keepdims=True)
    xc = x - mu
    var = jnp.mean(xc * xc, axis=-1, keepdims=True)
    return xc * lax.rsqrt(var + LN_EPS) * g + b


def _masked_softmax(s, mask, axis):
    sm = jnp.where(mask, s, NEG_INF)
    mx = jnp.max(sm, axis=axis, keepdims=True)
    e = jnp.where(mask, jnp.exp2(s - mx), 0.0)
    l = jnp.sum(e, axis=axis, keepdims=True)
    return e * (1.0 / jnp.where(l > 0.0, l, 1.0))


def _softmax_rows(sm, col_valid):
    e = jnp.exp2(sm - jnp.max(sm, axis=0, keepdims=True))
    inv = 1.0 / jnp.sum(e, axis=0, keepdims=True)
    if col_valid is not None:
        inv = jnp.where(col_valid, inv, 0.0)
    return e * inv


def _top_select(val, blk_f, n_blk, axis):
    sel = jnp.zeros_like(val)
    firsts, tops = [], []
    for _ in range(N_SELECT):
        mx = jnp.max(val, axis=axis, keepdims=True)
        first = jnp.min(jnp.where(val == mx, blk_f, float(n_blk)), axis=axis, keepdims=True)
        hit = blk_f == first
        sel = jnp.where(hit, 1.0, sel)
        val = jnp.where(hit, REMOVED, val)
        firsts.append(first)
        tops.append(mx)
    return sel, firsts, tops


def _shifted_conv(src_ref, x, w_ref, b_ref, pad, shift, rows):
    x2 = src_ref[pl.ds(pad - 2 * shift, rows), :]
    x1 = src_ref[pl.ds(pad - shift, rows), :]
    return w_ref[0:1, :] * x2 + w_ref[1:2, :] * x1 + w_ref[2:3, :] * x + b_ref[...]


def _ada_kernel(c_ref, w_ref, b_ref, o_ref):
    c = c_ref[...]
    o_ref[0] = _dot3(c * _sigmoid(c), w_ref[0]) + b_ref[0]


def _ada_call(c_all, w_ada, b_ada):
    depth, d_model, n_mod = w_ada.shape
    rc = c_all.shape[0]
    tn = 1024
    return pl.pallas_call(
        _ada_kernel,
        grid=(depth, n_mod // tn),
        in_specs=[pl.BlockSpec((rc, d_model), lambda l, n: (0, 0)),
                  pl.BlockSpec((1, d_model, tn), lambda l, n: (l, 0, n)),
                  pl.BlockSpec((1, 1, tn), lambda l, n: (l, 0, n))],
        out_specs=pl.BlockSpec((1, rc, tn), lambda l, n: (l, 0, n)),
        out_shape=jax.ShapeDtypeStruct((depth, rc, n_mod), F32),
        compiler_params=pltpu.CompilerParams(dimension_semantics=("arbitrary", "arbitrary"),
                                             vmem_limit_bytes=VMEM_LIMIT),
        name="ada_mod",
    )(c_all, w_ada, b_ada.reshape(depth, 1, n_mod))


_C_AB, _C_AC, _C_AH = 0, 256, 512
_C_Q = 768
_C_KV = 1792
_C_GATE = 2560
_C_GU = 2688
_C_GV = 2944
_N_COL = 3200
_N_SLOT = 6


def _in_proj_kernel(x_ref, sc_ref, sh_ref, w_ref, cw_ref, cb_ref, cpast_ref, lng_ref, lnb_ref, wm_ref, sb_ref,
                    *rest, shift, tm, pad, transposed):
    if transposed:
        (oa_ref, oc_ref, cstate_ref, qt_ref, kvp_ref, kvt_ref, kcr_ref, vcr_ref, ksl_ref, kwn_ref, vslt_ref, vwnt_ref,
         gatet_ref, zs_ref) = rest
    else:
        (oa_ref, oc_ref, cstate_ref, q_ref, kvf_ref, ksl_ref, vsl_ref, gate_ref, vrow_ref, zs_ref) = rest
    i = pl.program_id(0)

    @pl.when(i == 0)
    def _():
        zs_ref[pl.ds(pad - 2 * shift, 2 * shift), :] = cpast_ref[...]

    h = (x_ref[...] * (1.0 + sc_ref[...]) + sh_ref[...]).astype(BF16)
    p = _dot(h, w_ref[...])

    z = p[:, _C_AC:_C_AC + 256] * p[:, _C_AH:_C_AH + 256]
    zs_ref[pl.ds(pad, tm), :] = z
    y = _shifted_conv(zs_ref, z, cw_ref, cb_ref, pad, shift, tm)
    oa_ref[...] = (p[:, _C_AB:_C_AB + 256] * y).astype(BF16)
    tail = zs_ref[pl.ds(pad + tm - 2 * shift, 2 * shift), :]
    cstate_ref[...] = tail
    zs_ref[pl.ds(pad - 2 * shift, 2 * shift), :] = tail

    kv = [p[:, _C_KV + k * LANES:_C_KV + (k + 1) * LANES] for k in range(_N_SLOT)]
    gate = _sigmoid(p[:, _C_GATE:_C_GATE + LANES])
    if transposed:
        for hh in range(N_HEADS):
            qt_ref[pl.ds(hh * LANES, LANES), :] = p[:, _C_Q + hh * LANES:_C_Q + (hh + 1) * LANES].T.astype(BF16)
        kvt = [a.T for a in kv]
        for k in range(4):
            for pg in range(tm // PAGE_SIZE):
                kvp_ref[pg, pl.ds(k * LANES, LANES), :] = kvt[k][:, pg * PAGE_SIZE:(pg + 1) * PAGE_SIZE]
        for k in range(4, _N_SLOT):
            kvt_ref[pl.ds((k - 4) * LANES, LANES), :] = kvt[k]
        kcr_ref[...] = kv[0].astype(BF16)
        vcr_ref[...] = kv[1].astype(BF16)
        ksl_ref[...] = kv[2].astype(BF16)
        kwn_ref[...] = kv[4].astype(BF16)
        vslt_ref[...] = kvt[3].astype(BF16)
        vwnt_ref[...] = kvt[5].astype(BF16)
        gatet_ref[...] = gate.T
    else:
        q_ref[...] = p[:, _C_Q:_C_Q + 1024].astype(BF16)
        kvf_ref[...] = p[:, _C_KV:_C_KV + _N_SLOT * LANES]
        ksl_ref[...] = kv[2].astype(BF16)
        vsl_ref[...] = kv[3].astype(BF16)
        gate_ref[...] = gate

    u = _gelu(p[:, _C_GU:_C_GU + 256])
    v = _layer_norm(_gelu(p[:, _C_GV:_C_GV + 256]), lng_ref[...], lnb_ref[...])
    if not transposed:
        vrow_ref[...] = v
    lane = lax.broadcasted_iota(jnp.int32, (1, 256), 1)
    for c in range(tm // GMLP_CHUNK):
        vc = v[c * GMLP_CHUNK:(c + 1) * GMLP_CHUNK]
        mixed = sb_ref[...]
        for g in range(GMLP_GROUPS):
            vg = jnp.where((lane >= g * HEAD_DIM) & (lane < (g + 1) * HEAD_DIM), vc, 0.0).astype(BF16)
            mixed = mixed + _dot(wm_ref[g], vg)
        oc_ref[pl.ds(c * GMLP_CHUNK, GMLP_CHUNK), :] = (u[c * GMLP_CHUNK:(c + 1) * GMLP_CHUNK] * mixed).astype(BF16)


def _in_proj_call(x, sc, sh, w_all, cw, cb, cpast, lng, lnb, wm, sb, *, shift, tm, transposed, name):
    rows, d_model = x.shape
    pad = max(8, 2 * shift)
    mr = sc.shape[0]
    mod_spec = (pl.BlockSpec((1, d_model), lambda i: (0, 0)) if mr == 1
                else pl.BlockSpec((tm, d_model), lambda i: (i, 0)))

    def row_spec(n):
        return pl.BlockSpec((tm, n), lambda i: (i, 0))

    def col_spec(n):
        return pl.BlockSpec((n, tm), lambda i: (0, i))

    def full(a):
        nd = a.ndim
        return pl.BlockSpec(a.shape, lambda i: (0,) * nd)

    sds = jax.ShapeDtypeStruct
    out_shape = [sds((rows, 256), BF16), sds((rows, 256), BF16), sds((2 * shift, 256), F32)]
    out_specs = [row_spec(256), row_spec(256), pl.BlockSpec((2 * shift, 256), lambda i: (0, 0))]
    if transposed:
        out_shape += [sds((N_HEADS * LANES, rows), BF16), sds((rows // PAGE_SIZE, 4 * LANES, PAGE_SIZE), F32),
                      sds((2 * LANES, rows), F32)]
        out_specs += [col_spec(N_HEADS * LANES),
                      pl.BlockSpec((tm // PAGE_SIZE, 4 * LANES, PAGE_SIZE), lambda i: (i, 0, 0)), col_spec(2 * LANES)]
        out_shape += [sds((rows, LANES), BF16)] * 4 + [sds((LANES, rows), BF16)] * 2 + [sds((LANES, rows), F32)]
        out_specs += [row_spec(LANES)] * 4 + [col_spec(LANES)] * 3
    else:
        out_shape += [sds((rows, 1024), BF16), sds((rows, _N_SLOT * LANES), F32), sds((rows, LANES), BF16),
                      sds((rows, LANES), BF16), sds((rows, LANES), F32), sds((rows, 256), F32)]
        out_specs += [row_spec(1024), row_spec(_N_SLOT * LANES), row_spec(LANES), row_spec(LANES), row_spec(LANES),
                      row_spec(256)]
    return pl.pallas_call(
        functools.partial(_in_proj_kernel, shift=shift, tm=tm, pad=pad, transposed=transposed),
        grid=(rows // tm,),
        in_specs=[row_spec(d_model), mod_spec, mod_spec, full(w_all), full(cw), full(cb), full(cpast),
                  full(lng), full(lnb), full(wm), full(sb)],
        out_specs=out_specs,
        out_shape=out_shape,
        scratch_shapes=[pltpu.VMEM((pad + tm, 256), F32)],
        compiler_params=pltpu.CompilerParams(dimension_semantics=("arbitrary",), vmem_limit_bytes=VMEM_LIMIT),
        name=name,
    )(x, sc, sh, w_all, cw, cb, cpast, lng, lnb, wm, sb)


def _pe_term_kernel(pe_ref, w1_ref, o_ref):
    for s in range(2):
        o_ref[s] = _dot3(pe_ref[s], w1_ref[s])


def _pe_term_call(pe_flat, w1_flat):
    return pl.pallas_call(
        _pe_term_kernel,
        out_shape=jax.ShapeDtypeStruct((2, 8, CMP_HID), F32),
        compiler_params=pltpu.CompilerParams(vmem_limit_bytes=VMEM_LIMIT),
        name="cmp_pe_term",
    )(pe_flat, w1_flat)


def _compress_prompt_kernel(kx_ref, vx_ref, w1_ref, pet_ref, w2_ref, kc_ref, vct_ref, sh_ref, *, nc):
    sh_ref[pl.ds(0, 8), :] = jnp.zeros((8, CMP_HID), F32)
    for s, x_ref in enumerate((kx_ref, vx_ref)):
        parts = _dot(x_ref[...], w1_ref[s])
        acc = jnp.zeros((nc, LANES), F32)
        for g in range(N_KV_HEADS):
            p0 = parts[:, g * 256:g * 256 + CMP_HID]
            p1 = parts[:, g * 256 + CMP_HID:(g + 1) * 256]
            sh_ref[pl.ds(8, nc), :] = p0
            p0s = sh_ref[pl.ds(7, nc), :]
            hid = _gelu(p0s + p1 + pet_ref[s][0:1, :])
            acc = acc + _dot(hid.astype(BF16), w2_ref[s, g])
        if s == 0:
            kc_ref[...] = acc.astype(BF16)
        else:
            vct_ref[...] = acc.T.astype(BF16)


def _compress_prompt_call(kx, vx, w1p, pet, w2p):
    nc = kx.shape[0]
    return pl.pallas_call(
        functools.partial(_compress_prompt_kernel, nc=nc),
        out_shape=[jax.ShapeDtypeStruct((nc, LANES), BF16), jax.ShapeDtypeStruct((LANES, nc), BF16)],
        scratch_shapes=[pltpu.VMEM((nc + 8, CMP_HID), F32)],
        compiler_params=pltpu.CompilerParams(vmem_limit_bytes=VMEM_LIMIT),
        name="compress_prompt",
    )(kx, vx, w1p, pet, w2p)


def _attn_prompt_kernel(qt_ref, gatet_ref, kc_ref, vct_ref, covert_ref, ksl_ref, vslt_ref, kwn_ref, vwnt_ref, eb_ref,
                        o_ref, ns_ref, rhs_ref, sa_ref, sb_ref, ma_ref, mb_ref, ocmp_ref, *, nc, ns):
    qb = pl.program_id(0)
    s0 = qb * Q_BLOCK
    ncol = GQA * Q_BLOCK
    pair = 2 * Q_BLOCK
    col = lax.broadcasted_iota(jnp.int32, (1, ncol), 1)
    qpos_c = s0 + (col & (Q_BLOCK - 1))
    qp = s0 + lax.broadcasted_iota(jnp.int32, (1, Q_BLOCK), 1)
    m_idx = lax.broadcasted_iota(jnp.int32, (nc, 1), 0)
    blk = lax.broadcasted_iota(jnp.int32, (ns, 1), 0)
    blk_f = blk.astype(F32)
    key_row = lax.broadcasted_iota(jnp.int32, (KEY_TILE, 1), 0)
    n_tiles = s0 // KEY_TILE + 1

    ns_ref[pl.ds(ns, 8), :] = jnp.zeros((8, Q_BLOCK), F32)
    rhs_ref[pl.ds(LANES + NS_ROWS, 2 * LANES - LANES - NS_ROWS), :] = jnp.zeros((LANES - NS_ROWS, ncol), BF16)

    for g in (pl.program_id(1),):
        rq = jnp.concatenate([qt_ref[pl.ds(h * LANES, LANES), :] for h in range(GQA)], axis=1)
        rhs_ref[pl.ds(0, LANES), :] = rq

        vrows = pl.ds(pl.multiple_of(g * HEAD_DIM, HEAD_DIM), HEAD_DIM)

        def compressed_and_select(frac):
            nr, nsr = nc * frac // CAUSAL_VARIANTS, ns * frac // CAUSAL_VARIANTS

            def run():
                m_i = m_idx[0:nr]
                last_pos = jnp.where(m_i >= 1, CMP_STRIDE * (m_i - 1) + CMP_LEN - 1, 2 ** 30)
                sc = jnp.where(last_pos <= qpos_c, _dot(kc_ref[pl.ds(0, nr), :], rq), NEG_INF)
                p = _softmax_rows(sc, qpos_c >= CMP_LEN - 1)
                o_c = _dot(vct_ref[vrows, pl.ds(0, nr)], p.astype(BF16))
                p4 = (p[:, 0:Q_BLOCK] + p[:, Q_BLOCK:2 * Q_BLOCK] + p[:, 2 * Q_BLOCK:3 * Q_BLOCK]
                      + p[:, 3 * Q_BLOCK:4 * Q_BLOCK])
                hi, lo = _split(p4)
                cov = covert_ref[pl.ds(0, nsr), pl.ds(0, nr)]
                imp = _dot(cov, hi) + _dot(cov, lo)
                b_i, b_f = blk[0:nsr], blk_f[0:nsr]
                qblk = qp // SLC_BLOCK
                elig = b_i * SLC_BLOCK <= qp
                forced = (b_i == 0) | (b_i == qblk) | (b_i == qblk - 1)
                val = jnp.where(elig, jnp.where(forced, REMOVED, imp), -FORCE)
                for _ in range(N_SELECT - 3):
                    mx = jnp.max(val, axis=0, keepdims=True)
                    first = jnp.min(jnp.where(val == mx, b_f, float(ns)), axis=0, keepdims=True)
                    val = jnp.where(b_f == first, REMOVED, val)
                ns_ref[pl.ds(0, nsr), :] = jnp.where(elig & (val == REMOVED), 0.0, 1.0)
                if nsr < ns:
                    ns_ref[pl.ds(nsr, ns - nsr), :] = jnp.ones((ns - nsr, Q_BLOCK), F32)
                ocmp_ref[...] = o_c
            return run

        variant = (qb * CAUSAL_VARIANTS) // (nc * CMP_STRIDE // Q_BLOCK)
        for f in range(CAUSAL_VARIANTS):
            pl.when(variant == f)(compressed_and_select(f + 1))
        o_cmp = ocmp_ref[...]

        def tile_scores(kt):
            k0 = pl.multiple_of(kt * KEY_TILE, KEY_TILE)
            nsf = ns_ref[pl.ds(pl.multiple_of(kt * BLOCKS_PER_TILE, BLOCKS_PER_TILE), NS_ROWS), :].astype(BF16)
            rhs_ref[pl.ds(LANES, NS_ROWS), :] = jnp.concatenate([nsf] * GQA, axis=1)
            lhs = jnp.concatenate([ksl_ref[pl.ds(k0, KEY_TILE), :], eb_ref[...]], axis=1)
            return _dot(lhs, rhs_ref[...])

        ones_rows = jnp.ones((NS_ROWS, KEY_TILE), BF16)

        def flash_update(s, s_max, k0, carry):
            m_run, acc = carry
            m_new = jnp.maximum(m_run, s_max)
            pe = jnp.exp2(s - m_new).astype(BF16)
            v_aug = jnp.concatenate([vslt_ref[vrows, pl.ds(k0, KEY_TILE)], ones_rows], axis=0)
            return m_new, jnp.exp2(m_run - m_new) * acc + _dot(v_aug, pe)

        def produce(dst_ref, dmx_ref, kt):
            s = tile_scores(kt)
            dst_ref[...] = s
            dmx_ref[...] = jnp.max(s, axis=0, keepdims=True)

        def update_from(src_ref, smx_ref, kt, carry):
            k0 = pl.multiple_of(kt * KEY_TILE, KEY_TILE)
            return tuple(flash_update(src_ref[:, pl.ds(hp * pair, pair)], smx_ref[:, pl.ds(hp * pair, pair)], k0,
                                      carry[hp]) for hp in range(GQA // 2))

        produce(sa_ref, ma_ref, 0)

        def chain(kt, n, carry):
            bufs = ((sa_ref, ma_ref), (sb_ref, mb_ref))
            for i in range(n):
                produce(*bufs[(i + 1) % 2], kt + i + 1)
                carry = update_from(*bufs[i % 2], kt + i, carry)
            return carry

        init = (jnp.full((1, pair), NEG_INF, F32), jnp.zeros((HEAD_DIM + NS_ROWS, pair), F32))
        last = n_tiles - 1
        quads = last // TILE_UNROLL
        carry = lax.fori_loop(0, quads, lambda j, c: chain(TILE_UNROLL * j, TILE_UNROLL, c), (init,) * (GQA // 2))
        done = TILE_UNROLL * quads
        n = TILE_UNROLL // 2
        while n >= 2:
            take = last - done >= n
            carry = lax.cond(take, lambda c, done=done, n=n: chain(done, n, c), lambda c: c, carry)
            done = done + jnp.where(take, n, 0)
            n //= 2

        def odd_step(carry):
            carry = update_from(sa_ref, ma_ref, last - 1, carry)
            produce(sa_ref, ma_ref, last)
            return carry

        carry = lax.cond(lax.rem(last, 2) == 1, odd_step, lambda c: c, carry)
        k0 = pl.multiple_of(last * KEY_TILE, KEY_TILE)
        o_slc_parts = []
        for hp in range(GQA // 2):
            s = sa_ref[:, pl.ds(hp * pair, pair)]
            s = jnp.where(k0 + key_row <= qpos_c[:, hp * pair:(hp + 1) * pair], s, NEG_INF)
            _, acc = flash_update(s, jnp.max(s, axis=0, keepdims=True), k0, carry[hp])
            o_slc_parts.append(acc[0:HEAD_DIM] * (1.0 / acc[HEAD_DIM:HEAD_DIM + 1]))
        o_slc = jnp.concatenate(o_slc_parts, axis=1)

        w0 = pl.multiple_of(jnp.maximum(s0 - WINDOW, 0), Q_BLOCK)
        dist = qp - (w0 + lax.broadcasted_iota(jnp.int32, (WINDOW + Q_BLOCK, 1), 0))
        wbias = jnp.where(lax.bitcast_convert_type(dist, jnp.uint32) <= WINDOW, 0.0, NEG_INF)
        sw = _dot(kwn_ref[pl.ds(w0, WINDOW + Q_BLOCK), :], rq) + jnp.concatenate([wbias] * GQA, axis=1)
        pw = _softmax_rows(sw, None)
        o_win = _dot(vwnt_ref[vrows, pl.ds(w0, WINDOW + Q_BLOCK)], pw.astype(BF16))

        outs = []
        for h in range(GQA):
            c = slice(h * Q_BLOCK, (h + 1) * Q_BLOCK)
            gr = (GQA * g + h) * N_BRANCH
            outs.append(gatet_ref[pl.ds(gr, 1), :] * o_cmp[:, c] + gatet_ref[pl.ds(gr + 1, 1), :] * o_slc[:, c]
                        + gatet_ref[pl.ds(gr + 2, 1), :] * o_win[:, c])
        for hp in range(GQA // 2):
            o2 = jnp.concatenate(outs[2 * hp:2 * hp + 2], axis=0)
            o_ref[:, hp * LANES:(hp + 1) * LANES] = o2.T.astype(BF16)


def _attn_prompt_call(qt, gatet, kc, vct, covert, ksl, vslt, kwn, vwnt, ebias):
    t = qt.shape[1]
    ns, nc = covert.shape
    vmem = pl.BlockSpec(memory_space=pltpu.VMEM)
    return pl.pallas_call(
        functools.partial(_attn_prompt_kernel, nc=nc, ns=ns),
        grid=(t // Q_BLOCK, N_KV_HEADS),
        in_specs=[pl.BlockSpec((GQA * LANES, Q_BLOCK), lambda i, g: (g, i)),
                  pl.BlockSpec((LANES, Q_BLOCK), lambda i, g: (0, i)),
                  vmem, vmem, vmem, vmem, vmem, vmem, vmem, vmem],
        out_specs=pl.BlockSpec((Q_BLOCK, GQA * HEAD_DIM), lambda i, g: (i, g)),
        out_shape=jax.ShapeDtypeStruct((t, N_HEADS * HEAD_DIM), BF16),
        scratch_shapes=[pltpu.VMEM((ns + 8, Q_BLOCK), F32), pltpu.VMEM((2 * LANES, GQA * Q_BLOCK), BF16),
                        pltpu.VMEM((KEY_TILE, GQA * Q_BLOCK), F32), pltpu.VMEM((KEY_TILE, GQA * Q_BLOCK), F32),
                        pltpu.VMEM((1, GQA * Q_BLOCK), F32), pltpu.VMEM((1, GQA * Q_BLOCK), F32),
                        pltpu.VMEM((HEAD_DIM, GQA * Q_BLOCK), F32)],
        compiler_params=pltpu.CompilerParams(dimension_semantics=("arbitrary", "arbitrary"),
                                             vmem_limit_bytes=VMEM_LIMIT),
        name="attn_prompt",
    )(qt, gatet, kc, vct, covert, ksl, vslt, kwn, vwnt, ebias)


def _out_ffn_kernel(x_ref, oa_ref, ob_ref, oc_ref, g1_ref, sc2_ref, sh2_ref, g2_ref, wo_ref, lng_ref, lnb_ref,
                    wup_ref, cfw_ref, cfb_ref, fpast_ref, wdn_ref, xo_ref, fstate_ref, us_ref,
                    *, shift, tm, pad, alpha, d_ff):
    i = pl.program_id(0)

    @pl.when(i == 0)
    def _():
        us_ref[pl.ds(pad - 2 * shift, 2 * shift), :] = fpast_ref[...]

    mix = _dot(jnp.concatenate([oa_ref[...], ob_ref[...], oc_ref[...]], axis=1), wo_ref[...])
    x1 = _layer_norm(alpha * x_ref[...] + (1.0 + g1_ref[...]) * mix, lng_ref[0:1, :], lnb_ref[0:1, :])
    h2 = (x1 * (1.0 + sc2_ref[...]) + sh2_ref[...]).astype(BF16)
    up = _dot(h2, wup_ref[...])
    ua = up[:, :d_ff]
    us_ref[pl.ds(pad, tm), :] = ua
    yc = _shifted_conv(us_ref, ua, cfw_ref, cfb_ref, pad, shift, tm)
    tail = us_ref[pl.ds(pad + tm - 2 * shift, 2 * shift), :]
    fstate_ref[...] = tail
    us_ref[pl.ds(pad - 2 * shift, 2 * shift), :] = tail
    act = (yc * _sigmoid(yc) * up[:, d_ff:]).astype(BF16)
    y = _dot(act, wdn_ref[...])
    xo_ref[...] = _layer_norm(alpha * x1 + (1.0 + g2_ref[...]) * y, lng_ref[1:2, :], lnb_ref[1:2, :])


def _out_ffn_call(x, oa, ob, oc, g1, sc2, sh2, g2, wo, lng, lnb, wup, cfw, cfb, fpast, wdn, *, shift, tm, alpha, name):
    rows, d_model = x.shape
    d_ff = wdn.shape[0]
    pad = max(8, 2 * shift)
    mr = g1.shape[0]
    mod_spec = (pl.BlockSpec((1, d_model), lambda i: (0, 0)) if mr == 1
                else pl.BlockSpec((tm, d_model), lambda i: (i, 0)))

    def row_spec(n):
        return pl.BlockSpec((tm, n), lambda i: (i, 0))

    vmem = pl.BlockSpec(memory_space=pltpu.VMEM)
    return pl.pallas_call(
        functools.partial(_out_ffn_kernel, shift=shift, tm=tm, pad=pad, alpha=alpha, d_ff=d_ff),
        grid=(rows // tm,),
        in_specs=[row_spec(d_model), row_spec(256), row_spec(N_HEADS * HEAD_DIM), row_spec(256), mod_spec, mod_spec, mod_spec,
                  mod_spec, vmem, vmem, vmem, vmem, vmem, vmem, vmem, vmem],
        out_specs=[row_spec(d_model), pl.BlockSpec((2 * shift, d_ff), lambda i: (0, 0))],
        out_shape=[jax.ShapeDtypeStruct((rows, d_model), F32), jax.ShapeDtypeStruct((2 * shift, d_ff), F32)],
        scratch_shapes=[pltpu.VMEM((pad + tm, d_ff), F32)],
        compiler_params=pltpu.CompilerParams(dimension_semantics=("arbitrary",), vmem_limit_bytes=VMEM_LIMIT),
        name=name,
    )(x, oa, ob, oc, g1, sc2, sh2, g2, wo, lng, lnb, wup, cfw, cfb, fpast, wdn)


def _cmp_stream_kernel(pt_ref, cache_ref, w1_ref, pet_ref, w2_ref, kc_ref, vc_ref, buf, rbuf, sem, carry,
                       *, page_base, n_pages, pg, n_groups, total):
    b = pl.program_id(0)
    gi = pl.program_id(1)
    step = b * n_groups + gi
    slot = lax.rem(step, 2)
    m = pg * (PAGE_SIZE // CMP_STRIDE)

    def page_copy(page, slt, i):
        return pltpu.make_async_copy(cache_ref.at[page, pl.ds(0, 2 * LANES), :], buf.at[slt, i], sem.at[slt])

    def issue(stp, slt):
        base = lax.div(stp, n_groups) * n_pages + lax.rem(stp, n_groups) * pg
        for i in range(pg):
            page_copy(page_base + pt_ref[base + i], slt, i).start()

    @pl.when(step == 0)
    def _():
        carry[...] = jnp.zeros(carry.shape, F32)
        issue(step, slot)

    @pl.when(step + 1 < total)
    def _():
        issue(step + 1, 1 - slot)

    for i in range(pg):
        page_copy(0, slot, i).wait()

    for i in range(pg):
        for s in range(2):
            rbuf[s, pl.ds(i * PAGE_SIZE, PAGE_SIZE), :] = buf[slot, i, pl.ds(s * LANES, LANES), :].T

    lane = lax.broadcasted_iota(jnp.int32, (1, LANES), 1)
    first_half = lane < HEAD_DIM
    row0 = lax.broadcasted_iota(jnp.int32, (m, 1), 0) == 0
    pieces = [[[], []], [[], []]]
    for pr in range(CMP_STRIDE // 2):
        for s in range(2):
            a = rbuf[s, pl.ds(2 * pr, m, stride=CMP_STRIDE), :]
            bb = rbuf[s, pl.ds(2 * pr + 1, m, stride=CMP_STRIDE), :]
            pieces[s][0].append(jnp.where(first_half, a, pltpu.roll(bb, HEAD_DIM, 1)).astype(BF16))
            pieces[s][1].append(jnp.where(first_half, pltpu.roll(a, HEAD_DIM, 1), bb).astype(BF16))
    for s, o_ref in enumerate((kc_ref, vc_ref)):
        acc = jnp.zeros((m, LANES), F32)
        for g in range(N_KV_HEADS):
            parts = _dot(jnp.concatenate(pieces[s][g], axis=1), w1_ref[s])
            p0 = parts[:, :CMP_HID]
            p1 = parts[:, CMP_HID:]
            prev = jnp.where(gi == 0, 0.0, carry[s * 2 + g][0:1, :])
            p0s = jnp.where(row0, prev, pltpu.roll(p0, 1, 0))
            carry[s * 2 + g] = jnp.broadcast_to(p0[m - 1:m, :], (8, CMP_HID))
            hid = _gelu(p0s + p1 + pet_ref[s][0:1, :])
            acc = acc + _dot(hid.astype(BF16), w2_ref[s, g])
        o_ref[0] = acc.astype(BF16)


def _cmp_stream_call(pt_flat, cache_t, w1s, pet, w2p, *, layer, n_phys, batch, n_pages, pg):
    n_groups = n_pages // pg
    m = pg * (PAGE_SIZE // CMP_STRIDE)
    nc = n_pages * (PAGE_SIZE // CMP_STRIDE)
    total = batch * n_groups

    def full(a):
        nd = a.ndim
        return pl.BlockSpec(a.shape, lambda b, g, pt: (0,) * nd)

    grid_spec = pltpu.PrefetchScalarGridSpec(
        num_scalar_prefetch=1,
        grid=(batch, n_groups),
        in_specs=[pl.BlockSpec(memory_space=pl.ANY), full(w1s), full(pet), full(w2p)],
        out_specs=[pl.BlockSpec((1, m, LANES), lambda b, g, pt: (b, g, 0))] * 2,
        scratch_shapes=[pltpu.VMEM((2, pg, 2 * LANES, PAGE_SIZE), F32), pltpu.VMEM((2, pg * PAGE_SIZE, LANES), F32),
                        pltpu.SemaphoreType.DMA((2,)), pltpu.VMEM((4, 8, CMP_HID), F32)],
    )
    return pl.pallas_call(
        functools.partial(_cmp_stream_kernel, page_base=layer * n_phys, n_pages=n_pages, pg=pg,
                          n_groups=n_groups, total=total),
        grid_spec=grid_spec,
        out_shape=[jax.ShapeDtypeStruct((batch, nc, LANES), BF16)] * 2,
        compiler_params=pltpu.CompilerParams(dimension_semantics=("arbitrary", "arbitrary"),
                                             vmem_limit_bytes=VMEM_LIMIT),
        name="cmp_stream_sample",
    )(pt_flat, cache_t, w1s, pet, w2p)


def _cmp_attn_sample_kernel(q_ref, kc_ref, vc_ref, cover_ref, ocmp_ref, idx_ref, bias_ref,
                            *, past, nc, ns, nsp, n_cache_blocks):
    rows = lax.broadcasted_iota(jnp.int32, (32, 1), 0)
    qpos_r = past + (rows & 7)
    qp = past + lax.broadcasted_iota(jnp.int32, (8, 1), 0)
    m_idx = lax.broadcasted_iota(jnp.int32, (1, nc), 1)
    blk = lax.broadcasted_iota(jnp.int32, (1, nsp), 1)
    blk_f = blk.astype(F32)
    lane = lax.broadcasted_iota(jnp.int32, (1, LANES), 1)
    cmask = (m_idx >= 1) & (CMP_STRIDE * (m_idx - 1) + CMP_LEN - 1 <= qpos_r)
    for g in range(N_KV_HEADS):
        p = _masked_softmax(_dot_nt(q_ref[0, g], kc_ref[0]), cmask, -1)
        ocmp_ref[0, g] = _dot(p.astype(BF16), vc_ref[0])
        p4 = p[0:8] + p[8:16] + p[16:24] + p[24:32]
        hi, lo = _split(p4)
        imp = _dot(hi, cover_ref[...]) + _dot(lo, cover_ref[...])
        qblk = qp // SLC_BLOCK
        elig = blk * SLC_BLOCK <= qp
        forced = (blk == 0) | (blk == qblk) | (blk == qblk - 1)
        val = jnp.where(blk < ns, jnp.where(elig, jnp.where(forced, FORCE, imp), -FORCE), REMOVED)
        _, firsts, tops = _top_select(val, blk_f, nsp, -1)
        idx = jnp.zeros((8, LANES), F32)
        bias = jnp.zeros((8, LANES), F32)
        for t in range(N_SELECT):
            ok = (tops[t] > -0.5 * FORCE) & (firsts[t] < float(n_cache_blocks))
            odd = firsts[t] - 2.0 * jnp.floor(firsts[t] * 0.5)
            idx = jnp.where(lane == t, firsts[t], idx)
            for hf in range(2):
                bias = jnp.where(lane == 2 * t + hf, jnp.where(ok & (odd == float(hf)), 0.0, NEG_INF), bias)
        idx_ref[0, g] = idx.astype(jnp.int32)
        bias_ref[0, g] = bias


def _cmp_attn_sample_call(q_hq, kc, vc, cover, *, past, ns, n_cache_blocks):
    batch, nc, _ = kc.shape
    nsp = cover.shape[1]
    blk4 = lambda r: pl.BlockSpec((1, N_KV_HEADS, r, LANES), lambda b: (b, 0, 0, 0))
    return pl.pallas_call(
        functools.partial(_cmp_attn_sample_kernel, past=past, nc=nc, ns=ns, nsp=nsp, n_cache_blocks=n_cache_blocks),
        grid=(batch,),
        in_specs=[blk4(32), pl.BlockSpec((1, nc, LANES), lambda b: (b, 0, 0)),
                  pl.BlockSpec((1, nc, LANES), lambda b: (b, 0, 0)), pl.BlockSpec(cover.shape, lambda b: (0, 0))],
        out_specs=[blk4(32), blk4(8), blk4(8)],
        out_shape=[jax.ShapeDtypeStruct((batch, N_KV_HEADS, 32, LANES), F32),
                   jax.ShapeDtypeStruct((batch, N_KV_HEADS, 8, LANES), jnp.int32),
                   jax.ShapeDtypeStruct((batch, N_KV_HEADS, 8, LANES), F32)],
        compiler_params=pltpu.CompilerParams(dimension_semantics=("arbitrary",), vmem_limit_bytes=VMEM_LIMIT),
        name="cmp_attn_sample",
    )(q_hq, kc, vc, cover)


def _sel_attn_sample_kernel(idx_ref, pt_ref, cache_ref, q_ref, bias_ref, ex_ref, knew_ref, vnew_ref, wint_ref,
                            wnew_ref, wnewt_ref, ocmp_ref, gate_ref, o_ref, wout_ref, buf, sem,
                            *, page_base, n_pages, tq, n_cache_blocks, total):
    b = pl.program_id(0)
    g = pl.program_id(1)
    step = b * N_KV_HEADS + g
    slot = lax.rem(step, 2)
    n_ent = tq * N_SELECT
    blocks_per_page = PAGE_SIZE // SLC_BLOCK

    def page_copy(page, slt, e):
        return pltpu.make_async_copy(cache_ref.at[page, pl.ds(2 * LANES, 2 * LANES), :], buf.at[slt, e], sem.at[slt])

    def issue(stp, slt):
        bb = lax.div(stp, N_KV_HEADS)

        def body(e, _):
            blk = jnp.minimum(idx_ref[stp * n_ent + e], n_cache_blocks - 1)
            page = pt_ref[bb * n_pages + lax.div(blk, blocks_per_page)]
            page_copy(page_base + page, slt, e).start()
            return 0
        lax.fori_loop(0, n_ent, body, 0)

    @pl.when(step == 0)
    def _():
        issue(step, slot)

    @pl.when(step + 1 < total)
    def _():
        issue(step + 1, 1 - slot)

    def wait_body(e, _):
        page_copy(0, slot, e).wait()
        return 0
    lax.fori_loop(0, n_ent, wait_body, 0)

    lane = lax.broadcasted_iota(jnp.int32, (1, LANES), 1)
    col8 = lax.broadcasted_iota(jnp.int32, (1, 8), 1)
    wcol = lax.broadcasted_iota(jnp.int32, (1, WINDOW), 1)
    k_wt = wint_ref[0, pl.ds(0, LANES), :].astype(BF16)
    v_wt = wint_ref[0, pl.ds(LANES, LANES), :].astype(BF16)
    wnew = wnew_ref[0]
    k_wn = wnew[:, 0:LANES].astype(BF16)
    v_wn = wnew[:, LANES:2 * LANES].astype(BF16)
    bias_all = _dot(bias_ref[0, 0].astype(BF16), ex_ref[...])
    keep = (lane >= g * HEAD_DIM) & (lane < (g + 1) * HEAD_DIM)
    for qi in range(tq):
        q4 = q_ref[0, 0, qi]
        new_ok = (col8 <= qi) & (col8 < tq)
        k_t = jnp.concatenate([buf[slot, qi * N_SELECT + k, pl.ds(0, LANES), :] for k in range(N_SELECT)], axis=1)
        v_t = jnp.concatenate([buf[slot, qi * N_SELECT + k, pl.ds(LANES, LANES), :] for k in range(N_SELECT)], axis=1)
        s = _dot(q4, k_t.astype(BF16)) + bias_all[qi:qi + 1, :]
        s_n = jnp.where(new_ok, _dot_nt(q4, knew_ref[0]), NEG_INF)
        mx = jnp.maximum(jnp.max(s, axis=-1, keepdims=True), jnp.max(s_n, axis=-1, keepdims=True))
        pe = jnp.exp2(s - mx)
        pn = jnp.exp2(s_n - mx)
        l = jnp.sum(pe, axis=-1, keepdims=True) + jnp.sum(pn, axis=-1, keepdims=True)
        o_slc = (_dot_nt(pe.astype(BF16), v_t.astype(BF16)) + _dot(pn.astype(BF16), vnew_ref[0])) * (1.0 / l)
        sw = jnp.where(wcol >= qi, _dot(q4, k_wt), NEG_INF)
        sw_n = jnp.where(new_ok, _dot_nt(q4, k_wn), NEG_INF)
        mw = jnp.maximum(jnp.max(sw, axis=-1, keepdims=True), jnp.max(sw_n, axis=-1, keepdims=True))
        pw = jnp.exp2(sw - mw)
        pwn = jnp.exp2(sw_n - mw)
        lw = jnp.sum(pw, axis=-1, keepdims=True) + jnp.sum(pwn, axis=-1, keepdims=True)
        o_win = (_dot_nt(pw.astype(BF16), v_wt) + _dot(pwn.astype(BF16), v_wn)) * (1.0 / lw)
        gt = gate_ref[0, 0, qi]
        o = gt[:, 0:1] * ocmp_ref[0, 0, qi] + gt[:, 1:2] * o_slc + gt[:, 2:3] * o_win
        o_ref[0, 0, qi] = jnp.where(keep, o, 0.0)

    @pl.when(g == 0)
    def _():
        shifted = pltpu.roll(wint_ref[0], WINDOW - tq, 1)
        wout_ref[0, :, pl.ds(0, WINDOW - LANES)] = shifted[:, 0:WINDOW - LANES]
        wout_ref[0, :, pl.ds(WINDOW - LANES, LANES)] = jnp.where(lane >= LANES - tq, wnewt_ref[0],
                                                                  shifted[:, WINDOW - LANES:WINDOW])


def _sel_attn_sample_call(idx_flat, pt_flat, cache_t, q_qh, bias, expand, knew, vnew, win_t, wnew, wnew_t, ocmp_qh,
                          gate_qh, *, layer, n_phys, n_pages, tq, n_cache_blocks):
    batch = q_qh.shape[0]
    n_ent = tq * N_SELECT
    b5 = lambda: pl.BlockSpec((1, 1, tq, 8, LANES), lambda b, g, i, p: (b, g, 0, 0, 0))
    b3 = lambda r, c: pl.BlockSpec((1, r, c), lambda b, g, i, p: (b, 0, 0))
    grid_spec = pltpu.PrefetchScalarGridSpec(
        num_scalar_prefetch=2,
        grid=(batch, N_KV_HEADS),
        in_specs=[pl.BlockSpec(memory_space=pl.ANY), b5(),
                  pl.BlockSpec((1, 1, 8, LANES), lambda b, g, i, p: (b, g, 0, 0)),
                  pl.BlockSpec(expand.shape, lambda b, g, i, p: (0, 0)),
                  b3(8, LANES), b3(8, LANES),
                  pl.BlockSpec((1, 2 * LANES, WINDOW), lambda b, g, i, p: (layer * batch + b, 0, 0)),
                  b3(8, 2 * LANES), b3(2 * LANES, LANES),
                  b5(), b5()],
        out_specs=[b5(), b3(2 * LANES, WINDOW)],
        scratch_shapes=[pltpu.VMEM((2, n_ent, 2 * LANES, PAGE_SIZE), F32), pltpu.SemaphoreType.DMA((2,))],
    )
    return pl.pallas_call(
        functools.partial(_sel_attn_sample_kernel, page_base=layer * n_phys, n_pages=n_pages, tq=tq,
                          n_cache_blocks=n_cache_blocks, total=batch * N_KV_HEADS),
        grid_spec=grid_spec,
        out_shape=[jax.ShapeDtypeStruct((batch, N_KV_HEADS, tq, 8, LANES), F32),
                   jax.ShapeDtypeStruct((batch, 2 * LANES, WINDOW), F32)],
        compiler_params=pltpu.CompilerParams(dimension_semantics=("arbitrary", "arbitrary"),
                                             vmem_limit_bytes=VMEM_LIMIT),
        name="sel_attn_sample",
    )(idx_flat, pt_flat, cache_t, q_qh, bias, expand, knew, vnew, win_t, wnew, wnew_t, ocmp_qh, gate_qh)


def _prep_w_in(w):
    d = w.shape[0]
    wq = (w[:, 768:1280] * (HEAD_DIM ** -0.5 * LOG2E)).reshape(d, N_HEADS, HEAD_DIM)
    z = jnp.zeros_like(wq)
    grp = (jnp.arange(N_HEADS) // GQA)[None, :, None]
    wq = jnp.concatenate([jnp.where(grp == 0, wq, z), jnp.where(grp == 1, wq, z)], axis=-1).reshape(d, N_HEADS * LANES)
    gate = jnp.pad(w[:, 2048:2072], ((0, 0), (0, LANES - N_HEADS * N_BRANCH)))
    return jnp.concatenate([w[:, :768], wq, w[:, 1280:2048], gate, w[:, 2072:]], axis=1).astype(BF16)


def _prep_w1(w1):
    w = w1.reshape(2, 2, CMP_STRIDE, HEAD_DIM, CMP_HID)
    return jnp.transpose(w, (0, 2, 3, 1, 4)).reshape(2, CMP_STRIDE * HEAD_DIM, 2 * CMP_HID)


def _prep_w1_grouped(w1s):
    w = w1s.reshape(2, CMP_STRIDE, 1, HEAD_DIM, 1, 2 * CMP_HID)
    eye = jnp.eye(N_KV_HEADS, dtype=w.dtype).reshape(1, 1, N_KV_HEADS, 1, N_KV_HEADS, 1)
    return (w * eye).reshape(2, CMP_STRIDE * N_KV_HEADS * HEAD_DIM, N_KV_HEADS * 2 * CMP_HID)


def _prep_w2(w2):
    z = jnp.zeros_like(w2)
    return jnp.stack([jnp.concatenate([w2, z], axis=-1), jnp.concatenate([z, w2], axis=-1)], axis=1).astype(BF16)


def _cover_matrix(nc, ns_real, ns_pad):
    m = np.arange(nc)[:, None]
    b = np.arange(ns_pad)[None, :]
    return ((m >= 4 * b) & (m <= 4 * b + 4) & (m >= 1) & (b < ns_real)).astype(np.float32)


def _block_bias_matrix():
    k = np.arange(KEY_TILE)[:, None]
    b = np.arange(LANES)[None, :]
    return jnp.asarray(np.where(k // SLC_BLOCK == b, NEG_INF, 0.0).astype(np.float32), dtype=BF16)


def _expand_matrix():
    r = np.arange(LANES)[:, None]
    c = np.arange(N_SELECT * PAGE_SIZE)[None, :]
    return jnp.asarray((c // SLC_BLOCK == r).astype(np.float32), dtype=BF16)


def kernel(x_prompt, x_sample, cache_nsa_kv, state_win_kv, state_conv, state_ffn_conv, page_table, c_prompt, c_sample, w_ada, b_ada, w_in, conv_a_w, conv_a_b, cmp_pe, cmp_w1, cmp_w2, sgu_ln_g, sgu_ln_b, sgu_w, sgu_b, w_o, ln_g, ln_b, w_ffn_up, conv_f_w, conv_f_b, w_ffn_down):
    depth = w_in.shape[0]
    _, t, d_model = x_prompt.shape
    nb, tq, _ = x_sample.shape
    n_phys = cache_nsa_kv.shape[1]
    n_pages = page_table.shape[1]
    past = n_pages * PAGE_SIZE
    d_ff = w_ffn_down.shape[1]
    alpha = (2 * depth) ** 0.25
    rs = nb * tq
    kvw = 4 * N_KV_HEADS * HEAD_DIM
    assert x_prompt.shape[0] == 1 and c_prompt.shape[0] == 1
    assert d_model == 1024 and t % KEY_TILE == 0 and t >= WINDOW + Q_BLOCK
    assert tq == 4 and rs == GMLP_CHUNK and past % KEY_TILE == 0 and past >= WINDOW
    assert state_win_kv.shape[2] == WINDOW

    rc = -(-(1 + nb) // 8) * 8
    c_all = jnp.pad(jnp.concatenate([c_prompt, c_sample], axis=0), ((0, rc - 1 - nb), (0, 0)))
    mods = _ada_call(c_all, w_ada, b_ada)

    nc_p, ns_p = t // CMP_STRIDE, t // SLC_BLOCK
    covert_p = jnp.asarray(_cover_matrix(nc_p, ns_p, ns_p).T, dtype=BF16)
    nc_s = (past + tq) // CMP_STRIDE
    ns_s = -(-(past + tq) // SLC_BLOCK)
    ns_s_pad = -(-ns_s // LANES) * LANES
    n_cache_blocks = past // SLC_BLOCK
    cover_s = jnp.asarray(_cover_matrix(nc_s, ns_s, ns_s_pad), dtype=BF16)
    ebias = _block_bias_matrix()
    expand = _expand_matrix()
    pt_flat = page_table.reshape(-1)
    cache_t = jnp.transpose(cache_nsa_kv.reshape(depth * n_phys, PAGE_SIZE, kvw), (0, 2, 1))
    win_t_all = jnp.transpose(state_win_kv.reshape(depth * nb, WINDOW, 2 * LANES), (0, 2, 1))

    xp = x_prompt[0]
    xs = jnp.transpose(x_sample, (1, 0, 2)).reshape(rs, d_model)
    tril_full = jnp.tril(jnp.ones((GMLP_CHUNK, GMLP_CHUNK), F32))
    tril_tq = jnp.tril(jnp.ones((tq, tq), F32))

    ps, ss = [], []
    for l in range(depth):
        w_all = _prep_w_in(w_in[l])
        wo_p = w_o[l].astype(BF16)
        wup = w_ffn_up[l].astype(BF16)
        wdn = w_ffn_down[l].astype(BF16)
        w1s = _prep_w1(cmp_w1[l])
        w1g = _prep_w1_grouped(w1s).astype(BF16)
        w1s = w1s.astype(BF16)
        w2p = _prep_w2(cmp_w2[l])
        pe_flat = jnp.broadcast_to(cmp_pe[l].reshape(2, 1, CMP_LEN * HEAD_DIM), (2, 8, CMP_LEN * HEAD_DIM))
        pet = _pe_term_call(pe_flat, cmp_w1[l].reshape(2, CMP_LEN * HEAD_DIM, CMP_HID))
        cw, cb = conv_a_w[l], conv_a_b[l].reshape(1, -1)
        cfw, cfb = conv_f_w[l], conv_f_b[l].reshape(1, -1)
        lng, lnb = sgu_ln_g[l].reshape(1, -1), sgu_ln_b[l].reshape(1, -1)

        def mod_rows(r0, r1, rep):
            parts = [mods[l, r0:r1, k * d_model:(k + 1) * d_model] for k in range(6)]
            return [jnp.tile(p_, (rep, 1)) if rep > 1 else p_ for p_ in parts]

        sh1, sc1, g1, sh2, sc2, g2 = mod_rows(0, 1, 1)
        wm_p = (sgu_w[l] * tril_full).astype(BF16)
        sb_p = jnp.repeat(sgu_b[l].T, HEAD_DIM, axis=1)
        (oa, oc, cst, qt, kvp, kvt, kcr, vcr, ksl, kwn, vslt, vwnt, gatet) = _in_proj_call(
            xp, sc1, sh1, w_all, cw, cb, jnp.zeros((2, 256), F32), lng, lnb, wm_p, sb_p,
            shift=1, tm=512, transposed=True, name="in_proj_prompt")
        kc, vct = _compress_prompt_call(kcr.reshape(nc_p, CMP_STRIDE * LANES), vcr.reshape(nc_p, CMP_STRIDE * LANES),
                                        w1g, pet, w2p)
        ob = _attn_prompt_call(qt, gatet, kc, vct, covert_p, ksl, vslt, kwn, vwnt, ebias)
        xp, fst = _out_ffn_call(xp, oa, ob, oc, g1, sc2, sh2, g2, wo_p, ln_g[l], ln_b[l], wup, cfw, cfb,
                                jnp.zeros((2, d_ff), F32), wdn, shift=1, tm=256, alpha=alpha, name="out_ffn_prompt")
        paged = kvp.reshape(t // PAGE_SIZE, 4, N_KV_HEADS, HEAD_DIM, PAGE_SIZE)
        winr = kvt[:, t - WINDOW:].reshape(2, N_KV_HEADS, HEAD_DIM, WINDOW)
        ps.append((jnp.transpose(paged, (0, 4, 1, 2, 3))[None], jnp.transpose(winr, (3, 0, 1, 2))[None],
                   cst[None], fst[None]))

        sh1, sc1, g1, sh2, sc2, g2 = mod_rows(1, 1 + nb, tq)
        eye_b = jnp.eye(nb, dtype=F32)
        wm_s = jax.vmap(lambda w: jnp.kron(w[:tq, :tq] * tril_tq, eye_b))(sgu_w[l]).astype(BF16)
        sb_s = jnp.repeat(jnp.repeat(sgu_b[l][:, :tq].T, nb, axis=0), HEAD_DIM, axis=1)
        cpast = jnp.transpose(state_conv[l], (1, 0, 2)).reshape(2 * nb, -1)
        (oa, oc, cst, q, kvf, ksl, vsl, gate, vrow) = _in_proj_call(
            xs, sc1, sh1, w_all, cw, cb, cpast, lng, lnb, wm_s, sb_s, shift=nb, tm=rs, transposed=False,
            name="in_proj_sample")
        kc, vc = _cmp_stream_call(pt_flat, cache_t, w1s, pet, w2p, layer=l, n_phys=n_phys, batch=nb,
                                  n_pages=n_pages, pg=min(32, n_pages))

        def by_batch(a):
            return jnp.transpose(a.reshape(tq, nb, -1), (1, 0, 2))

        qb5 = by_batch(q).reshape(nb, tq, N_KV_HEADS, GQA, LANES)
        q_hq = jnp.pad(jnp.transpose(qb5, (0, 2, 3, 1, 4)), ((0, 0), (0, 0), (0, 0), (0, 8 - tq), (0, 0)))
        q_hq = q_hq.reshape(nb, N_KV_HEADS, 32, LANES)
        q_qh = jnp.pad(jnp.transpose(qb5, (0, 2, 1, 3, 4)), ((0, 0), (0, 0), (0, 0), (0, 8 - GQA), (0, 0)))
        ocmp, idx, bias = _cmp_attn_sample_call(q_hq, kc, vc, cover_s, past=past, ns=ns_s,
                                                n_cache_blocks=n_cache_blocks)
        ocmp_qh = jnp.transpose(ocmp.reshape(nb, N_KV_HEADS, GQA, 8, LANES)[:, :, :, :tq], (0, 1, 3, 2, 4))
        ocmp_qh = jnp.pad(ocmp_qh, ((0, 0), (0, 0), (0, 0), (0, 8 - GQA), (0, 0)))
        g5 = by_batch(gate)[:, :, :N_HEADS * N_BRANCH].reshape(nb, tq, N_KV_HEADS, GQA, N_BRANCH)
        gate_qh = jnp.pad(jnp.transpose(g5, (0, 2, 1, 3, 4)),
                          ((0, 0), (0, 0), (0, 0), (0, 8 - GQA), (0, LANES - N_BRANCH)))
        pad8 = lambda a: jnp.pad(by_batch(a), ((0, 0), (0, 8 - tq), (0, 0)))
        wnew_rows = by_batch(kvf[:, kvw:])
        wnew_t = jnp.pad(jnp.transpose(wnew_rows, (0, 2, 1)), ((0, 0), (0, 0), (LANES - tq, 0)))
        o5, wout_t = _sel_attn_sample_call(
            idx[:, :, :tq, :N_SELECT].reshape(-1), pt_flat, cache_t, q_qh, bias, expand, pad8(ksl), pad8(vsl),
            win_t_all, jnp.pad(wnew_rows, ((0, 0), (0, 8 - tq), (0, 0))), wnew_t, ocmp_qh, gate_qh,
            layer=l, n_phys=n_phys, n_pages=n_pages, tq=tq, n_cache_blocks=n_cache_blocks)
        o5 = o5[:, :, :, :GQA, :HEAD_DIM] + o5[:, :, :, :GQA, HEAD_DIM:]
        ob = jnp.transpose(o5, (2, 0, 1, 3, 4)).reshape(rs, N_HEADS * HEAD_DIM).astype(BF16)
        fpast = jnp.transpose(state_ffn_conv[l], (1, 0, 2)).reshape(2 * nb, -1)
        xs, fst = _out_ffn_call(xs, oa, ob, oc, g1, sc2, sh2, g2, wo_p, ln_g[l], ln_b[l], wup, cfw, cfb, fpast, wdn,
                                shift=nb, tm=rs, alpha=alpha, name="out_ffn_sample")
        ss.append((by_batch(kvf[:, :kvw]).reshape(nb, tq, 4, N_KV_HEADS, HEAD_DIM),
                   jnp.transpose(wout_t, (0, 2, 1)).reshape(nb, WINDOW, 2, N_KV_HEADS, HEAD_DIM),
                   jnp.transpose(cst.reshape(2, nb, -1), (1, 0, 2)),
                   jnp.transpose(fst.reshape(2, nb, -1), (1, 0, 2)),
                   by_batch(vrow)))

    ys = jnp.transpose(xs.reshape(tq, nb, d_model), (1, 0, 2))
    return (xp[None], ys,
            jnp.stack([s[0] for s in ps]), jnp.stack([s[1] for s in ps]),
            jnp.stack([s[2] for s in ps]), jnp.stack([s[3] for s in ps]),
            jnp.stack([s[0] for s in ss]), jnp.stack([s[1] for s in ss]),
            jnp.stack([s[2] for s in ss]), jnp.stack([s[3] for s in ss]),
            jnp.stack([s[4] for s in ss]))
```

```python
import functools
import math

import numpy as np
import jax
import jax.numpy as jnp
from jax import lax
from jax.experimental import pallas as pl
from jax.experimental.pallas import tpu as pltpu

F32 = jnp.float32
BF16 = jnp.bfloat16

HEAD_DIM = 64
N_HEADS = 8
N_KV_HEADS = 2
GQA = N_HEADS // N_KV_HEADS
N_BRANCH = 3
CONV_K = 3
CMP_LEN = 32
CMP_STRIDE = 16
CMP_HID = 128
SLC_BLOCK = 64
N_SELECT = 16
WINDOW = 512
Q_BLOCK = 128
PAGE_SIZE = 128
GMLP_CHUNK = 128
GMLP_GROUPS = 4
LN_EPS = 1e-5
NEG_INF = -1e30
FORCE = 1e4
REMOVED = -3e38
LOG2E = 1.4426950408889634

LANES = 128
KEY_TILE = 512
BLOCKS_PER_TILE = KEY_TILE // SLC_BLOCK
NS_ROWS = 16
TILE_UNROLL = 4
CAUSAL_VARIANTS = 4
VMEM_LIMIT = 56 * 1024 * 1024


def _dot(a, b):
    return jnp.dot(a, b, preferred_element_type=F32)


def _dot_nt(a, b):
    return lax.dot_general(a, b, (((1,), (1,)), ((), ())), preferred_element_type=F32)


def _split(a):
    hi = a.astype(BF16)
    lo = (a - hi.astype(F32)).astype(BF16)
    return hi, lo


def _dot3(a, b):
    ah, al = _split(a)
    bh, bl = _split(b)
    return _dot(ah, bh) + _dot(ah, bl) + _dot(al, bh)


def _sigmoid(x):
    return 1.0 / (1.0 + jnp.exp(-x))


def _gelu(x):
    c = math.sqrt(2.0 / math.pi)
    return 0.5 * x * (1.0 + jnp.tanh(c * (x + 0.044715 * (x * x * x))))


def _layer_norm(x, g, b):
    mu = jnp.mean(x, axis=-1, keepdims=True)
    xc = x - mu
    var = jnp.mean(xc * xc, axis=-1, keepdims=True)
    return xc * lax.rsqrt(var + LN_EPS) * g + b


def _masked_softmax(s, mask, axis):
    sm = jnp.where(mask, s, NEG_INF)
    mx = jnp.max(sm, axis=axis, keepdims=True)
    e = jnp.where(mask, jnp.exp2(s - mx), 0.0)
    l = jnp.sum(e, axis=axis, keepdims=True)
    return e * (1.0 / jnp.where(l > 0.0, l, 1.0))


def _softmax_rows(sm, col_valid):
    e = jnp.exp2(sm - jnp.max(sm, axis=0, keepdims=True))
    inv = 1.0 / jnp.sum(e, axis=0, keepdims=True)
    if col_valid is not None:
        inv = jnp.where(col_valid, inv, 0.0)
    return e * inv


def _top_select(val, blk_f, n_blk, axis):
    sel = jnp.zeros_like(val)
    firsts, tops = [], []
    for _ in range(N_SELECT):
        mx = jnp.max(val, axis=axis, keepdims=True)
        first = jnp.min(jnp.where(val == mx, blk_f, float(n_blk)), axis=axis, keepdims=True)
        hit = blk_f == first
        sel = jnp.where(hit, 1.0, sel)
        val = jnp.where(hit, REMOVED, val)
        firsts.append(first)
        tops.append(mx)
    return sel, firsts, tops


def _shifted_conv(src_ref, x, w_ref, b_ref, pad, shift, rows):
    x2 = src_ref[pl.ds(pad - 2 * shift, rows), :]
    x1 = src_ref[pl.ds(pad - shift, rows), :]
    return w_ref[0:1, :] * x2 + w_ref[1:2, :] * x1 + w_ref[2:3, :] * x + b_ref[...]


def _ada_kernel(c_ref, w_ref, b_ref, o_ref):
    c = c_ref[...]
    o_ref[0] = _dot3(c * _sigmoid(c), w_ref[0]) + b_ref[0]


def _ada_call(c_all, w_ada, b_ada):
    depth, d_model, n_mod = w_ada.shape
    rc = c_all.shape[0]
    tn = 1024
    return pl.pallas_call(
        _ada_kernel,
        grid=(depth, n_mod // tn),
        in_specs=[pl.BlockSpec((rc, d_model), lambda l, n: (0, 0)),
                  pl.BlockSpec((1, d_model, tn), lambda l, n: (l, 0, n)),
                  pl.BlockSpec((1, 1, tn), lambda l, n: (l, 0, n))],
        out_specs=pl.BlockSpec((1, rc, tn), lambda l, n: (l, 0, n)),
        out_shape=jax.ShapeDtypeStruct((depth, rc, n_mod), F32),
        compiler_params=pltpu.CompilerParams(dimension_semantics=("arbitrary", "arbitrary"),
                                             vmem_limit_bytes=VMEM_LIMIT),
        name="ada_mod",
    )(c_all, w_ada, b_ada.reshape(depth, 1, n_mod))


_C_AB, _C_AC, _C_AH = 0, 256, 512
_C_Q = 768
_C_KV = 1792
_C_GATE = 2560
_C_GU = 2688
_C_GV = 2944
_N_COL = 3200
_N_SLOT = 6


def _in_proj_kernel(x_ref, sc_ref, sh_ref, w_ref, cw_ref, cb_ref, cpast_ref, lng_ref, lnb_ref, wm_ref, sb_ref,
                    *rest, shift, tm, pad, transposed):
    if transposed:
        (oa_ref, oc_ref, cstate_ref, qt_ref, kvp_ref, kvt_ref, kcr_ref, vcr_ref, ksl_ref, kwn_ref, vslt_ref, vwnt_ref,
         gatet_ref, zs_ref) = rest
    else:
        (oa_ref, oc_ref, cstate_ref, q_ref, kvf_ref, ksl_ref, vsl_ref, gate_ref, vrow_ref, zs_ref) = rest
    i = pl.program_id(0)

    @pl.when(i == 0)
    def _():
        zs_ref[pl.ds(pad - 2 * shift, 2 * shift), :] = cpast_ref[...]

    h = (x_ref[...] * (1.0 + sc_ref[...]) + sh_ref[...]).astype(BF16)
    p = _dot(h, w_ref[...])

    z = p[:, _C_AC:_C_AC + 256] * p[:, _C_AH:_C_AH + 256]
    zs_ref[pl.ds(pad, tm), :] = z
    y = _shifted_conv(zs_ref, z, cw_ref, cb_ref, pad, shift, tm)
    oa_ref[...] = (p[:, _C_AB:_C_AB + 256] * y).astype(BF16)
    tail = zs_ref[pl.ds(pad + tm - 2 * shift, 2 * shift), :]
    cstate_ref[...] = tail
    zs_ref[pl.ds(pad - 2 * shift, 2 * shift), :] = tail

    kv = [p[:, _C_KV + k * LANES:_C_KV + (k + 1) * LANES] for k in range(_N_SLOT)]
    gate = _sigmoid(p[:, _C_GATE:_C_GATE + LANES])
    if transposed:
        for hh in range(N_HEADS):
            qt_ref[pl.ds(hh * LANES, LANES), :] = p[:, _C_Q + hh * LANES:_C_Q + (hh + 1) * LANES].T.astype(BF16)
        kvt = [a.T for a in kv]
        for k in range(4):
            for pg in range(tm // PAGE_SIZE):
                kvp_ref[pg, pl.ds(k * LANES, LANES), :] = kvt[k][:, pg * PAGE_SIZE:(pg + 1) * PAGE_SIZE]
        for k in range(4, _N_SLOT):
            kvt_ref[pl.ds((k - 4) * LANES, LANES), :] = kvt[k]
        kcr_ref[...] = kv[0].astype(BF16)
        vcr_ref[...] = kv[1].astype(BF16)
        ksl_ref[...] = kv[2].astype(BF16)
        kwn_ref[...] = kv[4].astype(BF16)
        vslt_ref[...] = kvt[3].astype(BF16)
        vwnt_ref[...] = kvt[5].astype(BF16)
        gatet_ref[...] = gate.T
    else:
        q_ref[...] = p[:, _C_Q:_C_Q + 1024].astype(BF16)
        kvf_ref[...] = p[:, _C_KV:_C_KV + _N_SLOT * LANES]
        ksl_ref[...] = kv[2].astype(BF16)
        vsl_ref[...] = kv[3].astype(BF16)
        gate_ref[...] = gate

    u = _gelu(p[:, _C_GU:_C_GU + 256])
    v = _layer_norm(_gelu(p[:, _C_GV:_C_GV + 256]), lng_ref[...], lnb_ref[...])
    if not transposed:
        vrow_ref[...] = v
    lane = lax.broadcasted_iota(jnp.int32, (1, 256), 1)
    for c in range(tm // GMLP_CHUNK):
        vc = v[c * GMLP_CHUNK:(c + 1) * GMLP_CHUNK]
        mixed = sb_ref[...]
        for g in range(GMLP_GROUPS):
            vg = jnp.where((lane >= g * HEAD_DIM) & (lane < (g + 1) * HEAD_DIM), vc, 0.0).astype(BF16)
            mixed = mixed + _dot(wm_ref[g], vg)
        oc_ref[pl.ds(c * GMLP_CHUNK, GMLP_CHUNK), :] = (u[c * GMLP_CHUNK:(c + 1) * GMLP_CHUNK] * mixed).astype(BF16)


def _in_proj_call(x, sc, sh, w_all, cw, cb, cpast, lng, lnb, wm, sb, *, shift, tm, transposed, name):
    rows, d_model = x.shape
    pad = max(8, 2 * shift)
    mr = sc.shape[0]
    mod_spec = (pl.BlockSpec((1, d_model), lambda i: (0, 0)) if mr == 1
                else pl.BlockSpec((tm, d_model), lambda i: (i, 0)))

    def row_spec(n):
        return pl.BlockSpec((tm, n), lambda i: (i, 0))

    def col_spec(n):
        return pl.BlockSpec((n, tm), lambda i: (0, i))

    def full(a):
        nd = a.ndim
        return pl.BlockSpec(a.shape, lambda i: (0,) * nd)

    sds = jax.ShapeDtypeStruct
    out_shape = [sds((rows, 256), BF16), sds((rows, 256), BF16), sds((2 * shift, 256), F32)]
    out_specs = [row_spec(256), row_spec(256), pl.BlockSpec((2 * shift, 256), lambda i: (0, 0))]
    if transposed:
        out_shape += [sds((N_HEADS * LANES, rows), BF16), sds((rows // PAGE_SIZE, 4 * LANES, PAGE_SIZE), F32),
                      sds((2 * LANES, rows), F32)]
        out_specs += [col_spec(N_HEADS * LANES),
                      pl.BlockSpec((tm // PAGE_SIZE, 4 * LANES, PAGE_SIZE), lambda i: (i, 0, 0)), col_spec(2 * LANES)]
        out_shape += [sds((rows, LANES), BF16)] * 4 + [sds((LANES, rows), BF16)] * 2 + [sds((LANES, rows), F32)]
        out_specs += [row_spec(LANES)] * 4 + [col_spec(LANES)] * 3
    else:
        out_shape += [sds((rows, 1024), BF16), sds((rows, _N_SLOT * LANES), F32), sds((rows, LANES), BF16),
                      sds((rows, LANES), BF16), sds((rows, LANES), F32), sds((rows, 256), F32)]
        out_specs += [row_spec(1024), row_spec(_N_SLOT * LANES), row_spec(LANES), row_spec(LANES), row_spec(LANES),
                      row_spec(256)]
    return pl.pallas_call(
        functools.partial(_in_proj_kernel, shift=shift, tm=tm, pad=pad, transposed=transposed),
        grid=(rows // tm,),
        in_specs=[row_spec(d_model), mod_spec, mod_spec, full(w_all), full(cw), full(cb), full(cpast),
                  full(lng), full(lnb), full(wm), full(sb)],
        out_specs=out_specs,
        out_shape=out_shape,
        scratch_shapes=[pltpu.VMEM((pad + tm, 256), F32)],
        compiler_params=pltpu.CompilerParams(dimension_semantics=("arbitrary",), vmem_limit_bytes=VMEM_LIMIT),
        name=name,
    )(x, sc, sh, w_all, cw, cb, cpast, lng, lnb, wm, sb)


def _pe_term_kernel(pe_ref, w1_ref, o_ref):
    for s in range(2):
        o_ref[s] = _dot3(pe_ref[s], w1_ref[s])


def _pe_term_call(pe_flat, w1_flat):
    return pl.pallas_call(
        _pe_term_kernel,
        out_shape=jax.ShapeDtypeStruct((2, 8, CMP_HID), F32),
        compiler_params=pltpu.CompilerParams(vmem_limit_bytes=VMEM_LIMIT),
        name="cmp_pe_term",
    )(pe_flat, w1_flat)


def _compress_prompt_kernel(kx_ref, vx_ref, w1_ref, pet_ref, w2_ref, kc_ref, vct_ref, sh_ref, *, nc):
    sh_ref[pl.ds(0, 8), :] = jnp.zeros((8, CMP_HID), F32)
    for s, x_ref in enumerate((kx_ref, vx_ref)):
        parts = _dot(x_ref[...], w1_ref[s])
        acc = jnp.zeros((nc, LANES), F32)
        for g in range(N_KV_HEADS):
            p0 = parts[:, g * 256:g * 256 + CMP_HID]
            p1 = parts[:, g * 256 + CMP_HID:(g + 1) * 256]
            sh_ref[pl.ds(8, nc), :] = p0
            p0s = sh_ref[pl.ds(7, nc), :]
            hid = _gelu(p0s + p1 + pet_ref[s][0:1, :])
            acc = acc + _dot(hid.astype(BF16), w2_ref[s, g])
        if s == 0:
            kc_ref[...] = acc.astype(BF16)
        else:
            vct_ref[...] = acc.T.astype(BF16)


def _compress_prompt_call(kx, vx, w1p, pet, w2p):
    nc = kx.shape[0]
    return pl.pallas_call(
        functools.partial(_compress_prompt_kernel, nc=nc),
        out_shape=[jax.ShapeDtypeStruct((nc, LANES), BF16), jax.ShapeDtypeStruct((LANES, nc), BF16)],
        scratch_shapes=[pltpu.VMEM((nc + 8, CMP_HID), F32)],
        compiler_params=pltpu.CompilerParams(vmem_limit_bytes=VMEM_LIMIT),
        name="compress_prompt",
    )(kx, vx, w1p, pet, w2p)


def _attn_prompt_kernel(qt_ref, gatet_ref, kc_ref, vct_ref, covert_ref, ksl_ref, vslt_ref, kwn_ref, vwnt_ref, eb_ref,
                        o_ref, ns_ref, s_ref, m_ref, ocmp_ref, *, nc, ns):
    qb = pl.program_id(0)
    s0 = qb * Q_BLOCK
    ncol = GQA * Q_BLOCK
    pair = 2 * Q_BLOCK
    col = lax.broadcasted_iota(jnp.int32, (1, ncol), 1)
    qpos_c = s0 + (col & (Q_BLOCK - 1))
    qp = s0 + lax.broadcasted_iota(jnp.int32, (1, Q_BLOCK), 1)
    m_idx = lax.broadcasted_iota(jnp.int32, (nc, 1), 0)
    blk = lax.broadcasted_iota(jnp.int32, (ns, 1), 0)
    blk_f = blk.astype(F32)
    key_row = lax.broadcasted_iota(jnp.int32, (KEY_TILE, 1), 0)
    n_tiles = s0 // KEY_TILE + 1

    ns_ref[pl.ds(ns, 8), :] = jnp.zeros((8, Q_BLOCK), F32)
    rhs_zero = jnp.zeros((LANES - NS_ROWS, ncol), BF16)

    for g in (pl.program_id(1),):
        rq = jnp.concatenate([qt_ref[pl.ds(h * LANES, LANES), :] for h in range(GQA)], axis=1)

        vrows = pl.ds(pl.multiple_of(g * HEAD_DIM, HEAD_DIM), HEAD_DIM)

        def compressed_and_select(frac):
            nr, nsr = nc * frac // CAUSAL_VARIANTS, ns * frac // CAUSAL_VARIANTS

            def run():
                m_i = m_idx[0:nr]
                last_pos = jnp.where(m_i >= 1, CMP_STRIDE * (m_i - 1) + CMP_LEN - 1, 2 ** 30)
                sc = jnp.where(last_pos <= qpos_c, _dot(kc_ref[pl.ds(0, nr), :], rq), NEG_INF)
                p = _softmax_rows(sc, qpos_c >= CMP_LEN - 1)
                o_c = _dot(vct_ref[vrows, pl.ds(0, nr)], p.astype(BF16))
                p4 = (p[:, 0:Q_BLOCK] + p[:, Q_BLOCK:2 * Q_BLOCK] + p[:, 2 * Q_BLOCK:3 * Q_BLOCK]
                      + p[:, 3 * Q_BLOCK:4 * Q_BLOCK])
                hi, lo = _split(p4)
                cov = covert_ref[pl.ds(0, nsr), pl.ds(0, nr)]
                imp = _dot(cov, hi) + _dot(cov, lo)
                b_i, b_f = blk[0:nsr], blk_f[0:nsr]
                qblk = qp // SLC_BLOCK
                elig = b_i * SLC_BLOCK <= qp
                forced = (b_i == 0) | (b_i == qblk) | (b_i == qblk - 1)
                val = jnp.where(elig, jnp.where(forced, REMOVED, imp), -FORCE)
                for _ in range(N_SELECT - 3):
                    mx = jnp.max(val, axis=0, keepdims=True)
                    first = jnp.min(jnp.where(val == mx, b_f, float(ns)), axis=0, keepdims=True)
                    val = jnp.where(b_f == first, REMOVED, val)
                ns_ref[pl.ds(0, nsr), :] = jnp.where(elig & (val == REMOVED), 0.0, 1.0)
                if nsr < ns:
                    ns_ref[pl.ds(nsr, ns - nsr), :] = jnp.ones((ns - nsr, Q_BLOCK), F32)
                ocmp_ref[...] = o_c
            return run

        variant = (qb * CAUSAL_VARIANTS) // (nc * CMP_STRIDE // Q_BLOCK)
        for f in range(CAUSAL_VARIANTS):
            pl.when(variant == f)(compressed_and_select(f + 1))
        o_cmp = ocmp_ref[...]

        def tile_scores(kt):
            k0 = pl.multiple_of(kt * KEY_TILE, KEY_TILE)
            nsf = ns_ref[pl.ds(pl.multiple_of(kt * BLOCKS_PER_TILE, BLOCKS_PER_TILE), NS_ROWS), :].astype(BF16)
            rhs = jnp.concatenate([rq, jnp.concatenate([nsf] * GQA, axis=1), rhs_zero], axis=0)
            lhs = jnp.concatenate([ksl_ref[pl.ds(k0, KEY_TILE), :], eb_ref[...]], axis=1)
            return _dot(lhs, rhs)

        ones_rows = jnp.ones((NS_ROWS, KEY_TILE), BF16)

        def flash_update(s, s_max, k0, carry):
            m_run, acc = carry
            m_new = jnp.maximum(m_run, s_max)
            pe = jnp.exp2(s - m_new).astype(BF16)
            v_aug = jnp.concatenate([vslt_ref[vrows, pl.ds(k0, KEY_TILE)], ones_rows], axis=0)
            return m_new, jnp.exp2(m_run - m_new) * acc + _dot(v_aug, pe)

        def produce(dst_ref, dmx_ref, kt):
            s = tile_scores(kt)
            dst_ref[...] = s
            dmx_ref[...] = jnp.max(s, axis=0, keepdims=True)

        def update_from(src_ref, smx_ref, kt, carry):
            k0 = pl.multiple_of(kt * KEY_TILE, KEY_TILE)
            return tuple(flash_update(src_ref[:, pl.ds(hp * pair, pair)], smx_ref[:, pl.ds(hp * pair, pair)], k0,
                                      carry[hp]) for hp in range(GQA // 2))

        sa_ref, ma_ref = s_ref.at[0], m_ref.at[0]
        produce(sa_ref, ma_ref, 0)

        def chain(kt, n, carry):
            for i in range(n):
                produce(s_ref.at[(i + 1) % n], m_ref.at[(i + 1) % n], kt + i + 1)
                carry = update_from(s_ref.at[i], m_ref.at[i], kt + i, carry)
            return carry

        init = (jnp.full((1, pair), NEG_INF, F32), jnp.zeros((HEAD_DIM + NS_ROWS, pair), F32))
        last = n_tiles - 1
        quads = last // TILE_UNROLL
        carry = lax.fori_loop(0, quads, lambda j, c: chain(TILE_UNROLL * j, TILE_UNROLL, c), (init,) * (GQA // 2))
        done = TILE_UNROLL * quads
        n = TILE_UNROLL // 2
        while n >= 2:
            take = last - done >= n
            carry = lax.cond(take, lambda c, done=done, n=n: chain(done, n, c), lambda c: c, carry)
            done = done + jnp.where(take, n, 0)
            n //= 2

        def odd_step(carry):
            carry = update_from(sa_ref, ma_ref, last - 1, carry)
            produce(sa_ref, ma_ref, last)
            return carry

        carry = lax.cond(lax.rem(last, 2) == 1, odd_step, lambda c: c, carry)
        k0 = pl.multiple_of(last * KEY_TILE, KEY_TILE)
        o_slc_parts = []
        for hp in range(GQA // 2):
            s = sa_ref[:, pl.ds(hp * pair, pair)]
            s = jnp.where(k0 + key_row <= qpos_c[:, hp * pair:(hp + 1) * pair], s, NEG_INF)
            _, acc = flash_update(s, jnp.max(s, axis=0, keepdims=True), k0, carry[hp])
            o_slc_parts.append(acc[0:HEAD_DIM] * (1.0 / acc[HEAD_DIM:HEAD_DIM + 1]))
        o_slc = jnp.concatenate(o_slc_parts, axis=1)

        w0 = pl.multiple_of(jnp.maximum(s0 - WINDOW, 0), Q_BLOCK)
        dist = qp - (w0 + lax.broadcasted_iota(jnp.int32, (WINDOW + Q_BLOCK, 1), 0))
        wbias = jnp.where(lax.bitcast_convert_type(dist, jnp.uint32) <= WINDOW, 0.0, NEG_INF)
        sw = _dot(kwn_ref[pl.ds(w0, WINDOW + Q_BLOCK), :], rq) + jnp.concatenate([wbias] * GQA, axis=1)
        pw = _softmax_rows(sw, None)
        o_win = _dot(vwnt_ref[vrows, pl.ds(w0, WINDOW + Q_BLOCK)], pw.astype(BF16))

        outs = []
        for h in range(GQA):
            c = slice(h * Q_BLOCK, (h + 1) * Q_BLOCK)
            gr = (GQA * g + h) * N_BRANCH
            outs.append(gatet_ref[pl.ds(gr, 1), :] * o_cmp[:, c] + gatet_ref[pl.ds(gr + 1, 1), :] * o_slc[:, c]
                        + gatet_ref[pl.ds(gr + 2, 1), :] * o_win[:, c])
        for hp in range(GQA // 2):
            o2 = jnp.concatenate(outs[2 * hp:2 * hp + 2], axis=0)
            o_ref[:, hp * LANES:(hp + 1) * LANES] = o2.T.astype(BF16)


def _attn_prompt_call(qt, gatet, kc, vct, covert, ksl, vslt, kwn, vwnt, ebias):
    t = qt.shape[1]
    ns, nc = covert.shape
    vmem = pl.BlockSpec(memory_space=pltpu.VMEM)
    return pl.pallas_call(
        functools.partial(_attn_prompt_kernel, nc=nc, ns=ns),
        grid=(t // Q_BLOCK, N_KV_HEADS),
        in_specs=[pl.BlockSpec((GQA * LANES, Q_BLOCK), lambda i, g: (g, i)),
                  pl.BlockSpec((LANES, Q_BLOCK), lambda i, g: (0, i)),
                  vmem, vmem, vmem, vmem, vmem, vmem, vmem, vmem],
        out_specs=pl.BlockSpec((Q_BLOCK, GQA * HEAD_DIM), lambda i, g: (i, g)),
        out_shape=jax.ShapeDtypeStruct((t, N_HEADS * HEAD_DIM), BF16),
        scratch_shapes=[pltpu.VMEM((ns + 8, Q_BLOCK), F32),
                        pltpu.VMEM((TILE_UNROLL, KEY_TILE, GQA * Q_BLOCK), F32),
                        pltpu.VMEM((TILE_UNROLL, 1, GQA * Q_BLOCK), F32),
                        pltpu.VMEM((HEAD_DIM, GQA * Q_BLOCK), F32)],
        compiler_params=pltpu.CompilerParams(dimension_semantics=("arbitrary", "arbitrary"),
                                             vmem_limit_bytes=VMEM_LIMIT),
        name="attn_prompt",
    )(qt, gatet, kc, vct, covert, ksl, vslt, kwn, vwnt, ebias)


def _out_ffn_kernel(x_ref, oa_ref, ob_ref, oc_ref, g1_ref, sc2_ref, sh2_ref, g2_ref, wo_ref, lng_ref, lnb_ref,
                    wup_ref, cfw_ref, cfb_ref, fpast_ref, wdn_ref, xo_ref, fstate_ref, us_ref,
                    *, shift, tm, pad, alpha, d_ff):
    i = pl.program_id(0)

    @pl.when(i == 0)
    def _():
        us_ref[pl.ds(pad - 2 * shift, 2 * shift), :] = fpast_ref[...]

    mix = _dot(jnp.concatenate([oa_ref[...], ob_ref[...], oc_ref[...]], axis=1), wo_ref[...])
    x1 = _layer_norm(alpha * x_ref[...] + (1.0 + g1_ref[...]) * mix, lng_ref[0:1, :], lnb_ref[0:1, :])
    h2 = (x1 * (1.0 + sc2_ref[...]) + sh2_ref[...]).astype(BF16)
    up = _dot(h2, wup_ref[...])
    ua = up[:, :d_ff]
    us_ref[pl.ds(pad, tm), :] = ua
    yc = _shifted_conv(us_ref, ua, cfw_ref, cfb_ref, pad, shift, tm)
    tail = us_ref[pl.ds(pad + tm - 2 * shift, 2 * shift), :]
    fstate_ref[...] = tail
    us_ref[pl.ds(pad - 2 * shift, 2 * shift), :] = tail
    act = (yc * _sigmoid(yc) * up[:, d_ff:]).astype(BF16)
    y = _dot(act, wdn_ref[...])
    xo_ref[...] = _layer_norm(alpha * x1 + (1.0 + g2_ref[...]) * y, lng_ref[1:2, :], lnb_ref[1:2, :])


def _out_ffn_call(x, oa, ob, oc, g1, sc2, sh2, g2, wo, lng, lnb, wup, cfw, cfb, fpast, wdn, *, shift, tm, alpha, name):
    rows, d_model = x.shape
    d_ff = wdn.shape[0]
    pad = max(8, 2 * shift)
    mr = g1.shape[0]
    mod_spec = (pl.BlockSpec((1, d_model), lambda i: (0, 0)) if mr == 1
                else pl.BlockSpec((tm, d_model), lambda i: (i, 0)))

    def row_spec(n):
        return pl.BlockSpec((tm, n), lambda i: (i, 0))

    vmem = pl.BlockSpec(memory_space=pltpu.VMEM)
    return pl.pallas_call(
        functools.partial(_out_ffn_kernel, shift=shift, tm=tm, pad=pad, alpha=alpha, d_ff=d_ff),
        grid=(rows // tm,),
        in_specs=[row_spec(d_model), row_spec(256), row_spec(N_HEADS * HEAD_DIM), row_spec(256), mod_spec, mod_spec, mod_spec,
                  mod_spec, vmem, vmem, vmem, vmem, vmem, vmem, vmem, vmem],
        out_specs=[row_spec(d_model), pl.BlockSpec((2 * shift, d_ff), lambda i: (0, 0))],
        out_shape=[jax.ShapeDtypeStruct((rows, d_model), F32), jax.ShapeDtypeStruct((2 * shift, d_ff), F32)],
        scratch_shapes=[pltpu.VMEM((pad + tm, d_ff), F32)],
        compiler_params=pltpu.CompilerParams(dimension_semantics=("arbitrary",), vmem_limit_bytes=VMEM_LIMIT),
        name=name,
    )(x, oa, ob, oc, g1, sc2, sh2, g2, wo, lng, lnb, wup, cfw, cfb, fpast, wdn)


def _cmp_stream_kernel(pt_ref, cache_ref, perm_ref, w1_ref, pet_ref, w2_ref, kc_ref, vc_ref, buf, rbuf, sem, carry,
                       *, page_base, n_pages, pg, n_groups, total):
    b = pl.program_id(0)
    gi = pl.program_id(1)
    step = b * n_groups + gi
    slot = lax.rem(step, 2)
    m = pg * (PAGE_SIZE // CMP_STRIDE)

    def page_copy(page, slt, i):
        return pltpu.make_async_copy(cache_ref.at[page, pl.ds(0, 2 * LANES), :], buf.at[slt, i], sem.at[slt])

    def issue(stp, slt):
        base = lax.div(stp, n_groups) * n_pages + lax.rem(stp, n_groups) * pg
        for i in range(pg):
            page_copy(page_base + pt_ref[base + i], slt, i).start()

    @pl.when(step == 0)
    def _():
        carry[...] = jnp.zeros(carry.shape, F32)
        issue(step, slot)

    @pl.when(step + 1 < total)
    def _():
        issue(step + 1, 1 - slot)

    for i in range(pg):
        page_copy(0, slot, i).wait()

    per_chunk = PAGE_SIZE // CMP_STRIDE
    for i in range(pg):
        rbuf[i] = _dot_nt(perm_ref[...], buf[slot, i].astype(BF16))

    def rows_of(j, s):
        return jnp.concatenate([rbuf[i, pl.ds(j * per_chunk, per_chunk), pl.ds(s * LANES, LANES)]
                                for i in range(pg)], axis=0)

    lane = lax.broadcasted_iota(jnp.int32, (1, LANES), 1)
    first_half = lane < HEAD_DIM
    row0 = lax.broadcasted_iota(jnp.int32, (m, 1), 0) == 0
    pieces = [[[], []], [[], []]]
    for pr in range(CMP_STRIDE // 2):
        for s in range(2):
            a = rows_of(2 * pr, s)
            bb = rows_of(2 * pr + 1, s)
            pieces[s][0].append(jnp.where(first_half, a, pltpu.roll(bb, HEAD_DIM, 1)).astype(BF16))
            pieces[s][1].append(jnp.where(first_half, pltpu.roll(a, HEAD_DIM, 1), bb).astype(BF16))
    for s, o_ref in enumerate((kc_ref, vc_ref)):
        acc = jnp.zeros((m, LANES), F32)
        for g in range(N_KV_HEADS):
            parts = _dot(jnp.concatenate(pieces[s][g], axis=1), w1_ref[s])
            p0 = parts[:, :CMP_HID]
            p1 = parts[:, CMP_HID:]
            prev = jnp.where(gi == 0, 0.0, carry[s * 2 + g][0:1, :])
            p0s = jnp.where(row0, prev, pltpu.roll(p0, 1, 0))
            carry[s * 2 + g] = jnp.broadcast_to(p0[m - 1:m, :], (8, CMP_HID))
            hid = _gelu(p0s + p1 + pet_ref[s][0:1, :])
            acc = acc + _dot(hid.astype(BF16), w2_ref[s, g])
        o_ref[0] = acc.astype(BF16)


def _cmp_stream_call(pt_flat, cache_t, w1s, pet, w2p, *, layer, n_phys, batch, n_pages, pg):
    n_groups = n_pages // pg
    m = pg * (PAGE_SIZE // CMP_STRIDE)
    nc = n_pages * (PAGE_SIZE // CMP_STRIDE)
    total = batch * n_groups
    out_row = np.arange(PAGE_SIZE)[:, None]
    src_row = (out_row % (PAGE_SIZE // CMP_STRIDE)) * CMP_STRIDE + out_row // (PAGE_SIZE // CMP_STRIDE)
    perm = jnp.asarray((np.arange(PAGE_SIZE)[None, :] == src_row).astype(np.float32), dtype=BF16)

    def full(a):
        nd = a.ndim
        return pl.BlockSpec(a.shape, lambda b, g, pt: (0,) * nd)

    grid_spec = pltpu.PrefetchScalarGridSpec(
        num_scalar_prefetch=1,
        grid=(batch, n_groups),
        in_specs=[pl.BlockSpec(memory_space=pl.ANY), full(perm), full(w1s), full(pet), full(w2p)],
        out_specs=[pl.BlockSpec((1, m, LANES), lambda b, g, pt: (b, g, 0))] * 2,
        scratch_shapes=[pltpu.VMEM((2, pg, 2 * LANES, PAGE_SIZE), F32), pltpu.VMEM((pg, PAGE_SIZE, 2 * LANES), F32),
                        pltpu.SemaphoreType.DMA((2,)), pltpu.VMEM((4, 8, CMP_HID), F32)],
    )
    return pl.pallas_call(
        functools.partial(_cmp_stream_kernel, page_base=layer * n_phys, n_pages=n_pages, pg=pg,
                          n_groups=n_groups, total=total),
        grid_spec=grid_spec,
        out_shape=[jax.ShapeDtypeStruct((batch, nc, LANES), BF16)] * 2,
        compiler_params=pltpu.CompilerParams(dimension_semantics=("arbitrary", "arbitrary"),
                                             vmem_limit_bytes=VMEM_LIMIT),
        name="cmp_stream_sample",
    )(pt_flat, cache_t, perm, w1s, pet, w2p)


def _cmp_attn_sample_kernel(q_ref, kc_ref, vc_ref, cover_ref, ocmp_ref, idx_ref, bias_ref,
                            *, past, nc, ns, nsp, n_cache_blocks):
    rows = lax.broadcasted_iota(jnp.int32, (32, 1), 0)
    qpos_r = past + (rows & 7)
    qp = past + lax.broadcasted_iota(jnp.int32, (8, 1), 0)
    m_idx = lax.broadcasted_iota(jnp.int32, (1, nc), 1)
    blk = lax.broadcasted_iota(jnp.int32, (1, nsp), 1)
    blk_f = blk.astype(F32)
    lane = lax.broadcasted_iota(jnp.int32, (1, LANES), 1)
    cmask = (m_idx >= 1) & (CMP_STRIDE * (m_idx - 1) + CMP_LEN - 1 <= qpos_r)
    for g in range(N_KV_HEADS):
        p = _masked_softmax(_dot_nt(q_ref[0, g], kc_ref[0]), cmask, -1)
        ocmp_ref[0, g] = _dot(p.astype(BF16), vc_ref[0])
        p4 = p[0:8] + p[8:16] + p[16:24] + p[24:32]
        hi, lo = _split(p4)
        imp = _dot(hi, cover_ref[...]) + _dot(lo, cover_ref[...])
        qblk = qp // SLC_BLOCK
        elig = blk * SLC_BLOCK <= qp
        forced = (blk == 0) | (blk == qblk) | (blk == qblk - 1)
        val = jnp.where(blk < ns, jnp.where(elig, jnp.where(forced, FORCE, imp), -FORCE), REMOVED)
        _, firsts, tops = _top_select(val, blk_f, nsp, -1)
        idx = jnp.zeros((8, LANES), F32)
        bias = jnp.zeros((8, LANES), F32)
        for t in range(N_SELECT):
            ok = (tops[t] > -0.5 * FORCE) & (firsts[t] < float(n_cache_blocks))
            odd = firsts[t] - 2.0 * jnp.floor(firsts[t] * 0.5)
            idx = jnp.where(lane == t, firsts[t], idx)
            for hf in range(2):
                bias = jnp.where(lane == 2 * t + hf, jnp.where(ok & (odd == float(hf)), 0.0, NEG_INF), bias)
        idx_ref[0, g] = idx.astype(jnp.int32)
        bias_ref[0, g] = bias


def _cmp_attn_sample_call(q_hq, kc, vc, cover, *, past, ns, n_cache_blocks):
    batch, nc, _ = kc.shape
    nsp = cover.shape[1]
    blk4 = lambda r: pl.BlockSpec((1, N_KV_HEADS, r, LANES), lambda b: (b, 0, 0, 0))
    return pl.pallas_call(
        functools.partial(_cmp_attn_sample_kernel, past=past, nc=nc, ns=ns, nsp=nsp, n_cache_blocks=n_cache_blocks),
        grid=(batch,),
        in_specs=[blk4(32), pl.BlockSpec((1, nc, LANES), lambda b: (b, 0, 0)),
                  pl.BlockSpec((1, nc, LANES), lambda b: (b, 0, 0)), pl.BlockSpec(cover.shape, lambda b: (0, 0))],
        out_specs=[blk4(32), blk4(8), blk4(8)],
        out_shape=[jax.ShapeDtypeStruct((batch, N_KV_HEADS, 32, LANES), F32),
                   jax.ShapeDtypeStruct((batch, N_KV_HEADS, 8, LANES), jnp.int32),
                   jax.ShapeDtypeStruct((batch, N_KV_HEADS, 8, LANES), F32)],
        compiler_params=pltpu.CompilerParams(dimension_semantics=("arbitrary",), vmem_limit_bytes=VMEM_LIMIT),
        name="cmp_attn_sample",
    )(q_hq, kc, vc, cover)


def _sel_attn_sample_kernel(idx_ref, pt_ref, cache_ref, q_ref, bias_ref, ex_ref, knew_ref, vnew_ref, wint_ref,
                            wnew_ref, wnewt_ref, ocmp_ref, gate_ref, o_ref, wout_ref, buf, sem,
                            *, page_base, n_pages, tq, n_cache_blocks, total):
    b = pl.program_id(0)
    g = pl.program_id(1)
    step = b * N_KV_HEADS + g
    slot = lax.rem(step, 2)
    n_ent = tq * N_SELECT
    blocks_per_page = PAGE_SIZE // SLC_BLOCK

    def page_copy(page, slt, e):
        return pltpu.make_async_copy(cache_ref.at[page, pl.ds(2 * LANES, 2 * LANES), :], buf.at[slt, e], sem.at[slt])

    def issue(stp, slt):
        bb = lax.div(stp, N_KV_HEADS)

        def body(e, _):
            blk = jnp.minimum(idx_ref[stp * n_ent + e], n_cache_blocks - 1)
            page = pt_ref[bb * n_pages + lax.div(blk, blocks_per_page)]
            page_copy(page_base + page, slt, e).start()
            return 0
        lax.fori_loop(0, n_ent, body, 0)

    @pl.when(step == 0)
    def _():
        issue(step, slot)

    @pl.when(step + 1 < total)
    def _():
        issue(step + 1, 1 - slot)

    def wait_body(e, _):
        page_copy(0, slot, e).wait()
        return 0
    lax.fori_loop(0, n_ent, wait_body, 0)

    lane = lax.broadcasted_iota(jnp.int32, (1, LANES), 1)
    col8 = lax.broadcasted_iota(jnp.int32, (1, 8), 1)
    wcol = lax.broadcasted_iota(jnp.int32, (1, WINDOW), 1)
    k_wt = wint_ref[0, pl.ds(0, LANES), :].astype(BF16)
    v_wt = wint_ref[0, pl.ds(LANES, LANES), :].astype(BF16)
    wnew = wnew_ref[0]
    k_wn = wnew[:, 0:LANES].astype(BF16)
    v_wn = wnew[:, LANES:2 * LANES].astype(BF16)
    bias_all = _dot(bias_ref[0, 0].astype(BF16), ex_ref[...])
    keep = (lane >= g * HEAD_DIM) & (lane < (g + 1) * HEAD_DIM)
    for qi in range(tq):
        q4 = q_ref[0, 0, qi]
        new_ok = (col8 <= qi) & (col8 < tq)
        k_t = jnp.concatenate([buf[slot, qi * N_SELECT + k, pl.ds(0, LANES), :] for k in range(N_SELECT)], axis=1)
        v_t = jnp.concatenate([buf[slot, qi * N_SELECT + k, pl.ds(LANES, LANES), :] for k in range(N_SELECT)], axis=1)
        s = _dot(q4, k_t.astype(BF16)) + bias_all[qi:qi + 1, :]
        s_n = jnp.where(new_ok, _dot_nt(q4, knew_ref[0]), NEG_INF)
        mx = jnp.maximum(jnp.max(s, axis=-1, keepdims=True), jnp.max(s_n, axis=-1, keepdims=True))
        pe = jnp.exp2(s - mx)
        pn = jnp.exp2(s_n - mx)
        l = jnp.sum(pe, axis=-1, keepdims=True) + jnp.sum(pn, axis=-1, keepdims=True)
        o_slc = (_dot_nt(pe.astype(BF16), v_t.astype(BF16)) + _dot(pn.astype(BF16), vnew_ref[0])) * (1.0 / l)
        sw = jnp.where(wcol >= qi, _dot(q4, k_wt), NEG_INF)
        sw_n = jnp.where(new_ok, _dot_nt(q4, k_wn), NEG_INF)
        mw = jnp.maximum(jnp.max(sw, axis=-1, keepdims=True), jnp.max(sw_n, axis=-1, keepdims=True))
        pw = jnp.exp2(sw - mw)
        pwn = jnp.exp2(sw_n - mw)
        lw = jnp.sum(pw, axis=-1, keepdims=True) + jnp.sum(pwn, axis=-1, keepdims=True)
        o_win = (_dot_nt(pw.astype(BF16), v_wt) + _dot(pwn.astype(BF16), v_wn)) * (1.0 / lw)
        gt = gate_ref[0, 0, qi]
        o = gt[:, 0:1] * ocmp_ref[0, 0, qi] + gt[:, 1:2] * o_slc + gt[:, 2:3] * o_win
        o_ref[0, 0, qi] = jnp.where(keep, o, 0.0)

    @pl.when(g == 0)
    def _():
        shifted = pltpu.roll(wint_ref[0], WINDOW - tq, 1)
        wout_ref[0, :, pl.ds(0, WINDOW - LANES)] = shifted[:, 0:WINDOW - LANES]
        wout_ref[0, :, pl.ds(WINDOW - LANES, LANES)] = jnp.where(lane >= LANES - tq, wnewt_ref[0],
                                                                  shifted[:, WINDOW - LANES:WINDOW])


def _sel_attn_sample_call(idx_flat, pt_flat, cache_t, q_qh, bias, expand, knew, vnew, win_t, wnew, wnew_t, ocmp_qh,
                          gate_qh, *, layer, n_phys, n_pages, tq, n_cache_blocks):
    batch = q_qh.shape[0]
    n_ent = tq * N_SELECT
    b5 = lambda: pl.BlockSpec((1, 1, tq, 8, LANES), lambda b, g, i, p: (b, g, 0, 0, 0))
    b3 = lambda r, c: pl.BlockSpec((1, r, c), lambda b, g, i, p: (b, 0, 0))
    grid_spec = pltpu.PrefetchScalarGridSpec(
        num_scalar_prefetch=2,
        grid=(batch, N_KV_HEADS),
        in_specs=[pl.BlockSpec(memory_space=pl.ANY), b5(),
                  pl.BlockSpec((1, 1, 8, LANES), lambda b, g, i, p: (b, g, 0, 0)),
                  pl.BlockSpec(expand.shape, lambda b, g, i, p: (0, 0)),
                  b3(8, LANES), b3(8, LANES),
                  pl.BlockSpec((1, 2 * LANES, WINDOW), lambda b, g, i, p: (layer * batch + b, 0, 0)),
                  b3(8, 2 * LANES), b3(2 * LANES, LANES),
                  b5(), b5()],
        out_specs=[b5(), b3(2 * LANES, WINDOW)],
        scratch_shapes=[pltpu.VMEM((2, n_ent, 2 * LANES, PAGE_SIZE), F32), pltpu.SemaphoreType.DMA((2,))],
    )
    return pl.pallas_call(
        functools.partial(_sel_attn_sample_kernel, page_base=layer * n_phys, n_pages=n_pages, tq=tq,
                          n_cache_blocks=n_cache_blocks, total=batch * N_KV_HEADS),
        grid_spec=grid_spec,
        out_shape=[jax.ShapeDtypeStruct((batch, N_KV_HEADS, tq, 8, LANES), F32),
                   jax.ShapeDtypeStruct((batch, 2 * LANES, WINDOW), F32)],
        compiler_params=pltpu.CompilerParams(dimension_semantics=("arbitrary", "arbitrary"),
                                             vmem_limit_bytes=VMEM_LIMIT),
        name="sel_attn_sample",
    )(idx_flat, pt_flat, cache_t, q_qh, bias, expand, knew, vnew, win_t, wnew, wnew_t, ocmp_qh, gate_qh)


def _prep_w_in(w):
    d = w.shape[0]
    wq = (w[:, 768:1280] * (HEAD_DIM ** -0.5 * LOG2E)).reshape(d, N_HEADS, HEAD_DIM)
    z = jnp.zeros_like(wq)
    grp = (jnp.arange(N_HEADS) // GQA)[None, :, None]
    wq = jnp.concatenate([jnp.where(grp == 0, wq, z), jnp.where(grp == 1, wq, z)], axis=-1).reshape(d, N_HEADS * LANES)
    gate = jnp.pad(w[:, 2048:2072], ((0, 0), (0, LANES - N_HEADS * N_BRANCH)))
    return jnp.concatenate([w[:, :768], wq, w[:, 1280:2048], gate, w[:, 2072:]], axis=1).astype(BF16)


def _prep_w1(w1):
    w = w1.reshape(2, 2, CMP_STRIDE, HEAD_DIM, CMP_HID)
    return jnp.transpose(w, (0, 2, 3, 1, 4)).reshape(2, CMP_STRIDE * HEAD_DIM, 2 * CMP_HID)


def _prep_w1_grouped(w1s):
    w = w1s.reshape(2, CMP_STRIDE, 1, HEAD_DIM, 1, 2 * CMP_HID)
    eye = jnp.eye(N_KV_HEADS, dtype=w.dtype).reshape(1, 1, N_KV_HEADS, 1, N_KV_HEADS, 1)
    return (w * eye).reshape(2, CMP_STRIDE * N_KV_HEADS * HEAD_DIM, N_KV_HEADS * 2 * CMP_HID)


def _prep_w2(w2):
    z = jnp.zeros_like(w2)
    return jnp.stack([jnp.concatenate([w2, z], axis=-1), jnp.concatenate([z, w2], axis=-1)], axis=1).astype(BF16)


def _cover_matrix(nc, ns_real, ns_pad):
    m = np.arange(nc)[:, None]
    b = np.arange(ns_pad)[None, :]
    return ((m >= 4 * b) & (m <= 4 * b + 4) & (m >= 1) & (b < ns_real)).astype(np.float32)


def _block_bias_matrix():
    k = np.arange(KEY_TILE)[:, None]
    b = np.arange(LANES)[None, :]
    return jnp.asarray(np.where(k // SLC_BLOCK == b, NEG_INF, 0.0).astype(np.float32), dtype=BF16)


def _expand_matrix():
    r = np.arange(LANES)[:, None]
    c = np.arange(N_SELECT * PAGE_SIZE)[None, :]
    return jnp.asarray((c // SLC_BLOCK == r).astype(np.float32), dtype=BF16)


def kernel(x_prompt, x_sample, cache_nsa_kv, state_win_kv, state_conv, state_ffn_conv, page_table, c_prompt, c_sample, w_ada, b_ada, w_in, conv_a_w, conv_a_b, cmp_pe, cmp_w1, cmp_w2, sgu_ln_g, sgu_ln_b, sgu_w, sgu_b, w_o, ln_g, ln_b, w_ffn_up, conv_f_w, conv_f_b, w_ffn_down):
    depth = w_in.shape[0]
    _, t, d_model = x_prompt.shape
    nb, tq, _ = x_sample.shape
    n_phys = cache_nsa_kv.shape[1]
    n_pages = page_table.shape[1]
    past = n_pages * PAGE_SIZE
    d_ff = w_ffn_down.shape[1]
    alpha = (2 * depth) ** 0.25
    rs = nb * tq
    kvw = 4 * N_KV_HEADS * HEAD_DIM
    assert x_prompt.shape[0] == 1 and c_prompt.shape[0] == 1
    assert d_model == 1024 and t % KEY_TILE == 0 and t >= WINDOW + Q_BLOCK
    assert tq == 4 and rs == GMLP_CHUNK and past % KEY_TILE == 0 and past >= WINDOW
    assert state_win_kv.shape[2] == WINDOW

    rc = -(-(1 + nb) // 8) * 8
    c_all = jnp.pad(jnp.concatenate([c_prompt, c_sample], axis=0), ((0, rc - 1 - nb), (0, 0)))
    mods = _ada_call(c_all, w_ada, b_ada)

    nc_p, ns_p = t // CMP_STRIDE, t // SLC_BLOCK
    covert_p = jnp.asarray(_cover_matrix(nc_p, ns_p, ns_p).T, dtype=BF16)
    nc_s = (past + tq) // CMP_STRIDE
    ns_s = -(-(past + tq) // SLC_BLOCK)
    ns_s_pad = -(-ns_s // LANES) * LANES
    n_cache_blocks = past // SLC_BLOCK
    cover_s = jnp.asarray(_cover_matrix(nc_s, ns_s, ns_s_pad), dtype=BF16)
    ebias = _block_bias_matrix()
    expand = _expand_matrix()
    pt_flat = page_table.reshape(-1)
    cache_t = jnp.transpose(cache_nsa_kv.reshape(depth * n_phys, PAGE_SIZE, kvw), (0, 2, 1))
    win_t_all = jnp.transpose(state_win_kv.reshape(depth * nb, WINDOW, 2 * LANES), (0, 2, 1))

    xp = x_prompt[0]
    xs = jnp.transpose(x_sample, (1, 0, 2)).reshape(rs, d_model)
    tril_full = jnp.tril(jnp.ones((GMLP_CHUNK, GMLP_CHUNK), F32))
    tril_tq = jnp.tril(jnp.ones((tq, tq), F32))

    ps, ss = [], []
    for l in range(depth):
        w_all = _prep_w_in(w_in[l])
        wo_p = w_o[l].astype(BF16)
        wup = w_ffn_up[l].astype(BF16)
        wdn = w_ffn_down[l].astype(BF16)
        w1s = _prep_w1(cmp_w1[l])
        w1g = _prep_w1_grouped(w1s).astype(BF16)
        w1s = w1s.astype(BF16)
        w2p = _prep_w2(cmp_w2[l])
        pe_flat = jnp.broadcast_to(cmp_pe[l].reshape(2, 1, CMP_LEN * HEAD_DIM), (2, 8, CMP_LEN * HEAD_DIM))
        pet = _pe_term_call(pe_flat, cmp_w1[l].reshape(2, CMP_LEN * HEAD_DIM, CMP_HID))
        cw, cb = conv_a_w[l], conv_a_b[l].reshape(1, -1)
        cfw, cfb = conv_f_w[l], conv_f_b[l].reshape(1, -1)
        lng, lnb = sgu_ln_g[l].reshape(1, -1), sgu_ln_b[l].reshape(1, -1)

        def mod_rows(r0, r1, rep):
            parts = [mods[l, r0:r1, k * d_model:(k + 1) * d_model] for k in range(6)]
            return [jnp.tile(p_, (rep, 1)) if rep > 1 else p_ for p_ in parts]

        sh1, sc1, g1, sh2, sc2, g2 = mod_rows(0, 1, 1)
        wm_p = (sgu_w[l] * tril_full).astype(BF16)
        sb_p = jnp.repeat(sgu_b[l].T, HEAD_DIM, axis=1)
        (oa, oc, cst, qt, kvp, kvt, kcr, vcr, ksl, kwn, vslt, vwnt, gatet) = _in_proj_call(
            xp, sc1, sh1, w_all, cw, cb, jnp.zeros((2, 256), F32), lng, lnb, wm_p, sb_p,
            shift=1, tm=512, transposed=True, name="in_proj_prompt")
        kc, vct = _compress_prompt_call(kcr.reshape(nc_p, CMP_STRIDE * LANES), vcr.reshape(nc_p, CMP_STRIDE * LANES),
                                        w1g, pet, w2p)
        ob = _attn_prompt_call(qt, gatet, kc, vct, covert_p, ksl, vslt, kwn, vwnt, ebias)
        xp, fst = _out_ffn_call(xp, oa, ob, oc, g1, sc2, sh2, g2, wo_p, ln_g[l], ln_b[l], wup, cfw, cfb,
                                jnp.zeros((2, d_ff), F32), wdn, shift=1, tm=256, alpha=alpha, name="out_ffn_prompt")
        paged = kvp.reshape(t // PAGE_SIZE, 4, N_KV_HEADS, HEAD_DIM, PAGE_SIZE)
        winr = kvt[:, t - WINDOW:].reshape(2, N_KV_HEADS, HEAD_DIM, WINDOW)
        ps.append((jnp.transpose(paged, (0, 4, 1, 2, 3))[None], jnp.transpose(winr, (3, 0, 1, 2))[None],
                   cst[None], fst[None]))

        sh1, sc1, g1, sh2, sc2, g2 = mod_rows(1, 1 + nb, tq)
        eye_b = jnp.eye(nb, dtype=F32)
        wm_s = jax.vmap(lambda w: jnp.kron(w[:tq, :tq] * tril_tq, eye_b))(sgu_w[l]).astype(BF16)
        sb_s = jnp.repeat(jnp.repeat(sgu_b[l][:, :tq].T, nb, axis=0), HEAD_DIM, axis=1)
        cpast = jnp.transpose(state_conv[l], (1, 0, 2)).reshape(2 * nb, -1)
        (oa, oc, cst, q, kvf, ksl, vsl, gate, vrow) = _in_proj_call(
            xs, sc1, sh1, w_all, cw, cb, cpast, lng, lnb, wm_s, sb_s, shift=nb, tm=rs, transposed=False,
            name="in_proj_sample")
        kc, vc = _cmp_stream_call(pt_flat, cache_t, w1s, pet, w2p, layer=l, n_phys=n_phys, batch=nb,
                                  n_pages=n_pages, pg=min(32, n_pages))

        def by_batch(a):
            return jnp.transpose(a.reshape(tq, nb, -1), (1, 0, 2))

        qb5 = by_batch(q).reshape(nb, tq, N_KV_HEADS, GQA, LANES)
        q_hq = jnp.pad(jnp.transpose(qb5, (0, 2, 3, 1, 4)), ((0, 0), (0, 0), (0, 0), (0, 8 - tq), (0, 0)))
        q_hq = q_hq.reshape(nb, N_KV_HEADS, 32, LANES)
        q_qh = jnp.pad(jnp.transpose(qb5, (0, 2, 1, 3, 4)), ((0, 0), (0, 0), (0, 0), (0, 8 - GQA), (0, 0)))
        ocmp, idx, bias = _cmp_attn_sample_call(q_hq, kc, vc, cover_s, past=past, ns=ns_s,
                                                n_cache_blocks=n_cache_blocks)
        ocmp_qh = jnp.transpose(ocmp.reshape(nb, N_KV_HEADS, GQA, 8, LANES)[:, :, :, :tq], (0, 1, 3, 2, 4))
        ocmp_qh = jnp.pad(ocmp_qh, ((0, 0), (0, 0), (0, 0), (0, 8 - GQA), (0, 0)))
        g5 = by_batch(gate)[:, :, :N_HEADS * N_BRANCH].reshape(nb, tq, N_KV_HEADS, GQA, N_BRANCH)
        gate_qh = jnp.pad(jnp.transpose(g5, (0, 2, 1, 3, 4)),
                          ((0, 0), (0, 0), (0, 0), (0, 8 - GQA), (0, LANES - N_BRANCH)))
        pad8 = lambda a: jnp.pad(by_batch(a), ((0, 0), (0, 8 - tq), (0, 0)))
        wnew_rows = by_batch(kvf[:, kvw:])
        wnew_t = jnp.pad(jnp.transpose(wnew_rows, (0, 2, 1)), ((0, 0), (0, 0), (LANES - tq, 0)))
        o5, wout_t = _sel_attn_sample_call(
            idx[:, :, :tq, :N_SELECT].reshape(-1), pt_flat, cache_t, q_qh, bias, expand, pad8(ksl), pad8(vsl),
            win_t_all, jnp.pad(wnew_rows, ((0, 0), (0, 8 - tq), (0, 0))), wnew_t, ocmp_qh, gate_qh,
            layer=l, n_phys=n_phys, n_pages=n_pages, tq=tq, n_cache_blocks=n_cache_blocks)
        o5 = o5[:, :, :, :GQA, :HEAD_DIM] + o5[:, :, :, :GQA, HEAD_DIM:]
        ob = jnp.transpose(o5, (2, 0, 1, 3, 4)).reshape(rs, N_HEADS * HEAD_DIM).astype(BF16)
        fpast = jnp.transpose(state_ffn_conv[l], (1, 0, 2)).reshape(2 * nb, -1)
        xs, fst = _out_ffn_call(xs, oa, ob, oc, g1, sc2, sh2, g2, wo_p, ln_g[l], ln_b[l], wup, cfw, cfb, fpast, wdn,
                                shift=nb, tm=rs, alpha=alpha, name="out_ffn_sample")
        ss.append((by_batch(kvf[:, :kvw]).reshape(nb, tq, 4, N_KV_HEADS, HEAD_DIM),
                   jnp.transpose(wout_t, (0, 2, 1)).reshape(nb, WINDOW, 2, N_KV_HEADS, HEAD_DIM),
                   jnp.transpose(cst.reshape(2, nb, -1), (1, 0, 2)),
                   jnp.transpose(fst.reshape(2, nb, -1), (1, 0, 2)),
                   by_batch(vrow)))

    ys = jnp.transpose(xs.reshape(tq, nb, d_model), (1, 0, 2))
    return (xp[None], ys,
            jnp.stack([s[0] for s in ps]), jnp.stack([s[1] for s in ps]),
            jnp.stack([s[2] for s in ps]), jnp.stack([s[3] for s in ps]),
            jnp.stack([s[0] for s in ss]), jnp.stack([s[1] for s in ss]),
            jnp.stack([s[2] for s in ss]), jnp.stack([s[3] for s in ss]),
            jnp.stack([s[4] for s in ss]))
```

```python
import functools
import math

import numpy as np
import jax
import jax.numpy as jnp
from jax import lax
from jax.experimental import pallas as pl
from jax.experimental.pallas import tpu as pltpu

F32 = jnp.float32
BF16 = jnp.bfloat16

HEAD_DIM = 64
N_HEADS = 8
N_KV_HEADS = 2
GQA = N_HEADS // N_KV_HEADS
N_BRANCH = 3
CONV_K = 3
CMP_LEN = 32
CMP_STRIDE = 16
CMP_HID = 128
SLC_BLOCK = 64
N_SELECT = 16
WINDOW = 512
Q_BLOCK = 128
PAGE_SIZE = 128
GMLP_CHUNK = 128
GMLP_GROUPS = 4
LN_EPS = 1e-5
NEG_INF = -1e30
FORCE = 1e4
REMOVED = -3e38
LOG2E = 1.4426950408889634

LANES = 128
KEY_TILE = 512
BLOCKS_PER_TILE = KEY_TILE // SLC_BLOCK
NS_ROWS = 16
TILE_UNROLL = 4
SEQS_PER_SELECT_STEP = 8
CAUSAL_VARIANTS = 4
VMEM_LIMIT = 56 * 1024 * 1024


def _dot(a, b):
    return jnp.dot(a, b, preferred_element_type=F32)


def _dot_nt(a, b):
    return lax.dot_general(a, b, (((1,), (1,)), ((), ())), preferred_element_type=F32)


def _split(a):
    hi = a.astype(BF16)
    lo = (a - hi.astype(F32)).astype(BF16)
    return hi, lo


def _dot3(a, b):
    ah, al = _split(a)
    bh, bl = _split(b)
    return _dot(ah, bh) + _dot(ah, bl) + _dot(al, bh)


def _sigmoid(x):
    return 1.0 / (1.0 + jnp.exp(-x))


def _gelu(x):
    c = math.sqrt(2.0 / math.pi)
    return 0.5 * x * (1.0 + jnp.tanh(c * (x + 0.044715 * (x * x * x))))


def _layer_norm(x, g, b):
    mu = jnp.mean(x, axis=-1, keepdims=True)
    xc = x - mu
    var = jnp.mean(xc * xc, axis=-1, keepdims=True)
    return xc * lax.rsqrt(var + LN_EPS) * g + b


def _masked_softmax(s, mask, axis):
    sm = jnp.where(mask, s, NEG_INF)
    mx = jnp.max(sm, axis=axis, keepdims=True)
    e = jnp.where(mask, jnp.exp2(s - mx), 0.0)
    l = jnp.sum(e, axis=axis, keepdims=True)
    return e * (1.0 / jnp.where(l > 0.0, l, 1.0))


def _softmax_rows(sm, col_valid):
    e = jnp.exp2(sm - jnp.max(sm, axis=0, keepdims=True))
    inv = 1.0 / jnp.sum(e, axis=0, keepdims=True)
    if col_valid is not None:
        inv = jnp.where(col_valid, inv, 0.0)
    return e * inv


def _top_select(val, blk_f, n_blk, axis):
    sel = jnp.zeros_like(val)
    firsts, tops = [], []
    for _ in range(N_SELECT):
        mx = jnp.max(val, axis=axis, keepdims=True)
        first = jnp.min(jnp.where(val == mx, blk_f, float(n_blk)), axis=axis, keepdims=True)
        hit = blk_f == first
        sel = jnp.where(hit, 1.0, sel)
        val = jnp.where(hit, REMOVED, val)
        firsts.append(first)
        tops.append(mx)
    return sel, firsts, tops


def _shifted_conv(src_ref, x, w_ref, b_ref, pad, shift, rows):
    x2 = src_ref[pl.ds(pad - 2 * shift, rows), :]
    x1 = src_ref[pl.ds(pad - shift, rows), :]
    return w_ref[0:1, :] * x2 + w_ref[1:2, :] * x1 + w_ref[2:3, :] * x + b_ref[...]


def _ada_kernel(c_ref, w_ref, b_ref, o_ref):
    c = c_ref[...]
    o_ref[0] = _dot3(c * _sigmoid(c), w_ref[0]) + b_ref[0]


def _ada_call(c_all, w_ada, b_ada):
    depth, d_model, n_mod = w_ada.shape
    rc = c_all.shape[0]
    tn = 1024
    return pl.pallas_call(
        _ada_kernel,
        grid=(depth, n_mod // tn),
        in_specs=[pl.BlockSpec((rc, d_model), lambda l, n: (0, 0)),
                  pl.BlockSpec((1, d_model, tn), lambda l, n: (l, 0, n)),
                  pl.BlockSpec((1, 1, tn), lambda l, n: (l, 0, n))],
        out_specs=pl.BlockSpec((1, rc, tn), lambda l, n: (l, 0, n)),
        out_shape=jax.ShapeDtypeStruct((depth, rc, n_mod), F32),
        compiler_params=pltpu.CompilerParams(dimension_semantics=("arbitrary", "arbitrary"),
                                             vmem_limit_bytes=VMEM_LIMIT),
        name="ada_mod",
    )(c_all, w_ada, b_ada.reshape(depth, 1, n_mod))


_C_AB, _C_AC, _C_AH = 0, 256, 512
_C_Q = 768
_C_KV = 1792
_C_GATE = 2560
_C_GU = 2688
_C_GV = 2944
_N_COL = 3200
_N_SLOT = 6


def _in_proj_kernel(x_ref, sc_ref, sh_ref, w_ref, cw_ref, cb_ref, cpast_ref, lng_ref, lnb_ref, wm_ref, sb_ref,
                    *rest, shift, tm, pad, transposed):
    if transposed:
        (oa_ref, oc_ref, cstate_ref, qt_ref, kvp_ref, kvt_ref, kcr_ref, vcr_ref, ksl_ref, kwn_ref, vslt_ref, vwnt_ref,
         gatet_ref, zs_ref) = rest
    else:
        (oa_ref, oc_ref, cstate_ref, q_ref, kvf_ref, ksl_ref, vsl_ref, gate_ref, vrow_ref, zs_ref) = rest
    i = pl.program_id(0)

    @pl.when(i == 0)
    def _():
        zs_ref[pl.ds(pad - 2 * shift, 2 * shift), :] = cpast_ref[...]

    h = (x_ref[...] * (1.0 + sc_ref[...]) + sh_ref[...]).astype(BF16)
    p = _dot(h, w_ref[...])

    z = p[:, _C_AC:_C_AC + 256] * p[:, _C_AH:_C_AH + 256]
    zs_ref[pl.ds(pad, tm), :] = z
    y = _shifted_conv(zs_ref, z, cw_ref, cb_ref, pad, shift, tm)
    oa_ref[...] = (p[:, _C_AB:_C_AB + 256] * y).astype(BF16)
    tail = zs_ref[pl.ds(pad + tm - 2 * shift, 2 * shift), :]
    cstate_ref[...] = tail
    zs_ref[pl.ds(pad - 2 * shift, 2 * shift), :] = tail

    kv = [p[:, _C_KV + k * LANES:_C_KV + (k + 1) * LANES] for k in range(_N_SLOT)]
    gate = _sigmoid(p[:, _C_GATE:_C_GATE + LANES])
    if transposed:
        for hh in range(N_HEADS):
            qt_ref[pl.ds(hh * LANES, LANES), :] = p[:, _C_Q + hh * LANES:_C_Q + (hh + 1) * LANES].T.astype(BF16)
        kvt = [a.T for a in kv]
        for k in range(4):
            for pg in range(tm // PAGE_SIZE):
                kvp_ref[pg, pl.ds(k * LANES, LANES), :] = kvt[k][:, pg * PAGE_SIZE:(pg + 1) * PAGE_SIZE]
        for k in range(4, _N_SLOT):
            kvt_ref[pl.ds((k - 4) * LANES, LANES), :] = kvt[k]
        kcr_ref[...] = kv[0].astype(BF16)
        vcr_ref[...] = kv[1].astype(BF16)
        ksl_ref[...] = kv[2].astype(BF16)
        kwn_ref[...] = kv[4].astype(BF16)
        vslt_ref[...] = kvt[3].astype(BF16)
        vwnt_ref[...] = kvt[5].astype(BF16)
        gatet_ref[...] = gate.T
    else:
        q_ref[...] = p[:, _C_Q:_C_Q + 1024].astype(BF16)
        kvf_ref[...] = p[:, _C_KV:_C_KV + _N_SLOT * LANES]
        ksl_ref[...] = kv[2].astype(BF16)
        vsl_ref[...] = kv[3].astype(BF16)
        gate_ref[...] = gate

    u = _gelu(p[:, _C_GU:_C_GU + 256])
    v = _layer_norm(_gelu(p[:, _C_GV:_C_GV + 256]), lng_ref[...], lnb_ref[...])
    if not transposed:
        vrow_ref[...] = v
    lane = lax.broadcasted_iota(jnp.int32, (1, 256), 1)
    for c in range(tm // GMLP_CHUNK):
        vc = v[c * GMLP_CHUNK:(c + 1) * GMLP_CHUNK]
        mixed = sb_ref[...]
        for g in range(GMLP_GROUPS):
            vg = jnp.where((lane >= g * HEAD_DIM) & (lane < (g + 1) * HEAD_DIM), vc, 0.0).astype(BF16)
            mixed = mixed + _dot(wm_ref[g], vg)
        oc_ref[pl.ds(c * GMLP_CHUNK, GMLP_CHUNK), :] = (u[c * GMLP_CHUNK:(c + 1) * GMLP_CHUNK] * mixed).astype(BF16)


def _in_proj_call(x, sc, sh, w_all, cw, cb, cpast, lng, lnb, wm, sb, *, shift, tm, transposed, name):
    rows, d_model = x.shape
    pad = max(8, 2 * shift)
    mr = sc.shape[0]
    mod_spec = (pl.BlockSpec((1, d_model), lambda i: (0, 0)) if mr == 1
                else pl.BlockSpec((tm, d_model), lambda i: (i, 0)))

    def row_spec(n):
        return pl.BlockSpec((tm, n), lambda i: (i, 0))

    def col_spec(n):
        return pl.BlockSpec((n, tm), lambda i: (0, i))

    def full(a):
        nd = a.ndim
        return pl.BlockSpec(a.shape, lambda i: (0,) * nd)

    sds = jax.ShapeDtypeStruct
    out_shape = [sds((rows, 256), BF16), sds((rows, 256), BF16), sds((2 * shift, 256), F32)]
    out_specs = [row_spec(256), row_spec(256), pl.BlockSpec((2 * shift, 256), lambda i: (0, 0))]
    if transposed:
        out_shape += [sds((N_HEADS * LANES, rows), BF16), sds((rows // PAGE_SIZE, 4 * LANES, PAGE_SIZE), F32),
                      sds((2 * LANES, rows), F32)]
        out_specs += [col_spec(N_HEADS * LANES),
                      pl.BlockSpec((tm // PAGE_SIZE, 4 * LANES, PAGE_SIZE), lambda i: (i, 0, 0)), col_spec(2 * LANES)]
        out_shape += [sds((rows, LANES), BF16)] * 4 + [sds((LANES, rows), BF16)] * 2 + [sds((LANES, rows), F32)]
        out_specs += [row_spec(LANES)] * 4 + [col_spec(LANES)] * 3
    else:
        out_shape += [sds((rows, 1024), BF16), sds((rows, _N_SLOT * LANES), F32), sds((rows, LANES), BF16),
                      sds((rows, LANES), BF16), sds((rows, LANES), F32), sds((rows, 256), F32)]
        out_specs += [row_spec(1024), row_spec(_N_SLOT * LANES), row_spec(LANES), row_spec(LANES), row_spec(LANES),
                      row_spec(256)]
    return pl.pallas_call(
        functools.partial(_in_proj_kernel, shift=shift, tm=tm, pad=pad, transposed=transposed),
        grid=(rows // tm,),
        in_specs=[row_spec(d_model), mod_spec, mod_spec, full(w_all), full(cw), full(cb), full(cpast),
                  full(lng), full(lnb), full(wm), full(sb)],
        out_specs=out_specs,
        out_shape=out_shape,
        scratch_shapes=[pltpu.VMEM((pad + tm, 256), F32)],
        compiler_params=pltpu.CompilerParams(dimension_semantics=("arbitrary",), vmem_limit_bytes=VMEM_LIMIT),
        name=name,
    )(x, sc, sh, w_all, cw, cb, cpast, lng, lnb, wm, sb)


def _pe_term_kernel(pe_ref, w1_ref, o_ref):
    for s in range(2):
        o_ref[s] = _dot3(pe_ref[s], w1_ref[s])


def _pe_term_call(pe_flat, w1_flat):
    return pl.pallas_call(
        _pe_term_kernel,
        out_shape=jax.ShapeDtypeStruct((2, 8, CMP_HID), F32),
        compiler_params=pltpu.CompilerParams(vmem_limit_bytes=VMEM_LIMIT),
        name="cmp_pe_term",
    )(pe_flat, w1_flat)


def _compress_prompt_kernel(kx_ref, vx_ref, w1_ref, pet_ref, w2_ref, kc_ref, vct_ref, sh_ref, *, nc):
    sh_ref[pl.ds(0, 8), :] = jnp.zeros((8, CMP_HID), F32)
    for s, x_ref in enumerate((kx_ref, vx_ref)):
        parts = _dot(x_ref[...], w1_ref[s])
        acc = jnp.zeros((nc, LANES), F32)
        for g in range(N_KV_HEADS):
            p0 = parts[:, g * 256:g * 256 + CMP_HID]
            p1 = parts[:, g * 256 + CMP_HID:(g + 1) * 256]
            sh_ref[pl.ds(8, nc), :] = p0
            p0s = sh_ref[pl.ds(7, nc), :]
            hid = _gelu(p0s + p1 + pet_ref[s][0:1, :])
            acc = acc + _dot(hid.astype(BF16), w2_ref[s, g])
        if s == 0:
            kc_ref[...] = acc.astype(BF16)
        else:
            vct_ref[...] = acc.T.astype(BF16)


def _compress_prompt_call(kx, vx, w1p, pet, w2p):
    nc = kx.shape[0]
    return pl.pallas_call(
        functools.partial(_compress_prompt_kernel, nc=nc),
        out_shape=[jax.ShapeDtypeStruct((nc, LANES), BF16), jax.ShapeDtypeStruct((LANES, nc), BF16)],
        scratch_shapes=[pltpu.VMEM((nc + 8, CMP_HID), F32)],
        compiler_params=pltpu.CompilerParams(vmem_limit_bytes=VMEM_LIMIT),
        name="compress_prompt",
    )(kx, vx, w1p, pet, w2p)


def _attn_prompt_kernel(qt_ref, gatet_ref, kc_ref, vct_ref, covert_ref, ksl_ref, vslt_ref, kwn_ref, vwnt_ref, eb_ref,
                        o_ref, ns_ref, s_ref, m_ref, ocmp_ref, *, nc, ns):
    qb = pl.program_id(0)
    s0 = qb * Q_BLOCK
    ncol = GQA * Q_BLOCK
    pair = 2 * Q_BLOCK
    col = lax.broadcasted_iota(jnp.int32, (1, ncol), 1)
    qpos_c = s0 + (col & (Q_BLOCK - 1))
    qp = s0 + lax.broadcasted_iota(jnp.int32, (1, Q_BLOCK), 1)
    m_idx = lax.broadcasted_iota(jnp.int32, (nc, 1), 0)
    blk = lax.broadcasted_iota(jnp.int32, (ns, 1), 0)
    blk_f = blk.astype(F32)
    key_row = lax.broadcasted_iota(jnp.int32, (KEY_TILE, 1), 0)
    n_tiles = s0 // KEY_TILE + 1

    ns_ref[pl.ds(ns, 8), :] = jnp.zeros((8, Q_BLOCK), F32)
    rhs_zero = jnp.zeros((LANES - NS_ROWS, ncol), BF16)

    for g in (pl.program_id(1),):
        rq = jnp.concatenate([qt_ref[pl.ds(h * LANES, LANES), :] for h in range(GQA)], axis=1)

        vrows = pl.ds(pl.multiple_of(g * HEAD_DIM, HEAD_DIM), HEAD_DIM)

        def compressed_and_select(frac):
            nr, nsr = nc * frac // CAUSAL_VARIANTS, ns * frac // CAUSAL_VARIANTS

            def run():
                m_i = m_idx[0:nr]
                last_pos = jnp.where(m_i >= 1, CMP_STRIDE * (m_i - 1) + CMP_LEN - 1, 2 ** 30)
                sc = jnp.where(last_pos <= qpos_c, _dot(kc_ref[pl.ds(0, nr), :], rq), NEG_INF)
                p = _softmax_rows(sc, qpos_c >= CMP_LEN - 1)
                o_c = _dot(vct_ref[vrows, pl.ds(0, nr)], p.astype(BF16))
                p4 = (p[:, 0:Q_BLOCK] + p[:, Q_BLOCK:2 * Q_BLOCK] + p[:, 2 * Q_BLOCK:3 * Q_BLOCK]
                      + p[:, 3 * Q_BLOCK:4 * Q_BLOCK])
                hi, lo = _split(p4)
                cov = covert_ref[pl.ds(0, nsr), pl.ds(0, nr)]
                imp = _dot(cov, hi) + _dot(cov, lo)
                b_i, b_f = blk[0:nsr], blk_f[0:nsr]
                qblk = qp // SLC_BLOCK
                elig = b_i * SLC_BLOCK <= qp
                forced = (b_i == 0) | (b_i == qblk) | (b_i == qblk - 1)
                val = jnp.where(elig, jnp.where(forced, REMOVED, imp), -FORCE)
                for _ in range(N_SELECT - 3):
                    mx = jnp.max(val, axis=0, keepdims=True)
                    first = jnp.min(jnp.where(val == mx, b_f, float(ns)), axis=0, keepdims=True)
                    val = jnp.where(b_f == first, REMOVED, val)
                ns_ref[pl.ds(0, nsr), :] = jnp.where(elig & (val < 0.5 * REMOVED), 0.0, 1.0)
                if nsr < ns:
                    ns_ref[pl.ds(nsr, ns - nsr), :] = jnp.ones((ns - nsr, Q_BLOCK), F32)
                ocmp_ref[...] = o_c
            return run

        variant = (qb * CAUSAL_VARIANTS) // (nc * CMP_STRIDE // Q_BLOCK)
        for f in range(CAUSAL_VARIANTS):
            pl.when(variant == f)(compressed_and_select(f + 1))
        o_cmp = ocmp_ref[...]

        def tile_scores(kt):
            k0 = pl.multiple_of(kt * KEY_TILE, KEY_TILE)
            nsf = ns_ref[pl.ds(pl.multiple_of(kt * BLOCKS_PER_TILE, BLOCKS_PER_TILE), NS_ROWS), :].astype(BF16)
            rhs = jnp.concatenate([rq, jnp.concatenate([nsf] * GQA, axis=1), rhs_zero], axis=0)
            lhs = jnp.concatenate([ksl_ref[pl.ds(k0, KEY_TILE), :], eb_ref[...]], axis=1)
            return _dot(lhs, rhs)

        ones_rows = jnp.ones((NS_ROWS, KEY_TILE), BF16)

        def flash_update(s, s_max, k0, carry):
            m_run, acc = carry
            m_new = jnp.maximum(m_run, s_max)
            pe = jnp.exp2(s - m_new).astype(BF16)
            v_aug = jnp.concatenate([vslt_ref[vrows, pl.ds(k0, KEY_TILE)], ones_rows], axis=0)
            return m_new, jnp.exp2(m_run - m_new) * acc + _dot(v_aug, pe)

        def produce(dst_ref, dmx_ref, kt):
            s = tile_scores(kt)
            dst_ref[...] = s
            dmx_ref[...] = jnp.max(s, axis=0, keepdims=True)

        def update_from(src_ref, smx_ref, kt, carry):
            k0 = pl.multiple_of(kt * KEY_TILE, KEY_TILE)
            return tuple(flash_update(src_ref[:, pl.ds(hp * pair, pair)], smx_ref[:, pl.ds(hp * pair, pair)], k0,
                                      carry[hp]) for hp in range(GQA // 2))

        sa_ref, ma_ref = s_ref.at[0], m_ref.at[0]
        produce(sa_ref, ma_ref, 0)

        def chain(kt, n, carry):
            for i in range(n):
                produce(s_ref.at[(i + 1) % n], m_ref.at[(i + 1) % n], kt + i + 1)
                carry = update_from(s_ref.at[i], m_ref.at[i], kt + i, carry)
            return carry

        init = (jnp.full((1, pair), NEG_INF, F32), jnp.zeros((HEAD_DIM + NS_ROWS, pair), F32))
        last = n_tiles - 1
        quads = last // TILE_UNROLL
        carry = lax.fori_loop(0, quads, lambda j, c: chain(TILE_UNROLL * j, TILE_UNROLL, c), (init,) * (GQA // 2))
        done = TILE_UNROLL * quads
        n = TILE_UNROLL // 2
        while n >= 2:
            take = last - done >= n
            carry = lax.cond(take, lambda c, done=done, n=n: chain(done, n, c), lambda c: c, carry)
            done = done + jnp.where(take, n, 0)
            n //= 2

        def odd_step(carry):
            carry = update_from(sa_ref, ma_ref, last - 1, carry)
            produce(sa_ref, ma_ref, last)
            return carry

        carry = lax.cond(lax.rem(last, 2) == 1, odd_step, lambda c: c, carry)
        k0 = pl.multiple_of(last * KEY_TILE, KEY_TILE)
        o_slc_parts = []
        for hp in range(GQA // 2):
            s = sa_ref[:, pl.ds(hp * pair, pair)]
            s = jnp.where(k0 + key_row <= qpos_c[:, hp * pair:(hp + 1) * pair], s, NEG_INF)
            _, acc = flash_update(s, jnp.max(s, axis=0, keepdims=True), k0, carry[hp])
            o_slc_parts.append(acc[0:HEAD_DIM] * (1.0 / acc[HEAD_DIM:HEAD_DIM + 1]))
        o_slc = jnp.concatenate(o_slc_parts, axis=1)

        w0 = pl.multiple_of(jnp.maximum(s0 - WINDOW, 0), Q_BLOCK)
        dist = qp - (w0 + lax.broadcasted_iota(jnp.int32, (WINDOW + Q_BLOCK, 1), 0))
        wbias = jnp.where(lax.bitcast_convert_type(dist, jnp.uint32) <= WINDOW, 0.0, NEG_INF)
        sw = _dot(kwn_ref[pl.ds(w0, WINDOW + Q_BLOCK), :], rq) + jnp.concatenate([wbias] * GQA, axis=1)
        pw = _softmax_rows(sw, None)
        o_win = _dot(vwnt_ref[vrows, pl.ds(w0, WINDOW + Q_BLOCK)], pw.astype(BF16))

        outs = []
        for h in range(GQA):
            c = slice(h * Q_BLOCK, (h + 1) * Q_BLOCK)
            gr = (GQA * g + h) * N_BRANCH
            outs.append(gatet_ref[pl.ds(gr, 1), :] * o_cmp[:, c] + gatet_ref[pl.ds(gr + 1, 1), :] * o_slc[:, c]
                        + gatet_ref[pl.ds(gr + 2, 1), :] * o_win[:, c])
        for hp in range(GQA // 2):
            o2 = jnp.concatenate(outs[2 * hp:2 * hp + 2], axis=0)
            o_ref[:, hp * LANES:(hp + 1) * LANES] = o2.T.astype(BF16)


def _attn_prompt_call(qt, gatet, kc, vct, covert, ksl, vslt, kwn, vwnt, ebias):
    t = qt.shape[1]
    ns, nc = covert.shape
    vmem = pl.BlockSpec(memory_space=pltpu.VMEM)
    return pl.pallas_call(
        functools.partial(_attn_prompt_kernel, nc=nc, ns=ns),
        grid=(t // Q_BLOCK, N_KV_HEADS),
        in_specs=[pl.BlockSpec((GQA * LANES, Q_BLOCK), lambda i, g: (g, i)),
                  pl.BlockSpec((LANES, Q_BLOCK), lambda i, g: (0, i)),
                  vmem, vmem, vmem, vmem, vmem, vmem, vmem, vmem],
        out_specs=pl.BlockSpec((Q_BLOCK, GQA * HEAD_DIM), lambda i, g: (i, g)),
        out_shape=jax.ShapeDtypeStruct((t, N_HEADS * HEAD_DIM), BF16),
        scratch_shapes=[pltpu.VMEM((ns + 8, Q_BLOCK), F32),
                        pltpu.VMEM((TILE_UNROLL, KEY_TILE, GQA * Q_BLOCK), F32),
                        pltpu.VMEM((TILE_UNROLL, 1, GQA * Q_BLOCK), F32),
                        pltpu.VMEM((HEAD_DIM, GQA * Q_BLOCK), F32)],
        compiler_params=pltpu.CompilerParams(dimension_semantics=("arbitrary", "arbitrary"),
                                             vmem_limit_bytes=VMEM_LIMIT),
        name="attn_prompt",
    )(qt, gatet, kc, vct, covert, ksl, vslt, kwn, vwnt, ebias)


def _out_ffn_kernel(x_ref, oa_ref, ob_ref, oc_ref, g1_ref, sc2_ref, sh2_ref, g2_ref, wo_ref, lng_ref, lnb_ref,
                    wup_ref, cfw_ref, cfb_ref, fpast_ref, wdn_ref, xo_ref, fstate_ref, us_ref,
                    *, shift, tm, pad, alpha, d_ff):
    i = pl.program_id(0)

    @pl.when(i == 0)
    def _():
        us_ref[pl.ds(pad - 2 * shift, 2 * shift), :] = fpast_ref[...]

    mix = _dot(jnp.concatenate([oa_ref[...], ob_ref[...], oc_ref[...]], axis=1), wo_ref[...])
    x1 = _layer_norm(alpha * x_ref[...] + (1.0 + g1_ref[...]) * mix, lng_ref[0:1, :], lnb_ref[0:1, :])
    h2 = (x1 * (1.0 + sc2_ref[...]) + sh2_ref[...]).astype(BF16)
    up = _dot(h2, wup_ref[...])
    ua = up[:, :d_ff]
    us_ref[pl.ds(pad, tm), :] = ua
    yc = _shifted_conv(us_ref, ua, cfw_ref, cfb_ref, pad, shift, tm)
    tail = us_ref[pl.ds(pad + tm - 2 * shift, 2 * shift), :]
    fstate_ref[...] = tail
    us_ref[pl.ds(pad - 2 * shift, 2 * shift), :] = tail
    act = (yc * _sigmoid(yc) * up[:, d_ff:]).astype(BF16)
    y = _dot(act, wdn_ref[...])
    xo_ref[...] = _layer_norm(alpha * x1 + (1.0 + g2_ref[...]) * y, lng_ref[1:2, :], lnb_ref[1:2, :])


def _out_ffn_call(x, oa, ob, oc, g1, sc2, sh2, g2, wo, lng, lnb, wup, cfw, cfb, fpast, wdn, *, shift, tm, alpha, name):
    rows, d_model = x.shape
    d_ff = wdn.shape[0]
    pad = max(8, 2 * shift)
    mr = g1.shape[0]
    mod_spec = (pl.BlockSpec((1, d_model), lambda i: (0, 0)) if mr == 1
                else pl.BlockSpec((tm, d_model), lambda i: (i, 0)))

    def row_spec(n):
        return pl.BlockSpec((tm, n), lambda i: (i, 0))

    vmem = pl.BlockSpec(memory_space=pltpu.VMEM)
    return pl.pallas_call(
        functools.partial(_out_ffn_kernel, shift=shift, tm=tm, pad=pad, alpha=alpha, d_ff=d_ff),
        grid=(rows // tm,),
        in_specs=[row_spec(d_model), row_spec(256), row_spec(N_HEADS * HEAD_DIM), row_spec(256), mod_spec, mod_spec, mod_spec,
                  mod_spec, vmem, vmem, vmem, vmem, vmem, vmem, vmem, vmem],
        out_specs=[row_spec(d_model), pl.BlockSpec((2 * shift, d_ff), lambda i: (0, 0))],
        out_shape=[jax.ShapeDtypeStruct((rows, d_model), F32), jax.ShapeDtypeStruct((2 * shift, d_ff), F32)],
        scratch_shapes=[pltpu.VMEM((pad + tm, d_ff), F32)],
        compiler_params=pltpu.CompilerParams(dimension_semantics=("arbitrary",), vmem_limit_bytes=VMEM_LIMIT),
        name=name,
    )(x, oa, ob, oc, g1, sc2, sh2, g2, wo, lng, lnb, wup, cfw, cfb, fpast, wdn)


def _cmp_stream_kernel(pt_ref, cache_ref, perm_ref, w1_ref, pet_ref, w2_ref, kc_ref, vc_ref, buf, rbuf, sem, carry,
                       *, page_base, n_pages, pg, n_groups, total):
    b = pl.program_id(0)
    gi = pl.program_id(1)
    step = b * n_groups + gi
    slot = lax.rem(step, 2)
    m = pg * (PAGE_SIZE // CMP_STRIDE)

    def page_copy(page, slt, i):
        return pltpu.make_async_copy(cache_ref.at[page, pl.ds(0, 2 * LANES), :], buf.at[slt, i], sem.at[slt])

    def issue(stp, slt):
        base = lax.div(stp, n_groups) * n_pages + lax.rem(stp, n_groups) * pg
        for i in range(pg):
            page_copy(page_base + pt_ref[base + i], slt, i).start()

    @pl.when(step == 0)
    def _():
        carry[...] = jnp.zeros(carry.shape, F32)
        issue(step, slot)

    @pl.when(step + 1 < total)
    def _():
        issue(step + 1, 1 - slot)

    for i in range(pg):
        page_copy(0, slot, i).wait()

    per_chunk = PAGE_SIZE // CMP_STRIDE
    for i in range(pg):
        rbuf[i] = _dot_nt(perm_ref[...], buf[slot, i].astype(BF16))

    def rows_of(j, s):
        return jnp.concatenate([rbuf[i, pl.ds(j * per_chunk, per_chunk), pl.ds(s * LANES, LANES)]
                                for i in range(pg)], axis=0)

    lane = lax.broadcasted_iota(jnp.int32, (1, LANES), 1)
    first_half = lane < HEAD_DIM
    row0 = lax.broadcasted_iota(jnp.int32, (m, 1), 0) == 0
    pieces = [[[], []], [[], []]]
    for pr in range(CMP_STRIDE // 2):
        for s in range(2):
            a = rows_of(2 * pr, s)
            bb = rows_of(2 * pr + 1, s)
            pieces[s][0].append(jnp.where(first_half, a, pltpu.roll(bb, HEAD_DIM, 1)).astype(BF16))
            pieces[s][1].append(jnp.where(first_half, pltpu.roll(a, HEAD_DIM, 1), bb).astype(BF16))
    for s, o_ref in enumerate((kc_ref, vc_ref)):
        acc = jnp.zeros((m, LANES), F32)
        for g in range(N_KV_HEADS):
            parts = _dot(jnp.concatenate(pieces[s][g], axis=1), w1_ref[s])
            p0 = parts[:, :CMP_HID]
            p1 = parts[:, CMP_HID:]
            prev = jnp.where(gi == 0, 0.0, carry[s * 2 + g][0:1, :])
            p0s = jnp.where(row0, prev, pltpu.roll(p0, 1, 0))
            carry[s * 2 + g] = jnp.broadcast_to(p0[m - 1:m, :], (8, CMP_HID))
            hid = _gelu(p0s + p1 + pet_ref[s][0:1, :])
            acc = acc + _dot(hid.astype(BF16), w2_ref[s, g])
        o_ref[0] = acc.astype(BF16)


def _cmp_stream_call(pt_flat, cache_t, w1s, pet, w2p, *, layer, n_phys, batch, n_pages, pg):
    n_groups = n_pages // pg
    m = pg * (PAGE_SIZE // CMP_STRIDE)
    nc = n_pages * (PAGE_SIZE // CMP_STRIDE)
    total = batch * n_groups
    out_row = np.arange(PAGE_SIZE)[:, None]
    src_row = (out_row % (PAGE_SIZE // CMP_STRIDE)) * CMP_STRIDE + out_row // (PAGE_SIZE // CMP_STRIDE)
    perm = jnp.asarray((np.arange(PAGE_SIZE)[None, :] == src_row).astype(np.float32), dtype=BF16)

    def full(a):
        nd = a.ndim
        return pl.BlockSpec(a.shape, lambda b, g, pt: (0,) * nd)

    grid_spec = pltpu.PrefetchScalarGridSpec(
        num_scalar_prefetch=1,
        grid=(batch, n_groups),
        in_specs=[pl.BlockSpec(memory_space=pl.ANY), full(perm), full(w1s), full(pet), full(w2p)],
        out_specs=[pl.BlockSpec((1, m, LANES), lambda b, g, pt: (b, g, 0))] * 2,
        scratch_shapes=[pltpu.VMEM((2, pg, 2 * LANES, PAGE_SIZE), F32), pltpu.VMEM((pg, PAGE_SIZE, 2 * LANES), F32),
                        pltpu.SemaphoreType.DMA((2,)), pltpu.VMEM((4, 8, CMP_HID), F32)],
    )
    return pl.pallas_call(
        functools.partial(_cmp_stream_kernel, page_base=layer * n_phys, n_pages=n_pages, pg=pg,
                          n_groups=n_groups, total=total),
        grid_spec=grid_spec,
        out_shape=[jax.ShapeDtypeStruct((batch, nc, LANES), BF16)] * 2,
        compiler_params=pltpu.CompilerParams(dimension_semantics=("arbitrary", "arbitrary"),
                                             vmem_limit_bytes=VMEM_LIMIT),
        name="cmp_stream_sample",
    )(pt_flat, cache_t, perm, w1s, pet, w2p)


def _cmp_attn_sample_kernel(q_ref, kc_ref, vc_ref, cover_ref, ocmp_ref, idx_ref, bias_ref,
                            *, past, nc, ns, nsp, n_cache_blocks, bb):
    rows = lax.broadcasted_iota(jnp.int32, (32, 1), 0)
    qpos_r = past + (rows & 7)
    m_idx = lax.broadcasted_iota(jnp.int32, (1, nc), 1)
    blk = lax.broadcasted_iota(jnp.int32, (1, nsp), 1)
    blk_f = blk.astype(F32)
    lane = lax.broadcasted_iota(jnp.int32, (1, LANES), 1)
    cmask = (m_idx >= 1) & (CMP_STRIDE * (m_idx - 1) + CMP_LEN - 1 <= qpos_r)
    imps = []
    for bi in range(bb):
        for g in range(N_KV_HEADS):
            p = _masked_softmax(_dot_nt(q_ref[bi, g], kc_ref[bi]), cmask, -1)
            ocmp_ref[bi, g] = _dot(p.astype(BF16), vc_ref[bi])
            p4 = p[0:8] + p[8:16] + p[16:24] + p[24:32]
            hi, lo = _split(p4)
            imps.append(_dot(hi, cover_ref[...]) + _dot(lo, cover_ref[...]))
    imp = jnp.concatenate(imps, axis=0)
    qp = past + (lax.broadcasted_iota(jnp.int32, (bb * N_KV_HEADS * 8, 1), 0) & 7)
    qblk = qp // SLC_BLOCK
    elig = blk * SLC_BLOCK <= qp
    forced = (blk == 0) | (blk == qblk) | (blk == qblk - 1)
    val = jnp.where(blk < ns, jnp.where(elig, jnp.where(forced, FORCE, imp), -FORCE), REMOVED)
    _, firsts, tops = _top_select(val, blk_f, nsp, -1)
    idx = jnp.zeros((bb * N_KV_HEADS * 8, LANES), F32)
    bias = jnp.zeros((bb * N_KV_HEADS * 8, LANES), F32)
    for t in range(N_SELECT):
        ok = (tops[t] > -0.5 * FORCE) & (firsts[t] < float(n_cache_blocks))
        odd = firsts[t] - 2.0 * jnp.floor(firsts[t] * 0.5)
        idx = jnp.where(lane == t, firsts[t], idx)
        for hf in range(2):
            bias = jnp.where(lane == 2 * t + hf, jnp.where(ok & (odd == float(hf)), 0.0, NEG_INF), bias)
    idx = idx.astype(jnp.int32)
    for bi in range(bb):
        for g in range(N_KV_HEADS):
            r = (bi * N_KV_HEADS + g) * 8
            idx_ref[bi, g] = idx[r:r + 8]
            bias_ref[bi, g] = bias[r:r + 8]


def _cmp_attn_sample_call(q_hq, kc, vc, cover, *, past, ns, n_cache_blocks):
    batch, nc, _ = kc.shape
    nsp = cover.shape[1]
    bb = math.gcd(batch, SEQS_PER_SELECT_STEP)
    blk4 = lambda r: pl.BlockSpec((bb, N_KV_HEADS, r, LANES), lambda b: (b, 0, 0, 0))
    return pl.pallas_call(
        functools.partial(_cmp_attn_sample_kernel, past=past, nc=nc, ns=ns, nsp=nsp, n_cache_blocks=n_cache_blocks,
                          bb=bb),
        grid=(batch // bb,),
        in_specs=[blk4(32), pl.BlockSpec((bb, nc, LANES), lambda b: (b, 0, 0)),
                  pl.BlockSpec((bb, nc, LANES), lambda b: (b, 0, 0)), pl.BlockSpec(cover.shape, lambda b: (0, 0))],
        out_specs=[blk4(32), blk4(8), blk4(8)],
        out_shape=[jax.ShapeDtypeStruct((batch, N_KV_HEADS, 32, LANES), F32),
                   jax.ShapeDtypeStruct((batch, N_KV_HEADS, 8, LANES), jnp.int32),
                   jax.ShapeDtypeStruct((batch, N_KV_HEADS, 8, LANES), F32)],
        compiler_params=pltpu.CompilerParams(dimension_semantics=("arbitrary",), vmem_limit_bytes=VMEM_LIMIT),
        name="cmp_attn_sample",
    )(q_hq, kc, vc, cover)


def _sel_attn_sample_kernel(idx_ref, pt_ref, cache_ref, q_ref, bias_ref, ex_ref, knew_ref, vnew_ref, wint_ref,
                            wnew_ref, wnewt_ref, ocmp_ref, gate_ref, o_ref, wout_ref, buf, sem,
                            *, page_base, n_pages, tq, n_cache_blocks, total):
    b = pl.program_id(0)
    g = pl.program_id(1)
    step = b * N_KV_HEADS + g
    slot = lax.rem(step, 2)
    n_ent = tq * N_SELECT
    blocks_per_page = PAGE_SIZE // SLC_BLOCK

    def page_copy(page, grp, slt, e):
        rows = pl.ds(pl.multiple_of(grp * HEAD_DIM, HEAD_DIM), HEAD_DIM)
        return pltpu.make_async_copy(cache_ref.at[page, pl.ds(2, 2), grp], buf.at[slt, e, :, rows, :], sem.at[slt])

    def issue(stp, slt):
        bb = lax.div(stp, N_KV_HEADS)
        grp = lax.rem(stp, N_KV_HEADS)

        def body(e, _):
            blk = jnp.minimum(idx_ref[stp * n_ent + e], n_cache_blocks - 1)
            page = pt_ref[bb * n_pages + lax.div(blk, blocks_per_page)]
            page_copy(page_base + page, grp, slt, e).start()
            return 0
        lax.fori_loop(0, n_ent, body, 0)

    @pl.when(step == 0)
    def _():
        buf[...] = jnp.zeros(buf.shape, F32)
        issue(step, slot)

    @pl.when(step + 1 < total)
    def _():
        issue(step + 1, 1 - slot)

    def wait_body(e, _):
        page_copy(0, g, slot, e).wait()
        return 0
    lax.fori_loop(0, n_ent, wait_body, 0)

    lane = lax.broadcasted_iota(jnp.int32, (1, LANES), 1)
    col8 = lax.broadcasted_iota(jnp.int32, (1, 8), 1)
    wcol = lax.broadcasted_iota(jnp.int32, (1, WINDOW), 1)
    k_wt = wint_ref[0, pl.ds(0, LANES), :].astype(BF16)
    v_wt = wint_ref[0, pl.ds(LANES, LANES), :].astype(BF16)
    wnew = wnew_ref[0]
    k_wn = wnew[:, 0:LANES].astype(BF16)
    v_wn = wnew[:, LANES:2 * LANES].astype(BF16)
    bias_all = _dot(bias_ref[0, 0].astype(BF16), ex_ref[...])
    keep = (lane >= g * HEAD_DIM) & (lane < (g + 1) * HEAD_DIM)
    for qi in range(tq):
        q4 = q_ref[0, 0, qi]
        new_ok = (col8 <= qi) & (col8 < tq)
        k_t = jnp.concatenate([buf[slot, qi * N_SELECT + k, 0] for k in range(N_SELECT)], axis=1)
        v_t = jnp.concatenate([buf[slot, qi * N_SELECT + k, 1] for k in range(N_SELECT)], axis=1)
        s = _dot(q4, k_t.astype(BF16)) + bias_all[qi:qi + 1, :]
        s_n = jnp.where(new_ok, _dot_nt(q4, knew_ref[0]), NEG_INF)
        mx = jnp.maximum(jnp.max(s, axis=-1, keepdims=True), jnp.max(s_n, axis=-1, keepdims=True))
        pe = jnp.exp2(s - mx)
        pn = jnp.exp2(s_n - mx)
        l = jnp.sum(pe, axis=-1, keepdims=True) + jnp.sum(pn, axis=-1, keepdims=True)
        o_slc = (_dot_nt(pe.astype(BF16), v_t.astype(BF16)) + _dot(pn.astype(BF16), vnew_ref[0])) * (1.0 / l)
        sw = jnp.where(wcol >= qi, _dot(q4, k_wt), NEG_INF)
        sw_n = jnp.where(new_ok, _dot_nt(q4, k_wn), NEG_INF)
        mw = jnp.maximum(jnp.max(sw, axis=-1, keepdims=True), jnp.max(sw_n, axis=-1, keepdims=True))
        pw = jnp.exp2(sw - mw)
        pwn = jnp.exp2(sw_n - mw)
        lw = jnp.sum(pw, axis=-1, keepdims=True) + jnp.sum(pwn, axis=-1, keepdims=True)
        o_win = (_dot_nt(pw.astype(BF16), v_wt) + _dot(pwn.astype(BF16), v_wn)) * (1.0 / lw)
        gt = gate_ref[0, 0, qi]
        o = gt[:, 0:1] * ocmp_ref[0, 0, qi] + gt[:, 1:2] * o_slc + gt[:, 2:3] * o_win
        o_ref[0, 0, qi] = jnp.where(keep, o, 0.0)

    @pl.when(g == 0)
    def _():
        shifted = pltpu.roll(wint_ref[0], WINDOW - tq, 1)
        wout_ref[0, :, pl.ds(0, WINDOW - LANES)] = shifted[:, 0:WINDOW - LANES]
        wout_ref[0, :, pl.ds(WINDOW - LANES, LANES)] = jnp.where(lane >= LANES - tq, wnewt_ref[0],
                                                                  shifted[:, WINDOW - LANES:WINDOW])


def _sel_attn_sample_call(idx_flat, pt_flat, cache_t, q_qh, bias, expand, knew, vnew, win_t, wnew, wnew_t, ocmp_qh,
                          gate_qh, *, layer, n_phys, n_pages, tq, n_cache_blocks):
    batch = q_qh.shape[0]
    n_ent = tq * N_SELECT
    b5 = lambda: pl.BlockSpec((1, 1, tq, 8, LANES), lambda b, g, i, p: (b, g, 0, 0, 0))
    b3 = lambda r, c: pl.BlockSpec((1, r, c), lambda b, g, i, p: (b, 0, 0))
    grid_spec = pltpu.PrefetchScalarGridSpec(
        num_scalar_prefetch=2,
        grid=(batch, N_KV_HEADS),
        in_specs=[pl.BlockSpec(memory_space=pl.ANY), b5(),
                  pl.BlockSpec((1, 1, 8, LANES), lambda b, g, i, p: (b, g, 0, 0)),
                  pl.BlockSpec(expand.shape, lambda b, g, i, p: (0, 0)),
                  b3(8, LANES), b3(8, LANES),
                  pl.BlockSpec((1, 2 * LANES, WINDOW), lambda b, g, i, p: (layer * batch + b, 0, 0)),
                  b3(8, 2 * LANES), b3(2 * LANES, LANES),
                  b5(), b5()],
        out_specs=[b5(), b3(2 * LANES, WINDOW)],
        scratch_shapes=[pltpu.VMEM((2, n_ent, 2, LANES, PAGE_SIZE), F32), pltpu.SemaphoreType.DMA((2,))],
    )
    return pl.pallas_call(
        functools.partial(_sel_attn_sample_kernel, page_base=layer * n_phys, n_pages=n_pages, tq=tq,
                          n_cache_blocks=n_cache_blocks, total=batch * N_KV_HEADS),
        grid_spec=grid_spec,
        out_shape=[jax.ShapeDtypeStruct((batch, N_KV_HEADS, tq, 8, LANES), F32),
                   jax.ShapeDtypeStruct((batch, 2 * LANES, WINDOW), F32)],
        compiler_params=pltpu.CompilerParams(dimension_semantics=("arbitrary", "arbitrary"),
                                             vmem_limit_bytes=VMEM_LIMIT),
        name="sel_attn_sample",
    )(idx_flat, pt_flat, cache_t, q_qh, bias, expand, knew, vnew, win_t, wnew, wnew_t, ocmp_qh, gate_qh)


def _prep_w_in(w):
    d = w.shape[0]
    wq = (w[:, 768:1280] * (HEAD_DIM ** -0.5 * LOG2E)).reshape(d, N_HEADS, HEAD_DIM)
    z = jnp.zeros_like(wq)
    grp = (jnp.arange(N_HEADS) // GQA)[None, :, None]
    wq = jnp.concatenate([jnp.where(grp == 0, wq, z), jnp.where(grp == 1, wq, z)], axis=-1).reshape(d, N_HEADS * LANES)
    gate = jnp.pad(w[:, 2048:2072], ((0, 0), (0, LANES - N_HEADS * N_BRANCH)))
    return jnp.concatenate([w[:, :768], wq, w[:, 1280:2048], gate, w[:, 2072:]], axis=1).astype(BF16)


def _prep_w1(w1):
    w = w1.reshape(2, 2, CMP_STRIDE, HEAD_DIM, CMP_HID)
    return jnp.transpose(w, (0, 2, 3, 1, 4)).reshape(2, CMP_STRIDE * HEAD_DIM, 2 * CMP_HID)


def _prep_w1_grouped(w1s):
    w = w1s.reshape(2, CMP_STRIDE, 1, HEAD_DIM, 1, 2 * CMP_HID)
    eye = jnp.eye(N_KV_HEADS, dtype=w.dtype).reshape(1, 1, N_KV_HEADS, 1, N_KV_HEADS, 1)
    return (w * eye).reshape(2, CMP_STRIDE * N_KV_HEADS * HEAD_DIM, N_KV_HEADS * 2 * CMP_HID)


def _prep_w2(w2):
    z = jnp.zeros_like(w2)
    return jnp.stack([jnp.concatenate([w2, z], axis=-1), jnp.concatenate([z, w2], axis=-1)], axis=1).astype(BF16)


def _cover_matrix(nc, ns_real, ns_pad):
    m = np.arange(nc)[:, None]
    b = np.arange(ns_pad)[None, :]
    return ((m >= 4 * b) & (m <= 4 * b + 4) & (m >= 1) & (b < ns_real)).astype(np.float32)


def _block_bias_matrix():
    k = np.arange(KEY_TILE)[:, None]
    b = np.arange(LANES)[None, :]
    return jnp.asarray(np.where(k // SLC_BLOCK == b, NEG_INF, 0.0).astype(np.float32), dtype=BF16)


def _expand_matrix():
    r = np.arange(LANES)[:, None]
    c = np.arange(N_SELECT * PAGE_SIZE)[None, :]
    return jnp.asarray((c // SLC_BLOCK == r).astype(np.float32), dtype=BF16)


def kernel(x_prompt, x_sample, cache_nsa_kv, state_win_kv, state_conv, state_ffn_conv, page_table, c_prompt, c_sample, w_ada, b_ada, w_in, conv_a_w, conv_a_b, cmp_pe, cmp_w1, cmp_w2, sgu_ln_g, sgu_ln_b, sgu_w, sgu_b, w_o, ln_g, ln_b, w_ffn_up, conv_f_w, conv_f_b, w_ffn_down):
    depth = w_in.shape[0]
    _, t, d_model = x_prompt.shape
    nb, tq, _ = x_sample.shape
    n_phys = cache_nsa_kv.shape[1]
    n_pages = page_table.shape[1]
    past = n_pages * PAGE_SIZE
    d_ff = w_ffn_down.shape[1]
    alpha = (2 * depth) ** 0.25
    rs = nb * tq
    kvw = 4 * N_KV_HEADS * HEAD_DIM
    assert x_prompt.shape[0] == 1 and c_prompt.shape[0] == 1
    assert d_model == 1024 and t % KEY_TILE == 0 and t >= WINDOW + Q_BLOCK
    assert tq == 4 and rs == GMLP_CHUNK and past % KEY_TILE == 0 and past >= WINDOW
    assert state_win_kv.shape[2] == WINDOW

    rc = -(-(1 + nb) // 8) * 8
    c_all = jnp.pad(jnp.concatenate([c_prompt, c_sample], axis=0), ((0, rc - 1 - nb), (0, 0)))
    mods = _ada_call(c_all, w_ada, b_ada)

    nc_p, ns_p = t // CMP_STRIDE, t // SLC_BLOCK
    covert_p = jnp.asarray(_cover_matrix(nc_p, ns_p, ns_p).T, dtype=BF16)
    nc_s = (past + tq) // CMP_STRIDE
    ns_s = -(-(past + tq) // SLC_BLOCK)
    ns_s_pad = -(-ns_s // LANES) * LANES
    n_cache_blocks = past // SLC_BLOCK
    cover_s = jnp.asarray(_cover_matrix(nc_s, ns_s, ns_s_pad), dtype=BF16)
    ebias = _block_bias_matrix()
    expand = _expand_matrix()
    pt_flat = page_table.reshape(-1)
    cache_t = jnp.transpose(cache_nsa_kv.reshape(depth * n_phys, PAGE_SIZE, kvw), (0, 2, 1))
    win_t_all = jnp.transpose(state_win_kv.reshape(depth * nb, WINDOW, 2 * LANES), (0, 2, 1))

    xp = x_prompt[0]
    xs = jnp.transpose(x_sample, (1, 0, 2)).reshape(rs, d_model)
    tril_full = jnp.tril(jnp.ones((GMLP_CHUNK, GMLP_CHUNK), F32))
    tril_tq = jnp.tril(jnp.ones((tq, tq), F32))

    ps, ss = [], []
    for l in range(depth):
        w_all = _prep_w_in(w_in[l])
        wo_p = w_o[l].astype(BF16)
        wup = w_ffn_up[l].astype(BF16)
        wdn = w_ffn_down[l].astype(BF16)
        w1s = _prep_w1(cmp_w1[l])
        w1g = _prep_w1_grouped(w1s).astype(BF16)
        w1s = w1s.astype(BF16)
        w2p = _prep_w2(cmp_w2[l])
        pe_flat = jnp.broadcast_to(cmp_pe[l].reshape(2, 1, CMP_LEN * HEAD_DIM), (2, 8, CMP_LEN * HEAD_DIM))
        pet = _pe_term_call(pe_flat, cmp_w1[l].reshape(2, CMP_LEN * HEAD_DIM, CMP_HID))
        cw, cb = conv_a_w[l], conv_a_b[l].reshape(1, -1)
        cfw, cfb = conv_f_w[l], conv_f_b[l].reshape(1, -1)
        lng, lnb = sgu_ln_g[l].reshape(1, -1), sgu_ln_b[l].reshape(1, -1)

        def mod_rows(r0, r1, rep):
            parts = [mods[l, r0:r1, k * d_model:(k + 1) * d_model] for k in range(6)]
            return [jnp.tile(p_, (rep, 1)) if rep > 1 else p_ for p_ in parts]

        sh1, sc1, g1, sh2, sc2, g2 = mod_rows(0, 1, 1)
        wm_p = (sgu_w[l] * tril_full).astype(BF16)
        sb_p = jnp.repeat(sgu_b[l].T, HEAD_DIM, axis=1)
        (oa, oc, cst, qt, kvp, kvt, kcr, vcr, ksl, kwn, vslt, vwnt, gatet) = _in_proj_call(
            xp, sc1, sh1, w_all, cw, cb, jnp.zeros((2, 256), F32), lng, lnb, wm_p, sb_p,
            shift=1, tm=512, transposed=True, name="in_proj_prompt")
        kc, vct = _compress_prompt_call(kcr.reshape(nc_p, CMP_STRIDE * LANES), vcr.reshape(nc_p, CMP_STRIDE * LANES),
                                        w1g, pet, w2p)
        ob = _attn_prompt_call(qt, gatet, kc, vct, covert_p, ksl, vslt, kwn, vwnt, ebias)
        xp, fst = _out_ffn_call(xp, oa, ob, oc, g1, sc2, sh2, g2, wo_p, ln_g[l], ln_b[l], wup, cfw, cfb,
                                jnp.zeros((2, d_ff), F32), wdn, shift=1, tm=256, alpha=alpha, name="out_ffn_prompt")
        paged = kvp.reshape(t // PAGE_SIZE, 4, N_KV_HEADS, HEAD_DIM, PAGE_SIZE)
        winr = kvt[:, t - WINDOW:].reshape(2, N_KV_HEADS, HEAD_DIM, WINDOW)
        ps.append((jnp.transpose(paged, (0, 4, 1, 2, 3))[None], jnp.transpose(winr, (3, 0, 1, 2))[None],
                   cst[None], fst[None]))

        sh1, sc1, g1, sh2, sc2, g2 = mod_rows(1, 1 + nb, tq)
        eye_b = jnp.eye(nb, dtype=F32)
        wm_s = jax.vmap(lambda w: jnp.kron(w[:tq, :tq] * tril_tq, eye_b))(sgu_w[l]).astype(BF16)
        sb_s = jnp.repeat(jnp.repeat(sgu_b[l][:, :tq].T, nb, axis=0), HEAD_DIM, axis=1)
        cpast = jnp.transpose(state_conv[l], (1, 0, 2)).reshape(2 * nb, -1)
        (oa, oc, cst, q, kvf, ksl, vsl, gate, vrow) = _in_proj_call(
            xs, sc1, sh1, w_all, cw, cb, cpast, lng, lnb, wm_s, sb_s, shift=nb, tm=rs, transposed=False,
            name="in_proj_sample")
        kc, vc = _cmp_stream_call(pt_flat, cache_t, w1s, pet, w2p, layer=l, n_phys=n_phys, batch=nb,
                                  n_pages=n_pages, pg=min(32, n_pages))

        def by_batch(a):
            return jnp.transpose(a.reshape(tq, nb, -1), (1, 0, 2))

        qb5 = by_batch(q).reshape(nb, tq, N_KV_HEADS, GQA, LANES)
        q_hq = jnp.pad(jnp.transpose(qb5, (0, 2, 3, 1, 4)), ((0, 0), (0, 0), (0, 0), (0, 8 - tq), (0, 0)))
        q_hq = q_hq.reshape(nb, N_KV_HEADS, 32, LANES)
        q_qh = jnp.pad(jnp.transpose(qb5, (0, 2, 1, 3, 4)), ((0, 0), (0, 0), (0, 0), (0, 8 - GQA), (0, 0)))
        ocmp, idx, bias = _cmp_attn_sample_call(q_hq, kc, vc, cover_s, past=past, ns=ns_s,
                                                n_cache_blocks=n_cache_blocks)
        ocmp_qh = jnp.transpose(ocmp.reshape(nb, N_KV_HEADS, GQA, 8, LANES)[:, :, :, :tq], (0, 1, 3, 2, 4))
        ocmp_qh = jnp.pad(ocmp_qh, ((0, 0), (0, 0), (0, 0), (0, 8 - GQA), (0, 0)))
        g5 = by_batch(gate)[:, :, :N_HEADS * N_BRANCH].reshape(nb, tq, N_KV_HEADS, GQA, N_BRANCH)
        gate_qh = jnp.pad(jnp.transpose(g5, (0, 2, 1, 3, 4)),
                          ((0, 0), (0, 0), (0, 0), (0, 8 - GQA), (0, LANES - N_BRANCH)))
        pad8 = lambda a: jnp.pad(by_batch(a), ((0, 0), (0, 8 - tq), (0, 0)))
        wnew_rows = by_batch(kvf[:, kvw:])
        wnew_t = jnp.pad(jnp.transpose(wnew_rows, (0, 2, 1)), ((0, 0), (0, 0), (LANES - tq, 0)))
        o5, wout_t = _sel_attn_sample_call(
            idx[:, :, :tq, :N_SELECT].reshape(-1), pt_flat,
            cache_t.reshape(depth * n_phys, 4, N_KV_HEADS, HEAD_DIM, PAGE_SIZE), q_qh, bias, expand, pad8(ksl), pad8(vsl),
            win_t_all, jnp.pad(wnew_rows, ((0, 0), (0, 8 - tq), (0, 0))), wnew_t, ocmp_qh, gate_qh,
            layer=l, n_phys=n_phys, n_pages=n_pages, tq=tq, n_cache_blocks=n_cache_blocks)
        o5 = o5[:, :, :, :GQA, :HEAD_DIM] + o5[:, :, :, :GQA, HEAD_DIM:]
        ob = jnp.transpose(o5, (2, 0, 1, 3, 4)).reshape(rs, N_HEADS * HEAD_DIM).astype(BF16)
        fpast = jnp.transpose(state_ffn_conv[l], (1, 0, 2)).reshape(2 * nb, -1)
        xs, fst = _out_ffn_call(xs, oa, ob, oc, g1, sc2, sh2, g2, wo_p, ln_g[l], ln_b[l], wup, cfw, cfb, fpast, wdn,
                                shift=nb, tm=rs, alpha=alpha, name="out_ffn_sample")
        ss.append((by_batch(kvf[:, :kvw]).reshape(nb, tq, 4, N_KV_HEADS, HEAD_DIM),
                   jnp.transpose(wout_t, (0, 2, 1)).reshape(nb, WINDOW, 2, N_KV_HEADS, HEAD_DIM),
                   jnp.transpose(cst.reshape(2, nb, -1), (1, 0, 2)),
                   jnp.transpose(fst.reshape(2, nb, -1), (1, 0, 2)),
                   by_batch(vrow)))

    ys = jnp.transpose(xs.reshape(tq, nb, d_model), (1, 0, 2))
    return (xp[None], ys,
            jnp.stack([s[0] for s in ps]), jnp.stack([s[1] for s in ps]),
            jnp.stack([s[2] for s in ps]), jnp.stack([s[3] for s in ps]),
            jnp.stack([s[0] for s in ss]), jnp.stack([s[1] for s in ss]),
            jnp.stack([s[2] for s in ss]), jnp.stack([s[3] for s in ss]),
            jnp.stack([s[4] for s in ss]))
```

```python
import functools
import math

import numpy as np
import jax
import jax.numpy as jnp
from jax import lax
from jax.experimental import pallas as pl
from jax.experimental.pallas import tpu as pltpu

F32 = jnp.float32
BF16 = jnp.bfloat16

HEAD_DIM = 64
N_HEADS = 8
N_KV_HEADS = 2
GQA = N_HEADS // N_KV_HEADS
N_BRANCH = 3
CONV_K = 3
CMP_LEN = 32
CMP_STRIDE = 16
CMP_HID = 128
SLC_BLOCK = 64
N_SELECT = 16
WINDOW = 512
Q_BLOCK = 128
PAGE_SIZE = 128
GMLP_CHUNK = 128
GMLP_GROUPS = 4
LN_EPS = 1e-5
NEG_INF = -1e30
FORCE = 1e4
REMOVED = -3e38
LOG2E = 1.4426950408889634

LANES = 128
KEY_TILE = 512
BLOCKS_PER_TILE = KEY_TILE // SLC_BLOCK
NS_ROWS = 16
TILE_UNROLL = 8
SEQS_PER_SELECT_STEP = 8
CAUSAL_VARIANTS = 4
VMEM_LIMIT = 56 * 1024 * 1024


def _dot(a, b):
    return jnp.dot(a, b, preferred_element_type=F32)


def _dot_nt(a, b):
    return lax.dot_general(a, b, (((1,), (1,)), ((), ())), preferred_element_type=F32)


def _split(a):
    hi = a.astype(BF16)
    lo = (a - hi.astype(F32)).astype(BF16)
    return hi, lo


def _dot3(a, b):
    ah, al = _split(a)
    bh, bl = _split(b)
    return _dot(ah, bh) + _dot(ah, bl) + _dot(al, bh)


def _sigmoid(x):
    return 1.0 / (1.0 + jnp.exp(-x))


def _gelu(x):
    c = math.sqrt(2.0 / math.pi)
    return 0.5 * x * (1.0 + jnp.tanh(c * (x + 0.044715 * (x * x * x))))


def _layer_norm(x, g, b):
    mu = jnp.mean(x, axis=-1, keepdims=True)
    xc = x - mu
    var = jnp.mean(xc * xc, axis=-1, keepdims=True)
    return xc * lax.rsqrt(var + LN_EPS) * g + b


def _masked_softmax(s, mask, axis):
    sm = jnp.where(mask, s, NEG_INF)
    mx = jnp.max(sm, axis=axis, keepdims=True)
    e = jnp.where(mask, jnp.exp2(s - mx), 0.0)
    l = jnp.sum(e, axis=axis, keepdims=True)
    return e * (1.0 / jnp.where(l > 0.0, l, 1.0))


def _softmax_rows(sm, col_valid):
    e = jnp.exp2(sm - jnp.max(sm, axis=0, keepdims=True))
    inv = 1.0 / jnp.sum(e, axis=0, keepdims=True)
    if col_valid is not None:
        inv = jnp.where(col_valid, inv, 0.0)
    return e * inv


def _top_select(val, blk_f, n_blk, axis):
    sel = jnp.zeros_like(val)
    firsts, tops = [], []
    for _ in range(N_SELECT):
        mx = jnp.max(val, axis=axis, keepdims=True)
        first = jnp.min(jnp.where(val == mx, blk_f, float(n_blk)), axis=axis, keepdims=True)
        hit = blk_f == first
        sel = jnp.where(hit, 1.0, sel)
        val = jnp.where(hit, REMOVED, val)
        firsts.append(first)
        tops.append(mx)
    return sel, firsts, tops


def _shifted_conv(src_ref, x, w_ref, b_ref, pad, shift, rows):
    x2 = src_ref[pl.ds(pad - 2 * shift, rows), :]
    x1 = src_ref[pl.ds(pad - shift, rows), :]
    return w_ref[0:1, :] * x2 + w_ref[1:2, :] * x1 + w_ref[2:3, :] * x + b_ref[...]


def _ada_kernel(c_ref, w_ref, b_ref, o_ref):
    c = c_ref[...]
    o_ref[0] = _dot3(c * _sigmoid(c), w_ref[0]) + b_ref[0]


def _ada_call(c_all, w_ada, b_ada):
    depth, d_model, n_mod = w_ada.shape
    rc = c_all.shape[0]
    tn = 1024
    return pl.pallas_call(
        _ada_kernel,
        grid=(depth, n_mod // tn),
        in_specs=[pl.BlockSpec((rc, d_model), lambda l, n: (0, 0)),
                  pl.BlockSpec((1, d_model, tn), lambda l, n: (l, 0, n)),
                  pl.BlockSpec((1, 1, tn), lambda l, n: (l, 0, n))],
        out_specs=pl.BlockSpec((1, rc, tn), lambda l, n: (l, 0, n)),
        out_shape=jax.ShapeDtypeStruct((depth, rc, n_mod), F32),
        compiler_params=pltpu.CompilerParams(dimension_semantics=("arbitrary", "arbitrary"),
                                             vmem_limit_bytes=VMEM_LIMIT),
        name="ada_mod",
    )(c_all, w_ada, b_ada.reshape(depth, 1, n_mod))


_C_AB, _C_AC, _C_AH = 0, 256, 512
_C_Q = 768
_C_KV = 1792
_C_GATE = 2560
_C_GU = 2688
_C_GV = 2944
_N_COL = 3200
_N_SLOT = 6


def _in_proj_kernel(x_ref, sc_ref, sh_ref, w_ref, cw_ref, cb_ref, cpast_ref, lng_ref, lnb_ref, wm_ref, sb_ref,
                    *rest, shift, tm, pad, transposed):
    if transposed:
        (oa_ref, oc_ref, cstate_ref, qt_ref, kvp_ref, kvt_ref, kcr_ref, vcr_ref, ksl_ref, kwn_ref, vslt_ref, vwnt_ref,
         gatet_ref, zs_ref) = rest
    else:
        (oa_ref, oc_ref, cstate_ref, q_ref, kvf_ref, ksl_ref, vsl_ref, gate_ref, vrow_ref, zs_ref) = rest
    i = pl.program_id(0)

    @pl.when(i == 0)
    def _():
        zs_ref[pl.ds(pad - 2 * shift, 2 * shift), :] = cpast_ref[...]

    h = (x_ref[...] * (1.0 + sc_ref[...]) + sh_ref[...]).astype(BF16)
    p = _dot(h, w_ref[...])

    z = p[:, _C_AC:_C_AC + 256] * p[:, _C_AH:_C_AH + 256]
    zs_ref[pl.ds(pad, tm), :] = z
    y = _shifted_conv(zs_ref, z, cw_ref, cb_ref, pad, shift, tm)
    oa_ref[...] = (p[:, _C_AB:_C_AB + 256] * y).astype(BF16)
    tail = zs_ref[pl.ds(pad + tm - 2 * shift, 2 * shift), :]
    cstate_ref[...] = tail
    zs_ref[pl.ds(pad - 2 * shift, 2 * shift), :] = tail

    kv = [p[:, _C_KV + k * LANES:_C_KV + (k + 1) * LANES] for k in range(_N_SLOT)]
    gate = _sigmoid(p[:, _C_GATE:_C_GATE + LANES])
    if transposed:
        for hh in range(N_HEADS):
            qt_ref[pl.ds(hh * LANES, LANES), :] = p[:, _C_Q + hh * LANES:_C_Q + (hh + 1) * LANES].T.astype(BF16)
        kvt = [a.T for a in kv]
        for k in range(4):
            for pg in range(tm // PAGE_SIZE):
                kvp_ref[pg, pl.ds(k * LANES, LANES), :] = kvt[k][:, pg * PAGE_SIZE:(pg + 1) * PAGE_SIZE]
        for k in range(4, _N_SLOT):
            kvt_ref[pl.ds((k - 4) * LANES, LANES), :] = kvt[k]
        kcr_ref[...] = kv[0].astype(BF16)
        vcr_ref[...] = kv[1].astype(BF16)
        ksl_ref[...] = kv[2].astype(BF16)
        kwn_ref[...] = kv[4].astype(BF16)
        vslt_ref[...] = kvt[3].astype(BF16)
        vwnt_ref[...] = kvt[5].astype(BF16)
        gatet_ref[...] = gate.T
    else:
        q_ref[...] = p[:, _C_Q:_C_Q + 1024].astype(BF16)
        kvf_ref[...] = p[:, _C_KV:_C_KV + _N_SLOT * LANES]
        ksl_ref[...] = kv[2].astype(BF16)
        vsl_ref[...] = kv[3].astype(BF16)
        gate_ref[...] = gate

    u = _gelu(p[:, _C_GU:_C_GU + 256])
    v = _layer_norm(_gelu(p[:, _C_GV:_C_GV + 256]), lng_ref[...], lnb_ref[...])
    if not transposed:
        vrow_ref[...] = v
    lane = lax.broadcasted_iota(jnp.int32, (1, 256), 1)
    for c in range(tm // GMLP_CHUNK):
        vc = v[c * GMLP_CHUNK:(c + 1) * GMLP_CHUNK]
        mixed = sb_ref[...]
        for g in range(GMLP_GROUPS):
            vg = jnp.where((lane >= g * HEAD_DIM) & (lane < (g + 1) * HEAD_DIM), vc, 0.0).astype(BF16)
            mixed = mixed + _dot(wm_ref[g], vg)
        oc_ref[pl.ds(c * GMLP_CHUNK, GMLP_CHUNK), :] = (u[c * GMLP_CHUNK:(c + 1) * GMLP_CHUNK] * mixed).astype(BF16)


def _in_proj_call(x, sc, sh, w_all, cw, cb, cpast, lng, lnb, wm, sb, *, shift, tm, transposed, name):
    rows, d_model = x.shape
    pad = max(8, 2 * shift)
    mr = sc.shape[0]
    mod_spec = (pl.BlockSpec((1, d_model), lambda i: (0, 0)) if mr == 1
                else pl.BlockSpec((tm, d_model), lambda i: (i, 0)))

    def row_spec(n):
        return pl.BlockSpec((tm, n), lambda i: (i, 0))

    def col_spec(n):
        return pl.BlockSpec((n, tm), lambda i: (0, i))

    def full(a):
        nd = a.ndim
        return pl.BlockSpec(a.shape, lambda i: (0,) * nd)

    sds = jax.ShapeDtypeStruct
    out_shape = [sds((rows, 256), BF16), sds((rows, 256), BF16), sds((2 * shift, 256), F32)]
    out_specs = [row_spec(256), row_spec(256), pl.BlockSpec((2 * shift, 256), lambda i: (0, 0))]
    if transposed:
        out_shape += [sds((N_HEADS * LANES, rows), BF16), sds((rows // PAGE_SIZE, 4 * LANES, PAGE_SIZE), F32),
                      sds((2 * LANES, rows), F32)]
        out_specs += [col_spec(N_HEADS * LANES),
                      pl.BlockSpec((tm // PAGE_SIZE, 4 * LANES, PAGE_SIZE), lambda i: (i, 0, 0)), col_spec(2 * LANES)]
        out_shape += [sds((rows, LANES), BF16)] * 4 + [sds((LANES, rows), BF16)] * 2 + [sds((LANES, rows), F32)]
        out_specs += [row_spec(LANES)] * 4 + [col_spec(LANES)] * 3
    else:
        out_shape += [sds((rows, 1024), BF16), sds((rows, _N_SLOT * LANES), F32), sds((rows, LANES), BF16),
                      sds((rows, LANES), BF16), sds((rows, LANES), F32), sds((rows, 256), F32)]
        out_specs += [row_spec(1024), row_spec(_N_SLOT * LANES), row_spec(LANES), row_spec(LANES), row_spec(LANES),
                      row_spec(256)]
    return pl.pallas_call(
        functools.partial(_in_proj_kernel, shift=shift, tm=tm, pad=pad, transposed=transposed),
        grid=(rows // tm,),
        in_specs=[row_spec(d_model), mod_spec, mod_spec, full(w_all), full(cw), full(cb), full(cpast),
                  full(lng), full(lnb), full(wm), full(sb)],
        out_specs=out_specs,
        out_shape=out_shape,
        scratch_shapes=[pltpu.VMEM((pad + tm, 256), F32)],
        compiler_params=pltpu.CompilerParams(dimension_semantics=("arbitrary",), vmem_limit_bytes=VMEM_LIMIT),
        name=name,
    )(x, sc, sh, w_all, cw, cb, cpast, lng, lnb, wm, sb)


def _pe_term_kernel(pe_ref, w1_ref, o_ref):
    for s in range(2):
        o_ref[s] = _dot3(pe_ref[s], w1_ref[s])


def _pe_term_call(pe_flat, w1_flat):
    return pl.pallas_call(
        _pe_term_kernel,
        out_shape=jax.ShapeDtypeStruct((2, 8, CMP_HID), F32),
        compiler_params=pltpu.CompilerParams(vmem_limit_bytes=VMEM_LIMIT),
        name="cmp_pe_term",
    )(pe_flat, w1_flat)


def _compress_prompt_kernel(kx_ref, vx_ref, w1_ref, pet_ref, w2_ref, kc_ref, vct_ref, sh_ref, *, nc):
    sh_ref[pl.ds(0, 8), :] = jnp.zeros((8, CMP_HID), F32)
    for s, x_ref in enumerate((kx_ref, vx_ref)):
        parts = _dot(x_ref[...], w1_ref[s])
        acc = jnp.zeros((nc, LANES), F32)
        for g in range(N_KV_HEADS):
            p0 = parts[:, g * 256:g * 256 + CMP_HID]
            p1 = parts[:, g * 256 + CMP_HID:(g + 1) * 256]
            sh_ref[pl.ds(8, nc), :] = p0
            p0s = sh_ref[pl.ds(7, nc), :]
            hid = _gelu(p0s + p1 + pet_ref[s][0:1, :])
            acc = acc + _dot(hid.astype(BF16), w2_ref[s, g])
        if s == 0:
            kc_ref[...] = acc.astype(BF16)
        else:
            vct_ref[...] = acc.T.astype(BF16)


def _compress_prompt_call(kx, vx, w1p, pet, w2p):
    nc = kx.shape[0]
    return pl.pallas_call(
        functools.partial(_compress_prompt_kernel, nc=nc),
        out_shape=[jax.ShapeDtypeStruct((nc, LANES), BF16), jax.ShapeDtypeStruct((LANES, nc), BF16)],
        scratch_shapes=[pltpu.VMEM((nc + 8, CMP_HID), F32)],
        compiler_params=pltpu.CompilerParams(vmem_limit_bytes=VMEM_LIMIT),
        name="compress_prompt",
    )(kx, vx, w1p, pet, w2p)


def _attn_prompt_kernel(qt_ref, gatet_ref, kc_ref, vct_ref, covert_ref, ksl_ref, vslt_ref, kwn_ref, vwnt_ref, eb_ref,
                        o_ref, ns_ref, s_ref, m_ref, ocmp_ref, *, nc, ns):
    qb = pl.program_id(0)
    s0 = qb * Q_BLOCK
    ncol = GQA * Q_BLOCK
    pair = 2 * Q_BLOCK
    col = lax.broadcasted_iota(jnp.int32, (1, ncol), 1)
    qpos_c = s0 + (col & (Q_BLOCK - 1))
    qp = s0 + lax.broadcasted_iota(jnp.int32, (1, Q_BLOCK), 1)
    m_idx = lax.broadcasted_iota(jnp.int32, (nc, 1), 0)
    blk = lax.broadcasted_iota(jnp.int32, (ns, 1), 0)
    blk_f = blk.astype(F32)
    key_row = lax.broadcasted_iota(jnp.int32, (KEY_TILE, 1), 0)
    n_tiles = s0 // KEY_TILE + 1

    ns_ref[pl.ds(ns, 8), :] = jnp.zeros((8, Q_BLOCK), F32)
    rhs_zero = jnp.zeros((LANES - NS_ROWS, ncol), BF16)

    for g in (pl.program_id(1),):
        rq = jnp.concatenate([qt_ref[pl.ds(h * LANES, LANES), :] for h in range(GQA)], axis=1)

        vrows = pl.ds(pl.multiple_of(g * HEAD_DIM, HEAD_DIM), HEAD_DIM)

        def compressed_and_select(frac):
            nr, nsr = nc * frac // CAUSAL_VARIANTS, ns * frac // CAUSAL_VARIANTS

            def run():
                m_i = m_idx[0:nr]
                last_pos = jnp.where(m_i >= 1, CMP_STRIDE * (m_i - 1) + CMP_LEN - 1, 2 ** 30)
                sc = jnp.where(last_pos <= qpos_c, _dot(kc_ref[pl.ds(0, nr), :], rq), NEG_INF)
                p = _softmax_rows(sc, qpos_c >= CMP_LEN - 1)
                o_c = _dot(vct_ref[vrows, pl.ds(0, nr)], p.astype(BF16))
                p4 = (p[:, 0:Q_BLOCK] + p[:, Q_BLOCK:2 * Q_BLOCK] + p[:, 2 * Q_BLOCK:3 * Q_BLOCK]
                      + p[:, 3 * Q_BLOCK:4 * Q_BLOCK])
                hi, lo = _split(p4)
                cov = covert_ref[pl.ds(0, nsr), pl.ds(0, nr)]
                imp = _dot(cov, hi) + _dot(cov, lo)
                b_i, b_f = blk[0:nsr], blk_f[0:nsr]
                qblk = qp // SLC_BLOCK
                elig = b_i * SLC_BLOCK <= qp
                forced = (b_i == 0) | (b_i == qblk) | (b_i == qblk - 1)
                val = jnp.where(elig, jnp.where(forced, REMOVED, imp), -FORCE)
                for _ in range(N_SELECT - 3):
                    mx = jnp.max(val, axis=0, keepdims=True)
                    first = jnp.min(jnp.where(val == mx, b_f, float(ns)), axis=0, keepdims=True)
                    val = jnp.where(b_f == first, REMOVED, val)
                ns_ref[pl.ds(0, nsr), :] = jnp.where(elig & (val < 0.5 * REMOVED), 0.0, 1.0)
                if nsr < ns:
                    ns_ref[pl.ds(nsr, ns - nsr), :] = jnp.ones((ns - nsr, Q_BLOCK), F32)
                ocmp_ref[...] = o_c
            return run

        variant = (qb * CAUSAL_VARIANTS) // (nc * CMP_STRIDE // Q_BLOCK)
        for f in range(CAUSAL_VARIANTS):
            pl.when(variant == f)(compressed_and_select(f + 1))
        o_cmp = ocmp_ref[...]

        def tile_scores(kt):
            k0 = pl.multiple_of(kt * KEY_TILE, KEY_TILE)
            nsf = ns_ref[pl.ds(pl.multiple_of(kt * BLOCKS_PER_TILE, BLOCKS_PER_TILE), NS_ROWS), :].astype(BF16)
            rhs = jnp.concatenate([rq, jnp.concatenate([nsf] * GQA, axis=1), rhs_zero], axis=0)
            lhs = jnp.concatenate([ksl_ref[pl.ds(k0, KEY_TILE), :], eb_ref[...]], axis=1)
            return _dot(lhs, rhs)

        ones_rows = jnp.ones((NS_ROWS, KEY_TILE), BF16)

        def flash_update(s, s_max, k0, carry):
            m_run, acc = carry
            m_new = jnp.maximum(m_run, s_max)
            pe = jnp.exp2(s - m_new).astype(BF16)
            v_aug = jnp.concatenate([vslt_ref[vrows, pl.ds(k0, KEY_TILE)], ones_rows], axis=0)
            return m_new, jnp.exp2(m_run - m_new) * acc + _dot(v_aug, pe)

        def produce(dst_ref, dmx_ref, kt):
            s = tile_scores(kt)
            dst_ref[...] = s
            dmx_ref[...] = jnp.max(s, axis=0, keepdims=True)

        def update_from(src_ref, smx_ref, kt, carry):
            k0 = pl.multiple_of(kt * KEY_TILE, KEY_TILE)
            return tuple(flash_update(src_ref[:, pl.ds(hp * pair, pair)], smx_ref[:, pl.ds(hp * pair, pair)], k0,
                                      carry[hp]) for hp in range(GQA // 2))

        sa_ref, ma_ref = s_ref.at[0], m_ref.at[0]
        produce(sa_ref, ma_ref, 0)

        def chain(kt, n, carry):
            for i in range(n):
                produce(s_ref.at[(i + 1) % n], m_ref.at[(i + 1) % n], kt + i + 1)
                carry = update_from(s_ref.at[i], m_ref.at[i], kt + i, carry)
            return carry

        init = (jnp.full((1, pair), NEG_INF, F32), jnp.zeros((HEAD_DIM + NS_ROWS, pair), F32))
        last = n_tiles - 1
        quads = last // TILE_UNROLL
        carry = lax.fori_loop(0, quads, lambda j, c: chain(TILE_UNROLL * j, TILE_UNROLL, c), (init,) * (GQA // 2))
        done = TILE_UNROLL * quads
        n = TILE_UNROLL // 2
        while n >= 2:
            take = last - done >= n
            carry = lax.cond(take, lambda c, done=done, n=n: chain(done, n, c), lambda c: c, carry)
            done = done + jnp.where(take, n, 0)
            n //= 2

        def odd_step(carry):
            carry = update_from(sa_ref, ma_ref, last - 1, carry)
            produce(sa_ref, ma_ref, last)
            return carry

        carry = lax.cond(lax.rem(last, 2) == 1, odd_step, lambda c: c, carry)
        k0 = pl.multiple_of(last * KEY_TILE, KEY_TILE)
        o_slc_parts = []
        for hp in range(GQA // 2):
            s = sa_ref[:, pl.ds(hp * pair, pair)]
            s = jnp.where(k0 + key_row <= qpos_c[:, hp * pair:(hp + 1) * pair], s, NEG_INF)
            _, acc = flash_update(s, jnp.max(s, axis=0, keepdims=True), k0, carry[hp])
            o_slc_parts.append(acc[0:HEAD_DIM] * (1.0 / acc[HEAD_DIM:HEAD_DIM + 1]))
        o_slc = jnp.concatenate(o_slc_parts, axis=1)

        w0 = pl.multiple_of(jnp.maximum(s0 - WINDOW, 0), Q_BLOCK)
        dist = qp - (w0 + lax.broadcasted_iota(jnp.int32, (WINDOW + Q_BLOCK, 1), 0))
        wbias = jnp.where(lax.bitcast_convert_type(dist, jnp.uint32) <= WINDOW, 0.0, NEG_INF)
        sw = _dot(kwn_ref[pl.ds(w0, WINDOW + Q_BLOCK), :], rq) + jnp.concatenate([wbias] * GQA, axis=1)
        pw = _softmax_rows(sw, None)
        o_win = _dot(vwnt_ref[vrows, pl.ds(w0, WINDOW + Q_BLOCK)], pw.astype(BF16))

        outs = []
        for h in range(GQA):
            c = slice(h * Q_BLOCK, (h + 1) * Q_BLOCK)
            gr = (GQA * g + h) * N_BRANCH
            outs.append(gatet_ref[pl.ds(gr, 1), :] * o_cmp[:, c] + gatet_ref[pl.ds(gr + 1, 1), :] * o_slc[:, c]
                        + gatet_ref[pl.ds(gr + 2, 1), :] * o_win[:, c])
        for hp in range(GQA // 2):
            o2 = jnp.concatenate(outs[2 * hp:2 * hp + 2], axis=0)
            o_ref[:, hp * LANES:(hp + 1) * LANES] = o2.T.astype(BF16)


def _attn_prompt_call(qt, gatet, kc, vct, covert, ksl, vslt, kwn, vwnt, ebias):
    t = qt.shape[1]
    ns, nc = covert.shape
    vmem = pl.BlockSpec(memory_space=pltpu.VMEM)
    return pl.pallas_call(
        functools.partial(_attn_prompt_kernel, nc=nc, ns=ns),
        grid=(t // Q_BLOCK, N_KV_HEADS),
        in_specs=[pl.BlockSpec((GQA * LANES, Q_BLOCK), lambda i, g: (g, i)),
                  pl.BlockSpec((LANES, Q_BLOCK), lambda i, g: (0, i)),
                  vmem, vmem, vmem, vmem, vmem, vmem, vmem, vmem],
        out_specs=pl.BlockSpec((Q_BLOCK, GQA * HEAD_DIM), lambda i, g: (i, g)),
        out_shape=jax.ShapeDtypeStruct((t, N_HEADS * HEAD_DIM), BF16),
        scratch_shapes=[pltpu.VMEM((ns + 8, Q_BLOCK), F32),
                        pltpu.VMEM((TILE_UNROLL, KEY_TILE, GQA * Q_BLOCK), F32),
                        pltpu.VMEM((TILE_UNROLL, 1, GQA * Q_BLOCK), F32),
                        pltpu.VMEM((HEAD_DIM, GQA * Q_BLOCK), F32)],
        compiler_params=pltpu.CompilerParams(dimension_semantics=("arbitrary", "arbitrary"),
                                             vmem_limit_bytes=VMEM_LIMIT),
        name="attn_prompt",
    )(qt, gatet, kc, vct, covert, ksl, vslt, kwn, vwnt, ebias)


def _out_ffn_kernel(x_ref, oa_ref, ob_ref, oc_ref, g1_ref, sc2_ref, sh2_ref, g2_ref, wo_ref, lng_ref, lnb_ref,
                    wup_ref, cfw_ref, cfb_ref, fpast_ref, wdn_ref, xo_ref, fstate_ref, us_ref,
                    *, shift, tm, pad, alpha, d_ff):
    i = pl.program_id(0)

    @pl.when(i == 0)
    def _():
        us_ref[pl.ds(pad - 2 * shift, 2 * shift), :] = fpast_ref[...]

    mix = _dot(jnp.concatenate([oa_ref[...], ob_ref[...], oc_ref[...]], axis=1), wo_ref[...])
    x1 = _layer_norm(alpha * x_ref[...] + (1.0 + g1_ref[...]) * mix, lng_ref[0:1, :], lnb_ref[0:1, :])
    h2 = (x1 * (1.0 + sc2_ref[...]) + sh2_ref[...]).astype(BF16)
    up = _dot(h2, wup_ref[...])
    ua = up[:, :d_ff]
    us_ref[pl.ds(pad, tm), :] = ua
    yc = _shifted_conv(us_ref, ua, cfw_ref, cfb_ref, pad, shift, tm)
    tail = us_ref[pl.ds(pad + tm - 2 * shift, 2 * shift), :]
    fstate_ref[...] = tail
    us_ref[pl.ds(pad - 2 * shift, 2 * shift), :] = tail
    act = (yc * _sigmoid(yc) * up[:, d_ff:]).astype(BF16)
    y = _dot(act, wdn_ref[...])
    xo_ref[...] = _layer_norm(alpha * x1 + (1.0 + g2_ref[...]) * y, lng_ref[1:2, :], lnb_ref[1:2, :])


def _out_ffn_call(x, oa, ob, oc, g1, sc2, sh2, g2, wo, lng, lnb, wup, cfw, cfb, fpast, wdn, *, shift, tm, alpha, name):
    rows, d_model = x.shape
    d_ff = wdn.shape[0]
    pad = max(8, 2 * shift)
    mr = g1.shape[0]
    mod_spec = (pl.BlockSpec((1, d_model), lambda i: (0, 0)) if mr == 1
                else pl.BlockSpec((tm, d_model), lambda i: (i, 0)))

    def row_spec(n):
        return pl.BlockSpec((tm, n), lambda i: (i, 0))

    vmem = pl.BlockSpec(memory_space=pltpu.VMEM)
    return pl.pallas_call(
        functools.partial(_out_ffn_kernel, shift=shift, tm=tm, pad=pad, alpha=alpha, d_ff=d_ff),
        grid=(rows // tm,),
        in_specs=[row_spec(d_model), row_spec(256), row_spec(N_HEADS * HEAD_DIM), row_spec(256), mod_spec, mod_spec, mod_spec,
                  mod_spec, vmem, vmem, vmem, vmem, vmem, vmem, vmem, vmem],
        out_specs=[row_spec(d_model), pl.BlockSpec((2 * shift, d_ff), lambda i: (0, 0))],
        out_shape=[jax.ShapeDtypeStruct((rows, d_model), F32), jax.ShapeDtypeStruct((2 * shift, d_ff), F32)],
        scratch_shapes=[pltpu.VMEM((pad + tm, d_ff), F32)],
        compiler_params=pltpu.CompilerParams(dimension_semantics=("arbitrary",), vmem_limit_bytes=VMEM_LIMIT),
        name=name,
    )(x, oa, ob, oc, g1, sc2, sh2, g2, wo, lng, lnb, wup, cfw, cfb, fpast, wdn)


def _cmp_stream_kernel(pt_ref, cache_ref, perm_ref, w1_ref, pet_ref, w2_ref, kc_ref, vc_ref, buf, rbuf, sem, carry,
                       *, page_base, n_pages, pg, n_groups, total):
    b = pl.program_id(0)
    gi = pl.program_id(1)
    step = b * n_groups + gi
    slot = lax.rem(step, 2)
    m = pg * (PAGE_SIZE // CMP_STRIDE)

    def page_copy(page, slt, i):
        return pltpu.make_async_copy(cache_ref.at[page, pl.ds(0, 2 * LANES), :], buf.at[slt, i], sem.at[slt])

    def issue(stp, slt):
        base = lax.div(stp, n_groups) * n_pages + lax.rem(stp, n_groups) * pg
        for i in range(pg):
            page_copy(page_base + pt_ref[base + i], slt, i).start()

    @pl.when(step == 0)
    def _():
        carry[...] = jnp.zeros(carry.shape, F32)
        issue(step, slot)

    @pl.when(step + 1 < total)
    def _():
        issue(step + 1, 1 - slot)

    for i in range(pg):
        page_copy(0, slot, i).wait()

    per_chunk = PAGE_SIZE // CMP_STRIDE
    for i in range(pg):
        rbuf[i] = _dot_nt(perm_ref[...], buf[slot, i].astype(BF16))

    def rows_of(j, s):
        return jnp.concatenate([rbuf[i, pl.ds(j * per_chunk, per_chunk), pl.ds(s * LANES, LANES)]
                                for i in range(pg)], axis=0)

    lane = lax.broadcasted_iota(jnp.int32, (1, LANES), 1)
    first_half = lane < HEAD_DIM
    row0 = lax.broadcasted_iota(jnp.int32, (m, 1), 0) == 0
    pieces = [[[], []], [[], []]]
    for pr in range(CMP_STRIDE // 2):
        for s in range(2):
            a = rows_of(2 * pr, s)
            bb = rows_of(2 * pr + 1, s)
            pieces[s][0].append(jnp.where(first_half, a, pltpu.roll(bb, HEAD_DIM, 1)).astype(BF16))
            pieces[s][1].append(jnp.where(first_half, pltpu.roll(a, HEAD_DIM, 1), bb).astype(BF16))
    for s, o_ref in enumerate((kc_ref, vc_ref)):
        acc = jnp.zeros((m, LANES), F32)
        for g in range(N_KV_HEADS):
            parts = _dot(jnp.concatenate(pieces[s][g], axis=1), w1_ref[s])
            p0 = parts[:, :CMP_HID]
            p1 = parts[:, CMP_HID:]
            prev = jnp.where(gi == 0, 0.0, carry[s * 2 + g][0:1, :])
            p0s = jnp.where(row0, prev, pltpu.roll(p0, 1, 0))
            carry[s * 2 + g] = jnp.broadcast_to(p0[m - 1:m, :], (8, CMP_HID))
            hid = _gelu(p0s + p1 + pet_ref[s][0:1, :])
            acc = acc + _dot(hid.astype(BF16), w2_ref[s, g])
        o_ref[0] = acc.astype(BF16)


def _cmp_stream_call(pt_flat, cache_t, w1s, pet, w2p, *, layer, n_phys, batch, n_pages, pg):
    n_groups = n_pages // pg
    m = pg * (PAGE_SIZE // CMP_STRIDE)
    nc = n_pages * (PAGE_SIZE // CMP_STRIDE)
    total = batch * n_groups
    out_row = np.arange(PAGE_SIZE)[:, None]
    src_row = (out_row % (PAGE_SIZE // CMP_STRIDE)) * CMP_STRIDE + out_row // (PAGE_SIZE // CMP_STRIDE)
    perm = jnp.asarray((np.arange(PAGE_SIZE)[None, :] == src_row).astype(np.float32), dtype=BF16)

    def full(a):
        nd = a.ndim
        return pl.BlockSpec(a.shape, lambda b, g, pt: (0,) * nd)

    grid_spec = pltpu.PrefetchScalarGridSpec(
        num_scalar_prefetch=1,
        grid=(batch, n_groups),
        in_specs=[pl.BlockSpec(memory_space=pl.ANY), full(perm), full(w1s), full(pet), full(w2p)],
        out_specs=[pl.BlockSpec((1, m, LANES), lambda b, g, pt: (b, g, 0))] * 2,
        scratch_shapes=[pltpu.VMEM((2, pg, 2 * LANES, PAGE_SIZE), F32), pltpu.VMEM((pg, PAGE_SIZE, 2 * LANES), F32),
                        pltpu.SemaphoreType.DMA((2,)), pltpu.VMEM((4, 8, CMP_HID), F32)],
    )
    return pl.pallas_call(
        functools.partial(_cmp_stream_kernel, page_base=layer * n_phys, n_pages=n_pages, pg=pg,
                          n_groups=n_groups, total=total),
        grid_spec=grid_spec,
        out_shape=[jax.ShapeDtypeStruct((batch, nc, LANES), BF16)] * 2,
        compiler_params=pltpu.CompilerParams(dimension_semantics=("arbitrary", "arbitrary"),
                                             vmem_limit_bytes=VMEM_LIMIT),
        name="cmp_stream_sample",
    )(pt_flat, cache_t, perm, w1s, pet, w2p)


def _cmp_attn_sample_kernel(q_ref, kc_ref, vc_ref, cover_ref, ocmp_ref, idx_ref, bias_ref,
                            *, past, nc, ns, nsp, n_cache_blocks, bb):
    rows = lax.broadcasted_iota(jnp.int32, (32, 1), 0)
    qpos_r = past + (rows & 7)
    m_idx = lax.broadcasted_iota(jnp.int32, (1, nc), 1)
    blk = lax.broadcasted_iota(jnp.int32, (1, nsp), 1)
    blk_f = blk.astype(F32)
    lane = lax.broadcasted_iota(jnp.int32, (1, LANES), 1)
    cmask = (m_idx >= 1) & (CMP_STRIDE * (m_idx - 1) + CMP_LEN - 1 <= qpos_r)
    imps = []
    for bi in range(bb):
        for g in range(N_KV_HEADS):
            p = _masked_softmax(_dot_nt(q_ref[bi, g], kc_ref[bi]), cmask, -1)
            ocmp_ref[bi, g] = _dot(p.astype(BF16), vc_ref[bi])
            p4 = p[0:8] + p[8:16] + p[16:24] + p[24:32]
            hi, lo = _split(p4)
            imps.append(_dot(hi, cover_ref[...]) + _dot(lo, cover_ref[...]))
    imp = jnp.concatenate(imps, axis=0)
    qp = past + (lax.broadcasted_iota(jnp.int32, (bb * N_KV_HEADS * 8, 1), 0) & 7)
    qblk = qp // SLC_BLOCK
    elig = blk * SLC_BLOCK <= qp
    forced = (blk == 0) | (blk == qblk) | (blk == qblk - 1)
    val = jnp.where(blk < ns, jnp.where(elig, jnp.where(forced, FORCE, imp), -FORCE), REMOVED)
    _, firsts, tops = _top_select(val, blk_f, nsp, -1)
    idx = jnp.zeros((bb * N_KV_HEADS * 8, LANES), F32)
    bias = jnp.zeros((bb * N_KV_HEADS * 8, LANES), F32)
    for t in range(N_SELECT):
        ok = (tops[t] > -0.5 * FORCE) & (firsts[t] < float(n_cache_blocks))
        odd = firsts[t] - 2.0 * jnp.floor(firsts[t] * 0.5)
        idx = jnp.where(lane == t, firsts[t], idx)
        for hf in range(2):
            bias = jnp.where(lane == 2 * t + hf, jnp.where(ok & (odd == float(hf)), 0.0, NEG_INF), bias)
    idx = idx.astype(jnp.int32)
    for bi in range(bb):
        for g in range(N_KV_HEADS):
            r = (bi * N_KV_HEADS + g) * 8
            idx_ref[bi, g] = idx[r:r + 8]
            bias_ref[bi, g] = bias[r:r + 8]


def _cmp_attn_sample_call(q_hq, kc, vc, cover, *, past, ns, n_cache_blocks):
    batch, nc, _ = kc.shape
    nsp = cover.shape[1]
    bb = math.gcd(batch, SEQS_PER_SELECT_STEP)
    blk4 = lambda r: pl.BlockSpec((bb, N_KV_HEADS, r, LANES), lambda b: (b, 0, 0, 0))
    return pl.pallas_call(
        functools.partial(_cmp_attn_sample_kernel, past=past, nc=nc, ns=ns, nsp=nsp, n_cache_blocks=n_cache_blocks,
                          bb=bb),
        grid=(batch // bb,),
        in_specs=[blk4(32), pl.BlockSpec((bb, nc, LANES), lambda b: (b, 0, 0)),
                  pl.BlockSpec((bb, nc, LANES), lambda b: (b, 0, 0)), pl.BlockSpec(cover.shape, lambda b: (0, 0))],
        out_specs=[blk4(32), blk4(8), blk4(8)],
        out_shape=[jax.ShapeDtypeStruct((batch, N_KV_HEADS, 32, LANES), F32),
                   jax.ShapeDtypeStruct((batch, N_KV_HEADS, 8, LANES), jnp.int32),
                   jax.ShapeDtypeStruct((batch, N_KV_HEADS, 8, LANES), F32)],
        compiler_params=pltpu.CompilerParams(dimension_semantics=("arbitrary",), vmem_limit_bytes=VMEM_LIMIT),
        name="cmp_attn_sample",
    )(q_hq, kc, vc, cover)


def _sel_attn_sample_kernel(idx_ref, pt_ref, cache_ref, q_ref, bias_ref, ex_ref, knew_ref, vnew_ref, wint_ref,
                            wnew_ref, wnewt_ref, ocmp_ref, gate_ref, o_ref, wout_ref, buf, sem,
                            *, page_base, n_pages, tq, n_cache_blocks, total):
    b = pl.program_id(0)
    g = pl.program_id(1)
    step = b * N_KV_HEADS + g
    slot = lax.rem(step, 2)
    n_ent = tq * N_SELECT
    blocks_per_page = PAGE_SIZE // SLC_BLOCK

    def page_copy(page, grp, slt, e):
        rows = pl.ds(pl.multiple_of(grp * HEAD_DIM, HEAD_DIM), HEAD_DIM)
        return pltpu.make_async_copy(cache_ref.at[page, pl.ds(2, 2), grp], buf.at[slt, e, :, rows, :], sem.at[slt])

    def issue(stp, slt):
        bb = lax.div(stp, N_KV_HEADS)
        grp = lax.rem(stp, N_KV_HEADS)

        def body(e, _):
            blk = jnp.minimum(idx_ref[stp * n_ent + e], n_cache_blocks - 1)
            page = pt_ref[bb * n_pages + lax.div(blk, blocks_per_page)]
            page_copy(page_base + page, grp, slt, e).start()
            return 0
        lax.fori_loop(0, n_ent, body, 0)

    @pl.when(step == 0)
    def _():
        buf[...] = jnp.zeros(buf.shape, F32)
        issue(step, slot)

    @pl.when(step + 1 < total)
    def _():
        issue(step + 1, 1 - slot)

    def wait_body(e, _):
        page_copy(0, g, slot, e).wait()
        return 0
    lax.fori_loop(0, n_ent, wait_body, 0)

    lane = lax.broadcasted_iota(jnp.int32, (1, LANES), 1)
    col8 = lax.broadcasted_iota(jnp.int32, (1, 8), 1)
    wcol = lax.broadcasted_iota(jnp.int32, (1, WINDOW), 1)
    k_wt = wint_ref[0, pl.ds(0, LANES), :].astype(BF16)
    v_wt = wint_ref[0, pl.ds(LANES, LANES), :].astype(BF16)
    wnew = wnew_ref[0]
    k_wn = wnew[:, 0:LANES].astype(BF16)
    v_wn = wnew[:, LANES:2 * LANES].astype(BF16)
    bias_all = _dot(bias_ref[0, 0].astype(BF16), ex_ref[...])
    keep = (lane >= g * HEAD_DIM) & (lane < (g + 1) * HEAD_DIM)
    for qi in range(tq):
        q4 = q_ref[0, 0, qi]
        new_ok = (col8 <= qi) & (col8 < tq)
        k_t = jnp.concatenate([buf[slot, qi * N_SELECT + k, 0] for k in range(N_SELECT)], axis=1)
        v_t = jnp.concatenate([buf[slot, qi * N_SELECT + k, 1] for k in range(N_SELECT)], axis=1)
        s = _dot(q4, k_t.astype(BF16)) + bias_all[qi:qi + 1, :]
        s_n = jnp.where(new_ok, _dot_nt(q4, knew_ref[0]), NEG_INF)
        mx = jnp.maximum(jnp.max(s, axis=-1, keepdims=True), jnp.max(s_n, axis=-1, keepdims=True))
        pe = jnp.exp2(s - mx)
        pn = jnp.exp2(s_n - mx)
        l = jnp.sum(pe, axis=-1, keepdims=True) + jnp.sum(pn, axis=-1, keepdims=True)
        o_slc = (_dot_nt(pe.astype(BF16), v_t.astype(BF16)) + _dot(pn.astype(BF16), vnew_ref[0])) * (1.0 / l)
        sw = jnp.where(wcol >= qi, _dot(q4, k_wt), NEG_INF)
        sw_n = jnp.where(new_ok, _dot_nt(q4, k_wn), NEG_INF)
        mw = jnp.maximum(jnp.max(sw, axis=-1, keepdims=True), jnp.max(sw_n, axis=-1, keepdims=True))
        pw = jnp.exp2(sw - mw)
        pwn = jnp.exp2(sw_n - mw)
        lw = jnp.sum(pw, axis=-1, keepdims=True) + jnp.sum(pwn, axis=-1, keepdims=True)
        o_win = (_dot_nt(pw.astype(BF16), v_wt) + _dot(pwn.astype(BF16), v_wn)) * (1.0 / lw)
        gt = gate_ref[0, 0, qi]
        o = gt[:, 0:1] * ocmp_ref[0, 0, qi] + gt[:, 1:2] * o_slc + gt[:, 2:3] * o_win
        o_ref[0, 0, qi] = jnp.where(keep, o, 0.0)

    @pl.when(g == 0)
    def _():
        shifted = pltpu.roll(wint_ref[0], WINDOW - tq, 1)
        wout_ref[0, :, pl.ds(0, WINDOW - LANES)] = shifted[:, 0:WINDOW - LANES]
        wout_ref[0, :, pl.ds(WINDOW - LANES, LANES)] = jnp.where(lane >= LANES - tq, wnewt_ref[0],
                                                                  shifted[:, WINDOW - LANES:WINDOW])


def _sel_attn_sample_call(idx_flat, pt_flat, cache_t, q_qh, bias, expand, knew, vnew, win_t, wnew, wnew_t, ocmp_qh,
                          gate_qh, *, layer, n_phys, n_pages, tq, n_cache_blocks):
    batch = q_qh.shape[0]
    n_ent = tq * N_SELECT
    b5 = lambda: pl.BlockSpec((1, 1, tq, 8, LANES), lambda b, g, i, p: (b, g, 0, 0, 0))
    b3 = lambda r, c: pl.BlockSpec((1, r, c), lambda b, g, i, p: (b, 0, 0))
    grid_spec = pltpu.PrefetchScalarGridSpec(
        num_scalar_prefetch=2,
        grid=(batch, N_KV_HEADS),
        in_specs=[pl.BlockSpec(memory_space=pl.ANY), b5(),
                  pl.BlockSpec((1, 1, 8, LANES), lambda b, g, i, p: (b, g, 0, 0)),
                  pl.BlockSpec(expand.shape, lambda b, g, i, p: (0, 0)),
                  b3(8, LANES), b3(8, LANES),
                  pl.BlockSpec((1, 2 * LANES, WINDOW), lambda b, g, i, p: (layer * batch + b, 0, 0)),
                  b3(8, 2 * LANES), b3(2 * LANES, LANES),
                  b5(), b5()],
        out_specs=[b5(), b3(2 * LANES, WINDOW)],
        scratch_shapes=[pltpu.VMEM((2, n_ent, 2, LANES, PAGE_SIZE), F32), pltpu.SemaphoreType.DMA((2,))],
    )
    return pl.pallas_call(
        functools.partial(_sel_attn_sample_kernel, page_base=layer * n_phys, n_pages=n_pages, tq=tq,
                          n_cache_blocks=n_cache_blocks, total=batch * N_KV_HEADS),
        grid_spec=grid_spec,
        out_shape=[jax.ShapeDtypeStruct((batch, N_KV_HEADS, tq, 8, LANES), F32),
                   jax.ShapeDtypeStruct((batch, 2 * LANES, WINDOW), F32)],
        compiler_params=pltpu.CompilerParams(dimension_semantics=("arbitrary", "arbitrary"),
                                             vmem_limit_bytes=VMEM_LIMIT),
        name="sel_attn_sample",
    )(idx_flat, pt_flat, cache_t, q_qh, bias, expand, knew, vnew, win_t, wnew, wnew_t, ocmp_qh, gate_qh)


def _prep_w_in(w):
    d = w.shape[0]
    wq = (w[:, 768:1280] * (HEAD_DIM ** -0.5 * LOG2E)).reshape(d, N_HEADS, HEAD_DIM)
    z = jnp.zeros_like(wq)
    grp = (jnp.arange(N_HEADS) // GQA)[None, :, None]
    wq = jnp.concatenate([jnp.where(grp == 0, wq, z), jnp.where(grp == 1, wq, z)], axis=-1).reshape(d, N_HEADS * LANES)
    gate = jnp.pad(w[:, 2048:2072], ((0, 0), (0, LANES - N_HEADS * N_BRANCH)))
    return jnp.concatenate([w[:, :768], wq, w[:, 1280:2048], gate, w[:, 2072:]], axis=1).astype(BF16)


def _prep_w1(w1):
    w = w1.reshape(2, 2, CMP_STRIDE, HEAD_DIM, CMP_HID)
    return jnp.transpose(w, (0, 2, 3, 1, 4)).reshape(2, CMP_STRIDE * HEAD_DIM, 2 * CMP_HID)


def _prep_w1_grouped(w1s):
    w = w1s.reshape(2, CMP_STRIDE, 1, HEAD_DIM, 1, 2 * CMP_HID)
    eye = jnp.eye(N_KV_HEADS, dtype=w.dtype).reshape(1, 1, N_KV_HEADS, 1, N_KV_HEADS, 1)
    return (w * eye).reshape(2, CMP_STRIDE * N_KV_HEADS * HEAD_DIM, N_KV_HEADS * 2 * CMP_HID)


def _prep_w2(w2):
    z = jnp.zeros_like(w2)
    return jnp.stack([jnp.concatenate([w2, z], axis=-1), jnp.concatenate([z, w2], axis=-1)], axis=1).astype(BF16)


def _cover_matrix(nc, ns_real, ns_pad):
    m = np.arange(nc)[:, None]
    b = np.arange(ns_pad)[None, :]
    return ((m >= 4 * b) & (m <= 4 * b + 4) & (m >= 1) & (b < ns_real)).astype(np.float32)


def _block_bias_matrix():
    k = np.arange(KEY_TILE)[:, None]
    b = np.arange(LANES)[None, :]
    return jnp.asarray(np.where(k // SLC_BLOCK == b, NEG_INF, 0.0).astype(np.float32), dtype=BF16)


def _expand_matrix():
    r = np.arange(LANES)[:, None]
    c = np.arange(N_SELECT * PAGE_SIZE)[None, :]
    return jnp.asarray((c // SLC_BLOCK == r).astype(np.float32), dtype=BF16)


def kernel(x_prompt, x_sample, cache_nsa_kv, state_win_kv, state_conv, state_ffn_conv, page_table, c_prompt, c_sample, w_ada, b_ada, w_in, conv_a_w, conv_a_b, cmp_pe, cmp_w1, cmp_w2, sgu_ln_g, sgu_ln_b, sgu_w, sgu_b, w_o, ln_g, ln_b, w_ffn_up, conv_f_w, conv_f_b, w_ffn_down):
    depth = w_in.shape[0]
    _, t, d_model = x_prompt.shape
    nb, tq, _ = x_sample.shape
    n_phys = cache_nsa_kv.shape[1]
    n_pages = page_table.shape[1]
    past = n_pages * PAGE_SIZE
    d_ff = w_ffn_down.shape[1]
    alpha = (2 * depth) ** 0.25
    rs = nb * tq
    kvw = 4 * N_KV_HEADS * HEAD_DIM
    assert x_prompt.shape[0] == 1 and c_prompt.shape[0] == 1
    assert d_model == 1024 and t % KEY_TILE == 0 and t >= WINDOW + Q_BLOCK
    assert tq == 4 and rs == GMLP_CHUNK and past % KEY_TILE == 0 and past >= WINDOW
    assert state_win_kv.shape[2] == WINDOW

    rc = -(-(1 + nb) // 8) * 8
    c_all = jnp.pad(jnp.concatenate([c_prompt, c_sample], axis=0), ((0, rc - 1 - nb), (0, 0)))
    mods = _ada_call(c_all, w_ada, b_ada)

    nc_p, ns_p = t // CMP_STRIDE, t // SLC_BLOCK
    covert_p = jnp.asarray(_cover_matrix(nc_p, ns_p, ns_p).T, dtype=BF16)
    nc_s = (past + tq) // CMP_STRIDE
    ns_s = -(-(past + tq) // SLC_BLOCK)
    ns_s_pad = -(-ns_s // LANES) * LANES
    n_cache_blocks = past // SLC_BLOCK
    cover_s = jnp.asarray(_cover_matrix(nc_s, ns_s, ns_s_pad), dtype=BF16)
    ebias = _block_bias_matrix()
    expand = _expand_matrix()
    pt_flat = page_table.reshape(-1)
    cache_t = jnp.transpose(cache_nsa_kv.reshape(depth * n_phys, PAGE_SIZE, kvw), (0, 2, 1))
    win_t_all = jnp.transpose(state_win_kv.reshape(depth * nb, WINDOW, 2 * LANES), (0, 2, 1))

    xp = x_prompt[0]
    xs = jnp.transpose(x_sample, (1, 0, 2)).reshape(rs, d_model)
    tril_full = jnp.tril(jnp.ones((GMLP_CHUNK, GMLP_CHUNK), F32))
    tril_tq = jnp.tril(jnp.ones((tq, tq), F32))

    ps, ss = [], []
    for l in range(depth):
        w_all = _prep_w_in(w_in[l])
        wo_p = w_o[l].astype(BF16)
        wup = w_ffn_up[l].astype(BF16)
        wdn = w_ffn_down[l].astype(BF16)
        w1s = _prep_w1(cmp_w1[l])
        w1g = _prep_w1_grouped(w1s).astype(BF16)
        w1s = w1s.astype(BF16)
        w2p = _prep_w2(cmp_w2[l])
        pe_flat = jnp.broadcast_to(cmp_pe[l].reshape(2, 1, CMP_LEN * HEAD_DIM), (2, 8, CMP_LEN * HEAD_DIM))
        pet = _pe_term_call(pe_flat, cmp_w1[l].reshape(2, CMP_LEN * HEAD_DIM, CMP_HID))
        cw, cb = conv_a_w[l], conv_a_b[l].reshape(1, -1)
        cfw, cfb = conv_f_w[l], conv_f_b[l].reshape(1, -1)
        lng, lnb = sgu_ln_g[l].reshape(1, -1), sgu_ln_b[l].reshape(1, -1)

        def mod_rows(r0, r1, rep):
            parts = [mods[l, r0:r1, k * d_model:(k + 1) * d_model] for k in range(6)]
            return [jnp.tile(p_, (rep, 1)) if rep > 1 else p_ for p_ in parts]

        sh1, sc1, g1, sh2, sc2, g2 = mod_rows(0, 1, 1)
        wm_p = (sgu_w[l] * tril_full).astype(BF16)
        sb_p = jnp.repeat(sgu_b[l].T, HEAD_DIM, axis=1)
        (oa, oc, cst, qt, kvp, kvt, kcr, vcr, ksl, kwn, vslt, vwnt, gatet) = _in_proj_call(
            xp, sc1, sh1, w_all, cw, cb, jnp.zeros((2, 256), F32), lng, lnb, wm_p, sb_p,
            shift=1, tm=512, transposed=True, name="in_proj_prompt")
        kc, vct = _compress_prompt_call(kcr.reshape(nc_p, CMP_STRIDE * LANES), vcr.reshape(nc_p, CMP_STRIDE * LANES),
                                        w1g, pet, w2p)
        ob = _attn_prompt_call(qt, gatet, kc, vct, covert_p, ksl, vslt, kwn, vwnt, ebias)
        xp, fst = _out_ffn_call(xp, oa, ob, oc, g1, sc2, sh2, g2, wo_p, ln_g[l], ln_b[l], wup, cfw, cfb,
                                jnp.zeros((2, d_ff), F32), wdn, shift=1, tm=256, alpha=alpha, name="out_ffn_prompt")
        paged = kvp.reshape(t // PAGE_SIZE, 4, N_KV_HEADS, HEAD_DIM, PAGE_SIZE)
        winr = kvt[:, t - WINDOW:].reshape(2, N_KV_HEADS, HEAD_DIM, WINDOW)
        ps.append((jnp.transpose(paged, (0, 4, 1, 2, 3))[None], jnp.transpose(winr, (3, 0, 1, 2))[None],
                   cst[None], fst[None]))

        sh1, sc1, g1, sh2, sc2, g2 = mod_rows(1, 1 + nb, tq)
        eye_b = jnp.eye(nb, dtype=F32)
        wm_s = jax.vmap(lambda w: jnp.kron(w[:tq, :tq] * tril_tq, eye_b))(sgu_w[l]).astype(BF16)
        sb_s = jnp.repeat(jnp.repeat(sgu_b[l][:, :tq].T, nb, axis=0), HEAD_DIM, axis=1)
        cpast = jnp.transpose(state_conv[l], (1, 0, 2)).reshape(2 * nb, -1)
        (oa, oc, cst, q, kvf, ksl, vsl, gate, vrow) = _in_proj_call(
            xs, sc1, sh1, w_all, cw, cb, cpast, lng, lnb, wm_s, sb_s, shift=nb, tm=rs, transposed=False,
            name="in_proj_sample")
        kc, vc = _cmp_stream_call(pt_flat, cache_t, w1s, pet, w2p, layer=l, n_phys=n_phys, batch=nb,
                                  n_pages=n_pages, pg=min(32, n_pages))

        def by_batch(a):
            return jnp.transpose(a.reshape(tq, nb, -1), (1, 0, 2))

        qb5 = by_batch(q).reshape(nb, tq, N_KV_HEADS, GQA, LANES)
        q_hq = jnp.pad(jnp.transpose(qb5, (0, 2, 3, 1, 4)), ((0, 0), (0, 0), (0, 0), (0, 8 - tq), (0, 0)))
        q_hq = q_hq.reshape(nb, N_KV_HEADS, 32, LANES)
        q_qh = jnp.pad(jnp.transpose(qb5, (0, 2, 1, 3, 4)), ((0, 0), (0, 0), (0, 0), (0, 8 - GQA), (0, 0)))
        ocmp, idx, bias = _cmp_attn_sample_call(q_hq, kc, vc, cover_s, past=past, ns=ns_s,
                                                n_cache_blocks=n_cache_blocks)
        ocmp_qh = jnp.transpose(ocmp.reshape(nb, N_KV_HEADS, GQA, 8, LANES)[:, :, :, :tq], (0, 1, 3, 2, 4))
        ocmp_qh = jnp.pad(ocmp_qh, ((0, 0), (0, 0), (0, 0), (0, 8 - GQA), (0, 0)))
        g5 = by_batch(gate)[:, :, :N_HEADS * N_BRANCH].reshape(nb, tq, N_KV_HEADS, GQA, N_BRANCH)
        gate_qh = jnp.pad(jnp.transpose(g5, (0, 2, 1, 3, 4)),
                          ((0, 0), (0, 0), (0, 0), (0, 8 - GQA), (0, LANES - N_BRANCH)))
        pad8 = lambda a: jnp.pad(by_batch(a), ((0, 0), (0, 8 - tq), (0, 0)))
        wnew_rows = by_batch(kvf[:, kvw:])
        wnew_t = jnp.pad(jnp.transpose(wnew_rows, (0, 2, 1)), ((0, 0), (0, 0), (LANES - tq, 0)))
        o5, wout_t = _sel_attn_sample_call(
            idx[:, :, :tq, :N_SELECT].reshape(-1), pt_flat,
            cache_t.reshape(depth * n_phys, 4, N_KV_HEADS, HEAD_DIM, PAGE_SIZE), q_qh, bias, expand, pad8(ksl), pad8(vsl),
            win_t_all, jnp.pad(wnew_rows, ((0, 0), (0, 8 - tq), (0, 0))), wnew_t, ocmp_qh, gate_qh,
            layer=l, n_phys=n_phys, n_pages=n_pages, tq=tq, n_cache_blocks=n_cache_blocks)
        o5 = o5[:, :, :, :GQA, :HEAD_DIM] + o5[:, :, :, :GQA, HEAD_DIM:]
        ob = jnp.transpose(o5, (2, 0, 1, 3, 4)).reshape(rs, N_HEADS * HEAD_DIM).astype(BF16)
        fpast = jnp.transpose(state_ffn_conv[l], (1, 0, 2)).reshape(2 * nb, -1)
        xs, fst = _out_ffn_call(xs, oa, ob, oc, g1, sc2, sh2, g2, wo_p, ln_g[l], ln_b[l], wup, cfw, cfb, fpast, wdn,
                                shift=nb, tm=rs, alpha=alpha, name="out_ffn_sample")
        ss.append((by_batch(kvf[:, :kvw]).reshape(nb, tq, 4, N_KV_HEADS, HEAD_DIM),
                   jnp.transpose(wout_t, (0, 2, 1)).reshape(nb, WINDOW, 2, N_KV_HEADS, HEAD_DIM),
                   jnp.transpose(cst.reshape(2, nb, -1), (1, 0, 2)),
                   jnp.transpose(fst.reshape(2, nb, -1), (1, 0, 2)),
                   by_batch(vrow)))

    ys = jnp.transpose(xs.reshape(tq, nb, d_model), (1, 0, 2))
    return (xp[None], ys,
            jnp.stack([s[0] for s in ps]), jnp.stack([s[1] for s in ps]),
            jnp.stack([s[2] for s in ps]), jnp.stack([s[3] for s in ps]),
            jnp.stack([s[0] for s in ss]), jnp.stack([s[1] for s in ss]),
            jnp.stack([s[2] for s in ss]), jnp.stack([s[3] for s in ss]),
            jnp.stack([s[4] for s in ss]))
```

```python
import functools
import math

import numpy as np
import jax
import jax.numpy as jnp
from jax import lax
from jax.experimental import pallas as pl
from jax.experimental.pallas import tpu as pltpu

F32 = jnp.float32
BF16 = jnp.bfloat16

HEAD_DIM = 64
N_HEADS = 8
N_KV_HEADS = 2
GQA = N_HEADS // N_KV_HEADS
N_BRANCH = 3
CONV_K = 3
CMP_LEN = 32
CMP_STRIDE = 16
CMP_HID = 128
SLC_BLOCK = 64
N_SELECT = 16
WINDOW = 512
Q_BLOCK = 128
PAGE_SIZE = 128
GMLP_CHUNK = 128
GMLP_GROUPS = 4
LN_EPS = 1e-5
NEG_INF = -1e30
FORCE = 1e4
REMOVED = -3e38
LOG2E = 1.4426950408889634

LANES = 128
KEY_TILE = 512
BLOCKS_PER_TILE = KEY_TILE // SLC_BLOCK
NS_ROWS = 16
TILE_UNROLL = 4
SEQS_PER_SELECT_STEP = 8
CAUSAL_VARIANTS = 4
VMEM_LIMIT = 56 * 1024 * 1024


def _dot(a, b):
    return jnp.dot(a, b, preferred_element_type=F32)


def _dot_nt(a, b):
    return lax.dot_general(a, b, (((1,), (1,)), ((), ())), preferred_element_type=F32)


def _split(a):
    hi = a.astype(BF16)
    lo = (a - hi.astype(F32)).astype(BF16)
    return hi, lo


def _dot3(a, b):
    ah, al = _split(a)
    bh, bl = _split(b)
    return _dot(ah, bh) + _dot(ah, bl) + _dot(al, bh)


def _sigmoid(x):
    return 1.0 / (1.0 + jnp.exp(-x))


def _gelu(x):
    c = math.sqrt(2.0 / math.pi)
    return 0.5 * x * (1.0 + jnp.tanh(c * (x + 0.044715 * (x * x * x))))


def _layer_norm(x, g, b):
    mu = jnp.mean(x, axis=-1, keepdims=True)
    xc = x - mu
    var = jnp.mean(xc * xc, axis=-1, keepdims=True)
    return xc * lax.rsqrt(var + LN_EPS) * g + b


def _masked_softmax(s, mask, axis):
    sm = jnp.where(mask, s, NEG_INF)
    mx = jnp.max(sm, axis=axis, keepdims=True)
    e = jnp.where(mask, jnp.exp2(s - mx), 0.0)
    l = jnp.sum(e, axis=axis, keepdims=True)
    return e * (1.0 / jnp.where(l > 0.0, l, 1.0))


def _softmax_rows(sm, col_valid):
    e = jnp.exp2(sm - jnp.max(sm, axis=0, keepdims=True))
    inv = 1.0 / jnp.sum(e, axis=0, keepdims=True)
    if col_valid is not None:
        inv = jnp.where(col_valid, inv, 0.0)
    return e * inv


def _top_select(val, blk_f, n_blk, axis):
    sel = jnp.zeros_like(val)
    firsts, tops = [], []
    for _ in range(N_SELECT):
        mx = jnp.max(val, axis=axis, keepdims=True)
        first = jnp.min(jnp.where(val == mx, blk_f, float(n_blk)), axis=axis, keepdims=True)
        hit = blk_f == first
        sel = jnp.where(hit, 1.0, sel)
        val = jnp.where(hit, REMOVED, val)
        firsts.append(first)
        tops.append(mx)
    return sel, firsts, tops


def _shifted_conv(src_ref, x, w_ref, b_ref, pad, shift, rows):
    x2 = src_ref[pl.ds(pad - 2 * shift, rows), :]
    x1 = src_ref[pl.ds(pad - shift, rows), :]
    return w_ref[0:1, :] * x2 + w_ref[1:2, :] * x1 + w_ref[2:3, :] * x + b_ref[...]


def _ada_kernel(c_ref, w_ref, b_ref, o_ref):
    c = c_ref[...]
    o_ref[0] = _dot3(c * _sigmoid(c), w_ref[0]) + b_ref[0]


def _ada_call(c_all, w_ada, b_ada):
    depth, d_model, n_mod = w_ada.shape
    rc = c_all.shape[0]
    tn = 1024
    return pl.pallas_call(
        _ada_kernel,
        grid=(depth, n_mod // tn),
        in_specs=[pl.BlockSpec((rc, d_model), lambda l, n: (0, 0)),
                  pl.BlockSpec((1, d_model, tn), lambda l, n: (l, 0, n)),
                  pl.BlockSpec((1, 1, tn), lambda l, n: (l, 0, n))],
        out_specs=pl.BlockSpec((1, rc, tn), lambda l, n: (l, 0, n)),
        out_shape=jax.ShapeDtypeStruct((depth, rc, n_mod), F32),
        compiler_params=pltpu.CompilerParams(dimension_semantics=("arbitrary", "arbitrary"),
                                             vmem_limit_bytes=VMEM_LIMIT),
        name="ada_mod",
    )(c_all, w_ada, b_ada.reshape(depth, 1, n_mod))


_C_AB, _C_AC, _C_AH = 0, 256, 512
_C_Q = 768
_C_KV = 1792
_C_GATE = 2560
_C_GU = 2688
_C_GV = 2944
_N_COL = 3200
_N_SLOT = 6


def _in_proj_kernel(x_ref, sc_ref, sh_ref, w_ref, cw_ref, cb_ref, cpast_ref, lng_ref, lnb_ref, wm_ref, sb_ref,
                    *rest, shift, tm, pad, transposed):
    if transposed:
        (oa_ref, oc_ref, cstate_ref, qt_ref, kvp_ref, kvt_ref, kcr_ref, vcr_ref, ksl_ref, kwn_ref, vslt_ref, vwnt_ref,
         gatet_ref, zs_ref) = rest
    else:
        (oa_ref, oc_ref, cstate_ref, q_ref, kvf_ref, ksl_ref, vsl_ref, gate_ref, vrow_ref, zs_ref) = rest
    i = pl.program_id(0)

    @pl.when(i == 0)
    def _():
        zs_ref[pl.ds(pad - 2 * shift, 2 * shift), :] = cpast_ref[...]

    h = (x_ref[...] * (1.0 + sc_ref[...]) + sh_ref[...]).astype(BF16)
    p = _dot(h, w_ref[...])

    z = p[:, _C_AC:_C_AC + 256] * p[:, _C_AH:_C_AH + 256]
    zs_ref[pl.ds(pad, tm), :] = z
    y = _shifted_conv(zs_ref, z, cw_ref, cb_ref, pad, shift, tm)
    oa_ref[...] = (p[:, _C_AB:_C_AB + 256] * y).astype(BF16)
    tail = zs_ref[pl.ds(pad + tm - 2 * shift, 2 * shift), :]
    cstate_ref[...] = tail
    zs_ref[pl.ds(pad - 2 * shift, 2 * shift), :] = tail

    kv = [p[:, _C_KV + k * LANES:_C_KV + (k + 1) * LANES] for k in range(_N_SLOT)]
    gate = _sigmoid(p[:, _C_GATE:_C_GATE + LANES])
    if transposed:
        for hh in range(N_HEADS):
            qt_ref[pl.ds(hh * LANES, LANES), :] = p[:, _C_Q + hh * LANES:_C_Q + (hh + 1) * LANES].T.astype(BF16)
        kvt = [a.T for a in kv]
        for k in range(4):
            for pg in range(tm // PAGE_SIZE):
                kvp_ref[pg, pl.ds(k * LANES, LANES), :] = kvt[k][:, pg * PAGE_SIZE:(pg + 1) * PAGE_SIZE]
        for k in range(4, _N_SLOT):
            kvt_ref[pl.ds((k - 4) * LANES, LANES), :] = kvt[k]
        kcr_ref[...] = kv[0].astype(BF16)
        vcr_ref[...] = kv[1].astype(BF16)
        ksl_ref[...] = kv[2].astype(BF16)
        kwn_ref[...] = kv[4].astype(BF16)
        vslt_ref[...] = kvt[3].astype(BF16)
        vwnt_ref[...] = kvt[5].astype(BF16)
        gatet_ref[...] = gate.T
    else:
        q_ref[...] = p[:, _C_Q:_C_Q + 1024].astype(BF16)
        kvf_ref[...] = p[:, _C_KV:_C_KV + _N_SLOT * LANES]
        ksl_ref[...] = kv[2].astype(BF16)
        vsl_ref[...] = kv[3].astype(BF16)
        gate_ref[...] = gate

    u = _gelu(p[:, _C_GU:_C_GU + 256])
    v = _layer_norm(_gelu(p[:, _C_GV:_C_GV + 256]), lng_ref[...], lnb_ref[...])
    if not transposed:
        vrow_ref[...] = v
    lane = lax.broadcasted_iota(jnp.int32, (1, 256), 1)
    for c in range(tm // GMLP_CHUNK):
        vc = v[c * GMLP_CHUNK:(c + 1) * GMLP_CHUNK]
        mixed = sb_ref[...]
        for g in range(GMLP_GROUPS):
            vg = jnp.where((lane >= g * HEAD_DIM) & (lane < (g + 1) * HEAD_DIM), vc, 0.0).astype(BF16)
            mixed = mixed + _dot(wm_ref[g], vg)
        oc_ref[pl.ds(c * GMLP_CHUNK, GMLP_CHUNK), :] = (u[c * GMLP_CHUNK:(c + 1) * GMLP_CHUNK] * mixed).astype(BF16)


def _in_proj_call(x, sc, sh, w_all, cw, cb, cpast, lng, lnb, wm, sb, *, shift, tm, transposed, name):
    rows, d_model = x.shape
    pad = max(8, 2 * shift)
    mr = sc.shape[0]
    mod_spec = (pl.BlockSpec((1, d_model), lambda i: (0, 0)) if mr == 1
                else pl.BlockSpec((tm, d_model), lambda i: (i, 0)))

    def row_spec(n):
        return pl.BlockSpec((tm, n), lambda i: (i, 0))

    def col_spec(n):
        return pl.BlockSpec((n, tm), lambda i: (0, i))

    def full(a):
        nd = a.ndim
        return pl.BlockSpec(a.shape, lambda i: (0,) * nd)

    sds = jax.ShapeDtypeStruct
    out_shape = [sds((rows, 256), BF16), sds((rows, 256), BF16), sds((2 * shift, 256), F32)]
    out_specs = [row_spec(256), row_spec(256), pl.BlockSpec((2 * shift, 256), lambda i: (0, 0))]
    if transposed:
        out_shape += [sds((N_HEADS * LANES, rows), BF16), sds((rows // PAGE_SIZE, 4 * LANES, PAGE_SIZE), F32),
                      sds((2 * LANES, rows), F32)]
        out_specs += [col_spec(N_HEADS * LANES),
                      pl.BlockSpec((tm // PAGE_SIZE, 4 * LANES, PAGE_SIZE), lambda i: (i, 0, 0)), col_spec(2 * LANES)]
        out_shape += [sds((rows, LANES), BF16)] * 4 + [sds((LANES, rows), BF16)] * 2 + [sds((LANES, rows), F32)]
        out_specs += [row_spec(LANES)] * 4 + [col_spec(LANES)] * 3
    else:
        out_shape += [sds((rows, 1024), BF16), sds((rows, _N_SLOT * LANES), F32), sds((rows, LANES), BF16),
                      sds((rows, LANES), BF16), sds((rows, LANES), F32), sds((rows, 256), F32)]
        out_specs += [row_spec(1024), row_spec(_N_SLOT * LANES), row_spec(LANES), row_spec(LANES), row_spec(LANES),
                      row_spec(256)]
    return pl.pallas_call(
        functools.partial(_in_proj_kernel, shift=shift, tm=tm, pad=pad, transposed=transposed),
        grid=(rows // tm,),
        in_specs=[row_spec(d_model), mod_spec, mod_spec, full(w_all), full(cw), full(cb), full(cpast),
                  full(lng), full(lnb), full(wm), full(sb)],
        out_specs=out_specs,
        out_shape=out_shape,
        scratch_shapes=[pltpu.VMEM((pad + tm, 256), F32)],
        compiler_params=pltpu.CompilerParams(dimension_semantics=("arbitrary",), vmem_limit_bytes=VMEM_LIMIT),
        name=name,
    )(x, sc, sh, w_all, cw, cb, cpast, lng, lnb, wm, sb)


def _pe_term_kernel(pe_ref, w1_ref, o_ref):
    for s in range(2):
        o_ref[s] = _dot3(pe_ref[s], w1_ref[s])


def _pe_term_call(pe_flat, w1_flat):
    return pl.pallas_call(
        _pe_term_kernel,
        out_shape=jax.ShapeDtypeStruct((2, 8, CMP_HID), F32),
        compiler_params=pltpu.CompilerParams(vmem_limit_bytes=VMEM_LIMIT),
        name="cmp_pe_term",
    )(pe_flat, w1_flat)


def _compress_prompt_kernel(kx_ref, vx_ref, w1_ref, pet_ref, w2_ref, kc_ref, vct_ref, sh_ref, *, nc):
    sh_ref[pl.ds(0, 8), :] = jnp.zeros((8, CMP_HID), F32)
    for s, x_ref in enumerate((kx_ref, vx_ref)):
        parts = _dot(x_ref[...], w1_ref[s])
        acc = jnp.zeros((nc, LANES), F32)
        for g in range(N_KV_HEADS):
            p0 = parts[:, g * 256:g * 256 + CMP_HID]
            p1 = parts[:, g * 256 + CMP_HID:(g + 1) * 256]
            sh_ref[pl.ds(8, nc), :] = p0
            p0s = sh_ref[pl.ds(7, nc), :]
            hid = _gelu(p0s + p1 + pet_ref[s][0:1, :])
            acc = acc + _dot(hid.astype(BF16), w2_ref[s, g])
        if s == 0:
            kc_ref[...] = acc.astype(BF16)
        else:
            vct_ref[...] = acc.T.astype(BF16)


def _compress_prompt_call(kx, vx, w1p, pet, w2p):
    nc = kx.shape[0]
    return pl.pallas_call(
        functools.partial(_compress_prompt_kernel, nc=nc),
        out_shape=[jax.ShapeDtypeStruct((nc, LANES), BF16), jax.ShapeDtypeStruct((LANES, nc), BF16)],
        scratch_shapes=[pltpu.VMEM((nc + 8, CMP_HID), F32)],
        compiler_params=pltpu.CompilerParams(vmem_limit_bytes=VMEM_LIMIT),
        name="compress_prompt",
    )(kx, vx, w1p, pet, w2p)


def _attn_prompt_kernel(qt_ref, gatet_ref, kc_ref, vct_ref, covert_ref, ksl_ref, vslt_ref, kwn_ref, vwnt_ref, eb_ref,
                        o_ref, ns_ref, s_ref, m_ref, ocmp_ref, *, nc, ns):
    qb = pl.program_id(0)
    s0 = qb * Q_BLOCK
    ncol = GQA * Q_BLOCK
    pair = 2 * Q_BLOCK
    col = lax.broadcasted_iota(jnp.int32, (1, ncol), 1)
    qpos_c = s0 + (col & (Q_BLOCK - 1))
    qp = s0 + lax.broadcasted_iota(jnp.int32, (1, Q_BLOCK), 1)
    m_idx = lax.broadcasted_iota(jnp.int32, (nc, 1), 0)
    blk = lax.broadcasted_iota(jnp.int32, (ns, 1), 0)
    blk_f = blk.astype(F32)
    key_row = lax.broadcasted_iota(jnp.int32, (KEY_TILE, 1), 0)
    n_tiles = s0 // KEY_TILE + 1

    ns_ref[pl.ds(ns, 8), :] = jnp.zeros((8, Q_BLOCK), F32)
    rhs_zero = jnp.zeros((LANES - NS_ROWS, ncol), BF16)

    for g in (pl.program_id(1),):
        rq = jnp.concatenate([qt_ref[pl.ds(h * LANES, LANES), :] for h in range(GQA)], axis=1)

        vrows = pl.ds(pl.multiple_of(g * HEAD_DIM, HEAD_DIM), HEAD_DIM)

        def compressed_and_select(frac):
            nr, nsr = nc * frac // CAUSAL_VARIANTS, ns * frac // CAUSAL_VARIANTS

            def run():
                m_i = m_idx[0:nr]
                last_pos = jnp.where(m_i >= 1, CMP_STRIDE * (m_i - 1) + CMP_LEN - 1, 2 ** 30)
                sc = jnp.where(last_pos <= qpos_c, _dot(kc_ref[pl.ds(0, nr), :], rq), NEG_INF)
                p = _softmax_rows(sc, qpos_c >= CMP_LEN - 1)
                o_c = _dot(vct_ref[vrows, pl.ds(0, nr)], p.astype(BF16))
                p4 = (p[:, 0:Q_BLOCK] + p[:, Q_BLOCK:2 * Q_BLOCK] + p[:, 2 * Q_BLOCK:3 * Q_BLOCK]
                      + p[:, 3 * Q_BLOCK:4 * Q_BLOCK])
                hi, lo = _split(p4)
                cov = covert_ref[pl.ds(0, nsr), pl.ds(0, nr)]
                imp = _dot(cov, hi) + _dot(cov, lo)
                b_i, b_f = blk[0:nsr], blk_f[0:nsr]
                qblk = qp // SLC_BLOCK
                elig = b_i * SLC_BLOCK <= qp
                forced = (b_i == 0) | (b_i == qblk) | (b_i == qblk - 1)
                val = jnp.where(elig, jnp.where(forced, REMOVED, imp), -FORCE)
                for _ in range(N_SELECT - 3):
                    mx = jnp.max(val, axis=0, keepdims=True)
                    first = jnp.min(jnp.where(val == mx, b_f, float(ns)), axis=0, keepdims=True)
                    val = jnp.where(b_f == first, REMOVED, val)
                ns_ref[pl.ds(0, nsr), :] = jnp.where(elig & (val < 0.5 * REMOVED), 0.0, 1.0)
                if nsr < ns:
                    ns_ref[pl.ds(nsr, ns - nsr), :] = jnp.ones((ns - nsr, Q_BLOCK), F32)
                ocmp_ref[...] = o_c
            return run

        variant = (qb * CAUSAL_VARIANTS) // (nc * CMP_STRIDE // Q_BLOCK)
        for f in range(CAUSAL_VARIANTS):
            pl.when(variant == f)(compressed_and_select(f + 1))
        o_cmp = ocmp_ref[...]

        def tile_scores(kt):
            k0 = pl.multiple_of(kt * KEY_TILE, KEY_TILE)
            nsf = ns_ref[pl.ds(pl.multiple_of(kt * BLOCKS_PER_TILE, BLOCKS_PER_TILE), NS_ROWS), :].astype(BF16)
            rhs = jnp.concatenate([rq, jnp.concatenate([nsf] * GQA, axis=1), rhs_zero], axis=0)
            lhs = jnp.concatenate([ksl_ref[pl.ds(k0, KEY_TILE), :], eb_ref[...]], axis=1)
            return _dot(lhs, rhs)

        ones_rows = jnp.ones((NS_ROWS, KEY_TILE), BF16)

        def flash_update(s, s_max, k0, carry):
            m_run, acc = carry
            m_new = jnp.maximum(m_run, s_max)
            pe = jnp.exp2(s - m_new).astype(BF16)
            v_aug = jnp.concatenate([vslt_ref[vrows, pl.ds(k0, KEY_TILE)], ones_rows], axis=0)
            return m_new, jnp.exp2(m_run - m_new) * acc + _dot(v_aug, pe)

        def produce(dst_ref, dmx_ref, kt):
            s = tile_scores(kt)
            dst_ref[...] = s
            dmx_ref[...] = jnp.max(s, axis=0, keepdims=True)

        def update_from(src_ref, smx_ref, kt, carry):
            k0 = pl.multiple_of(kt * KEY_TILE, KEY_TILE)
            return tuple(flash_update(src_ref[:, pl.ds(hp * pair, pair)], smx_ref[:, pl.ds(hp * pair, pair)], k0,
                                      carry[hp]) for hp in range(GQA // 2))

        sa_ref, ma_ref = s_ref.at[0], m_ref.at[0]
        produce(sa_ref, ma_ref, 0)

        def chain(kt, n, carry):
            for i in range(n):
                produce(s_ref.at[(i + 1) % n], m_ref.at[(i + 1) % n], kt + i + 1)
                carry = update_from(s_ref.at[i], m_ref.at[i], kt + i, carry)
            return carry

        init = (jnp.full((1, pair), NEG_INF, F32), jnp.zeros((HEAD_DIM + NS_ROWS, pair), F32))
        last = n_tiles - 1
        quads = last // TILE_UNROLL
        carry = lax.fori_loop(0, quads, lambda j, c: chain(TILE_UNROLL * j, TILE_UNROLL, c), (init,) * (GQA // 2))
        done = TILE_UNROLL * quads
        n = TILE_UNROLL // 2
        while n >= 2:
            take = last - done >= n
            carry = lax.cond(take, lambda c, done=done, n=n: chain(done, n, c), lambda c: c, carry)
            done = done + jnp.where(take, n, 0)
            n //= 2

        def odd_step(carry):
            carry = update_from(sa_ref, ma_ref, last - 1, carry)
            produce(sa_ref, ma_ref, last)
            return carry

        carry = lax.cond(lax.rem(last, 2) == 1, odd_step, lambda c: c, carry)
        k0 = pl.multiple_of(last * KEY_TILE, KEY_TILE)
        o_slc_parts = []
        for hp in range(GQA // 2):
            s = sa_ref[:, pl.ds(hp * pair, pair)]
            s = jnp.where(k0 + key_row <= qpos_c[:, hp * pair:(hp + 1) * pair], s, NEG_INF)
            _, acc = flash_update(s, jnp.max(s, axis=0, keepdims=True), k0, carry[hp])
            o_slc_parts.append(acc[0:HEAD_DIM] * (1.0 / acc[HEAD_DIM:HEAD_DIM + 1]))
        o_slc = jnp.concatenate(o_slc_parts, axis=1)

        w0 = pl.multiple_of(jnp.maximum(s0 - WINDOW, 0), Q_BLOCK)
        dist = qp - (w0 + lax.broadcasted_iota(jnp.int32, (WINDOW + Q_BLOCK, 1), 0))
        wbias = jnp.where(lax.bitcast_convert_type(dist, jnp.uint32) <= WINDOW, 0.0, NEG_INF)
        sw = _dot(kwn_ref[pl.ds(w0, WINDOW + Q_BLOCK), :], rq) + jnp.concatenate([wbias] * GQA, axis=1)
        pw = _softmax_rows(sw, None)
        o_win = _dot(vwnt_ref[vrows, pl.ds(w0, WINDOW + Q_BLOCK)], pw.astype(BF16))

        outs = []
        for h in range(GQA):
            c = slice(h * Q_BLOCK, (h + 1) * Q_BLOCK)
            gr = (GQA * g + h) * N_BRANCH
            outs.append(gatet_ref[pl.ds(gr, 1), :] * o_cmp[:, c] + gatet_ref[pl.ds(gr + 1, 1), :] * o_slc[:, c]
                        + gatet_ref[pl.ds(gr + 2, 1), :] * o_win[:, c])
        for hp in range(GQA // 2):
            o2 = jnp.concatenate(outs[2 * hp:2 * hp + 2], axis=0)
            o_ref[:, hp * LANES:(hp + 1) * LANES] = o2.T.astype(BF16)


def _attn_prompt_call(qt, gatet, kc, vct, covert, ksl, vslt, kwn, vwnt, ebias):
    t = qt.shape[1]
    ns, nc = covert.shape
    vmem = pl.BlockSpec(memory_space=pltpu.VMEM)
    return pl.pallas_call(
        functools.partial(_attn_prompt_kernel, nc=nc, ns=ns),
        grid=(t // Q_BLOCK, N_KV_HEADS),
        in_specs=[pl.BlockSpec((GQA * LANES, Q_BLOCK), lambda i, g: (g, i)),
                  pl.BlockSpec((LANES, Q_BLOCK), lambda i, g: (0, i)),
                  vmem, vmem, vmem, vmem, vmem, vmem, vmem, vmem],
        out_specs=pl.BlockSpec((Q_BLOCK, GQA * HEAD_DIM), lambda i, g: (i, g)),
        out_shape=jax.ShapeDtypeStruct((t, N_HEADS * HEAD_DIM), BF16),
        scratch_shapes=[pltpu.VMEM((ns + 8, Q_BLOCK), F32),
                        pltpu.VMEM((TILE_UNROLL, KEY_TILE, GQA * Q_BLOCK), F32),
                        pltpu.VMEM((TILE_UNROLL, 1, GQA * Q_BLOCK), F32),
                        pltpu.VMEM((HEAD_DIM, GQA * Q_BLOCK), F32)],
        compiler_params=pltpu.CompilerParams(dimension_semantics=("arbitrary", "arbitrary"),
                                             vmem_limit_bytes=VMEM_LIMIT),
        name="attn_prompt",
    )(qt, gatet, kc, vct, covert, ksl, vslt, kwn, vwnt, ebias)


def _out_ffn_kernel(x_ref, oa_ref, ob_ref, oc_ref, g1_ref, sc2_ref, sh2_ref, g2_ref, wo_ref, lng_ref, lnb_ref,
                    wup_ref, cfw_ref, cfb_ref, fpast_ref, wdn_ref, xo_ref, fstate_ref, us_ref,
                    *, shift, tm, pad, alpha, d_ff):
    i = pl.program_id(0)

    @pl.when(i == 0)
    def _():
        us_ref[pl.ds(pad - 2 * shift, 2 * shift), :] = fpast_ref[...]

    mix = _dot(jnp.concatenate([oa_ref[...], ob_ref[...], oc_ref[...]], axis=1), wo_ref[...])
    x1 = _layer_norm(alpha * x_ref[...] + (1.0 + g1_ref[...]) * mix, lng_ref[0:1, :], lnb_ref[0:1, :])
    h2 = (x1 * (1.0 + sc2_ref[...]) + sh2_ref[...]).astype(BF16)
    up = _dot(h2, wup_ref[...])
    ua = up[:, :d_ff]
    us_ref[pl.ds(pad, tm), :] = ua
    yc = _shifted_conv(us_ref, ua, cfw_ref, cfb_ref, pad, shift, tm)
    tail = us_ref[pl.ds(pad + tm - 2 * shift, 2 * shift), :]
    fstate_ref[...] = tail
    us_ref[pl.ds(pad - 2 * shift, 2 * shift), :] = tail
    act = (yc * _sigmoid(yc) * up[:, d_ff:]).astype(BF16)
    y = _dot(act, wdn_ref[...])
    xo_ref[...] = _layer_norm(alpha * x1 + (1.0 + g2_ref[...]) * y, lng_ref[1:2, :], lnb_ref[1:2, :])


def _out_ffn_call(x, oa, ob, oc, g1, sc2, sh2, g2, wo, lng, lnb, wup, cfw, cfb, fpast, wdn, *, shift, tm, alpha, name):
    rows, d_model = x.shape
    d_ff = wdn.shape[0]
    pad = max(8, 2 * shift)
    mr = g1.shape[0]
    mod_spec = (pl.BlockSpec((1, d_model), lambda i: (0, 0)) if mr == 1
                else pl.BlockSpec((tm, d_model), lambda i: (i, 0)))

    def row_spec(n):
        return pl.BlockSpec((tm, n), lambda i: (i, 0))

    vmem = pl.BlockSpec(memory_space=pltpu.VMEM)
    return pl.pallas_call(
        functools.partial(_out_ffn_kernel, shift=shift, tm=tm, pad=pad, alpha=alpha, d_ff=d_ff),
        grid=(rows // tm,),
        in_specs=[row_spec(d_model), row_spec(256), row_spec(N_HEADS * HEAD_DIM), row_spec(256), mod_spec, mod_spec, mod_spec,
                  mod_spec, vmem, vmem, vmem, vmem, vmem, vmem, vmem, vmem],
        out_specs=[row_spec(d_model), pl.BlockSpec((2 * shift, d_ff), lambda i: (0, 0))],
        out_shape=[jax.ShapeDtypeStruct((rows, d_model), F32), jax.ShapeDtypeStruct((2 * shift, d_ff), F32)],
        scratch_shapes=[pltpu.VMEM((pad + tm, d_ff), F32)],
        compiler_params=pltpu.CompilerParams(dimension_semantics=("arbitrary",), vmem_limit_bytes=VMEM_LIMIT),
        name=name,
    )(x, oa, ob, oc, g1, sc2, sh2, g2, wo, lng, lnb, wup, cfw, cfb, fpast, wdn)


def _cmp_stream_kernel(pt_ref, cache_ref, perm_ref, w1_ref, pet_ref, w2_ref, kc_ref, vc_ref, buf, rbuf, sem, carry,
                       *, page_base, n_pages, pg, n_groups, total):
    b = pl.program_id(0)
    gi = pl.program_id(1)
    step = b * n_groups + gi
    slot = lax.rem(step, 2)
    m = pg * (PAGE_SIZE // CMP_STRIDE)

    def page_copy(page, slt, i):
        return pltpu.make_async_copy(cache_ref.at[page, pl.ds(0, 2 * LANES), :], buf.at[slt, i], sem.at[slt])

    def issue(stp, slt):
        base = lax.div(stp, n_groups) * n_pages + lax.rem(stp, n_groups) * pg
        for i in range(pg):
            page_copy(page_base + pt_ref[base + i], slt, i).start()

    @pl.when(step == 0)
    def _():
        carry[...] = jnp.zeros(carry.shape, F32)
        issue(step, slot)

    @pl.when(step + 1 < total)
    def _():
        issue(step + 1, 1 - slot)

    for i in range(pg):
        page_copy(0, slot, i).wait()

    per_chunk = PAGE_SIZE // CMP_STRIDE
    for i in range(pg):
        rbuf[i] = _dot_nt(perm_ref[...], buf[slot, i].astype(BF16))

    def rows_of(j, s):
        return jnp.concatenate([rbuf[i, pl.ds(j * per_chunk, per_chunk), pl.ds(s * LANES, LANES)]
                                for i in range(pg)], axis=0)

    lane = lax.broadcasted_iota(jnp.int32, (1, LANES), 1)
    first_half = lane < HEAD_DIM
    row0 = lax.broadcasted_iota(jnp.int32, (m, 1), 0) == 0
    pieces = [[[], []], [[], []]]
    for pr in range(CMP_STRIDE // 2):
        for s in range(2):
            a = rows_of(2 * pr, s)
            bb = rows_of(2 * pr + 1, s)
            pieces[s][0].append(jnp.where(first_half, a, pltpu.roll(bb, HEAD_DIM, 1)).astype(BF16))
            pieces[s][1].append(jnp.where(first_half, pltpu.roll(a, HEAD_DIM, 1), bb).astype(BF16))
    for s, o_ref in enumerate((kc_ref, vc_ref)):
        acc = jnp.zeros((m, LANES), F32)
        for g in range(N_KV_HEADS):
            parts = _dot(jnp.concatenate(pieces[s][g], axis=1), w1_ref[s])
            p0 = parts[:, :CMP_HID]
            p1 = parts[:, CMP_HID:]
            prev = jnp.where(gi == 0, 0.0, carry[s * 2 + g][0:1, :])
            p0s = jnp.where(row0, prev, pltpu.roll(p0, 1, 0))
            carry[s * 2 + g] = jnp.broadcast_to(p0[m - 1:m, :], (8, CMP_HID))
            hid = _gelu(p0s + p1 + pet_ref[s][0:1, :])
            acc = acc + _dot(hid.astype(BF16), w2_ref[s, g])
        o_ref[0] = acc.astype(BF16)


def _cmp_stream_call(pt_flat, cache_t, w1s, pet, w2p, *, layer, n_phys, batch, n_pages, pg):
    n_groups = n_pages // pg
    m = pg * (PAGE_SIZE // CMP_STRIDE)
    nc = n_pages * (PAGE_SIZE // CMP_STRIDE)
    total = batch * n_groups
    out_row = np.arange(PAGE_SIZE)[:, None]
    src_row = (out_row % (PAGE_SIZE // CMP_STRIDE)) * CMP_STRIDE + out_row // (PAGE_SIZE // CMP_STRIDE)
    perm = jnp.asarray((np.arange(PAGE_SIZE)[None, :] == src_row).astype(np.float32), dtype=BF16)

    def full(a):
        nd = a.ndim
        return pl.BlockSpec(a.shape, lambda b, g, pt: (0,) * nd)

    grid_spec = pltpu.PrefetchScalarGridSpec(
        num_scalar_prefetch=1,
        grid=(batch, n_groups),
        in_specs=[pl.BlockSpec(memory_space=pl.ANY), full(perm), full(w1s), full(pet), full(w2p)],
        out_specs=[pl.BlockSpec((1, m, LANES), lambda b, g, pt: (b, g, 0))] * 2,
        scratch_shapes=[pltpu.VMEM((2, pg, 2 * LANES, PAGE_SIZE), F32), pltpu.VMEM((pg, PAGE_SIZE, 2 * LANES), F32),
                        pltpu.SemaphoreType.DMA((2,)), pltpu.VMEM((4, 8, CMP_HID), F32)],
    )
    return pl.pallas_call(
        functools.partial(_cmp_stream_kernel, page_base=layer * n_phys, n_pages=n_pages, pg=pg,
                          n_groups=n_groups, total=total),
        grid_spec=grid_spec,
        out_shape=[jax.ShapeDtypeStruct((batch, nc, LANES), BF16)] * 2,
        compiler_params=pltpu.CompilerParams(dimension_semantics=("arbitrary", "arbitrary"),
                                             vmem_limit_bytes=VMEM_LIMIT),
        name="cmp_stream_sample",
    )(pt_flat, cache_t, perm, w1s, pet, w2p)


def _cmp_attn_sample_kernel(q_ref, kc_ref, vc_ref, cover_ref, ocmp_ref, idx_ref, bias_ref,
                            *, past, nc, ns, nsp, n_cache_blocks, bb):
    rows = lax.broadcasted_iota(jnp.int32, (32, 1), 0)
    qpos_r = past + (rows & 7)
    m_idx = lax.broadcasted_iota(jnp.int32, (1, nc), 1)
    blk = lax.broadcasted_iota(jnp.int32, (1, nsp), 1)
    blk_f = blk.astype(F32)
    lane = lax.broadcasted_iota(jnp.int32, (1, LANES), 1)
    cmask = (m_idx >= 1) & (CMP_STRIDE * (m_idx - 1) + CMP_LEN - 1 <= qpos_r)
    imps = []
    for bi in range(bb):
        for g in range(N_KV_HEADS):
            p = _masked_softmax(_dot_nt(q_ref[bi, g], kc_ref[bi]), cmask, -1)
            ocmp_ref[bi, g] = _dot(p.astype(BF16), vc_ref[bi])
            p4 = p[0:8] + p[8:16] + p[16:24] + p[24:32]
            hi, lo = _split(p4)
            imps.append(_dot(hi, cover_ref[...]) + _dot(lo, cover_ref[...]))
    imp = jnp.concatenate(imps, axis=0)
    qp = past + (lax.broadcasted_iota(jnp.int32, (bb * N_KV_HEADS * 8, 1), 0) & 7)
    qblk = qp // SLC_BLOCK
    elig = blk * SLC_BLOCK <= qp
    forced = (blk == 0) | (blk == qblk) | (blk == qblk - 1)
    val = jnp.where(blk < ns, jnp.where(elig, jnp.where(forced, FORCE, imp), -FORCE), REMOVED)
    _, firsts, tops = _top_select(val, blk_f, nsp, -1)
    idx = jnp.zeros((bb * N_KV_HEADS * 8, LANES), F32)
    bias = jnp.zeros((bb * N_KV_HEADS * 8, LANES), F32)
    for t in range(N_SELECT):
        ok = (tops[t] > -0.5 * FORCE) & (firsts[t] < float(n_cache_blocks))
        odd = firsts[t] - 2.0 * jnp.floor(firsts[t] * 0.5)
        idx = jnp.where(lane == t, firsts[t], idx)
        for hf in range(2):
            bias = jnp.where(lane == 2 * t + hf, jnp.where(ok & (odd == float(hf)), 0.0, NEG_INF), bias)
    idx = idx.astype(jnp.int32)
    for bi in range(bb):
        for g in range(N_KV_HEADS):
            r = (bi * N_KV_HEADS + g) * 8
            idx_ref[bi, g] = idx[r:r + 8]
            bias_ref[bi, g] = bias[r:r + 8]


def _cmp_attn_sample_call(q_hq, kc, vc, cover, *, past, ns, n_cache_blocks):
    batch, nc, _ = kc.shape
    nsp = cover.shape[1]
    bb = math.gcd(batch, SEQS_PER_SELECT_STEP)
    blk4 = lambda r: pl.BlockSpec((bb, N_KV_HEADS, r, LANES), lambda b: (b, 0, 0, 0))
    return pl.pallas_call(
        functools.partial(_cmp_attn_sample_kernel, past=past, nc=nc, ns=ns, nsp=nsp, n_cache_blocks=n_cache_blocks,
                          bb=bb),
        grid=(batch // bb,),
        in_specs=[blk4(32), pl.BlockSpec((bb, nc, LANES), lambda b: (b, 0, 0)),
                  pl.BlockSpec((bb, nc, LANES), lambda b: (b, 0, 0)), pl.BlockSpec(cover.shape, lambda b: (0, 0))],
        out_specs=[blk4(32), blk4(8), blk4(8)],
        out_shape=[jax.ShapeDtypeStruct((batch, N_KV_HEADS, 32, LANES), F32),
                   jax.ShapeDtypeStruct((batch, N_KV_HEADS, 8, LANES), jnp.int32),
                   jax.ShapeDtypeStruct((batch, N_KV_HEADS, 8, LANES), F32)],
        compiler_params=pltpu.CompilerParams(dimension_semantics=("arbitrary",), vmem_limit_bytes=VMEM_LIMIT),
        name="cmp_attn_sample",
    )(q_hq, kc, vc, cover)


def _sel_attn_sample_kernel(idx_ref, pt_ref, cache_ref, q_ref, bias_ref, ex_ref, knew_ref, vnew_ref, wint_ref,
                            wnew_ref, wnewt_ref, ocmp_ref, gate_ref, o_ref, wout_ref, buf, sem,
                            *, page_base, n_pages, tq, n_cache_blocks, total):
    b = pl.program_id(0)
    g = pl.program_id(1)
    step = b * N_KV_HEADS + g
    slot = lax.rem(step, 2)
    n_ent = tq * N_SELECT
    blocks_per_page = PAGE_SIZE // SLC_BLOCK

    def page_copy(page, grp, slt, e):
        rows = pl.ds(pl.multiple_of(grp * HEAD_DIM, HEAD_DIM), HEAD_DIM)
        return pltpu.make_async_copy(cache_ref.at[page, pl.ds(2, 2), grp], buf.at[slt, e, :, rows, :], sem.at[slt])

    def issue(stp, slt):
        bb = lax.div(stp, N_KV_HEADS)
        grp = lax.rem(stp, N_KV_HEADS)

        def body(i, _):
            for prio in range(2):
                e = 2 * i + prio
                blk = jnp.minimum(idx_ref[stp * n_ent + e], n_cache_blocks - 1)
                page = pt_ref[bb * n_pages + lax.div(blk, blocks_per_page)]
                page_copy(page_base + page, grp, slt, e).start(priority=prio)
            return 0
        lax.fori_loop(0, n_ent // 2, body, 0)

    @pl.when(step == 0)
    def _():
        buf[...] = jnp.zeros(buf.shape, F32)
        issue(step, slot)

    @pl.when(step + 1 < total)
    def _():
        issue(step + 1, 1 - slot)

    def wait_body(e, _):
        page_copy(0, g, slot, e).wait()
        return 0
    lax.fori_loop(0, n_ent, wait_body, 0)

    lane = lax.broadcasted_iota(jnp.int32, (1, LANES), 1)
    col8 = lax.broadcasted_iota(jnp.int32, (1, 8), 1)
    wcol = lax.broadcasted_iota(jnp.int32, (1, WINDOW), 1)
    k_wt = wint_ref[0, pl.ds(0, LANES), :].astype(BF16)
    v_wt = wint_ref[0, pl.ds(LANES, LANES), :].astype(BF16)
    wnew = wnew_ref[0]
    k_wn = wnew[:, 0:LANES].astype(BF16)
    v_wn = wnew[:, LANES:2 * LANES].astype(BF16)
    bias_all = _dot(bias_ref[0, 0].astype(BF16), ex_ref[...])
    keep = (lane >= g * HEAD_DIM) & (lane < (g + 1) * HEAD_DIM)
    for qi in range(tq):
        q4 = q_ref[0, 0, qi]
        new_ok = (col8 <= qi) & (col8 < tq)
        k_t = jnp.concatenate([buf[slot, qi * N_SELECT + k, 0] for k in range(N_SELECT)], axis=1)
        v_t = jnp.concatenate([buf[slot, qi * N_SELECT + k, 1] for k in range(N_SELECT)], axis=1)
        s = _dot(q4, k_t.astype(BF16)) + bias_all[qi:qi + 1, :]
        s_n = jnp.where(new_ok, _dot_nt(q4, knew_ref[0]), NEG_INF)
        mx = jnp.maximum(jnp.max(s, axis=-1, keepdims=True), jnp.max(s_n, axis=-1, keepdims=True))
        pe = jnp.exp2(s - mx)
        pn = jnp.exp2(s_n - mx)
        l = jnp.sum(pe, axis=-1, keepdims=True) + jnp.sum(pn, axis=-1, keepdims=True)
        o_slc = (_dot_nt(pe.astype(BF16), v_t.astype(BF16)) + _dot(pn.astype(BF16), vnew_ref[0])) * (1.0 / l)
        sw = jnp.where(wcol >= qi, _dot(q4, k_wt), NEG_INF)
        sw_n = jnp.where(new_ok, _dot_nt(q4, k_wn), NEG_INF)
        mw = jnp.maximum(jnp.max(sw, axis=-1, keepdims=True), jnp.max(sw_n, axis=-1, keepdims=True))
        pw = jnp.exp2(sw - mw)
        pwn = jnp.exp2(sw_n - mw)
        lw = jnp.sum(pw, axis=-1, keepdims=True) + jnp.sum(pwn, axis=-1, keepdims=True)
        o_win = (_dot_nt(pw.astype(BF16), v_wt) + _dot(pwn.astype(BF16), v_wn)) * (1.0 / lw)
        gt = gate_ref[0, 0, qi]
        o = gt[:, 0:1] * ocmp_ref[0, 0, qi] + gt[:, 1:2] * o_slc + gt[:, 2:3] * o_win
        o_ref[0, 0, qi] = jnp.where(keep, o, 0.0)

    @pl.when(g == 0)
    def _():
        shifted = pltpu.roll(wint_ref[0], WINDOW - tq, 1)
        wout_ref[0, :, pl.ds(0, WINDOW - LANES)] = shifted[:, 0:WINDOW - LANES]
        wout_ref[0, :, pl.ds(WINDOW - LANES, LANES)] = jnp.where(lane >= LANES - tq, wnewt_ref[0],
                                                                  shifted[:, WINDOW - LANES:WINDOW])


def _sel_attn_sample_call(idx_flat, pt_flat, cache_t, q_qh, bias, expand, knew, vnew, win_t, wnew, wnew_t, ocmp_qh,
                          gate_qh, *, layer, n_phys, n_pages, tq, n_cache_blocks):
    batch = q_qh.shape[0]
    n_ent = tq * N_SELECT
    b5 = lambda: pl.BlockSpec((1, 1, tq, 8, LANES), lambda b, g, i, p: (b, g, 0, 0, 0))
    b3 = lambda r, c: pl.BlockSpec((1, r, c), lambda b, g, i, p: (b, 0, 0))
    grid_spec = pltpu.PrefetchScalarGridSpec(
        num_scalar_prefetch=2,
        grid=(batch, N_KV_HEADS),
        in_specs=[pl.BlockSpec(memory_space=pl.ANY), b5(),
                  pl.BlockSpec((1, 1, 8, LANES), lambda b, g, i, p: (b, g, 0, 0)),
                  pl.BlockSpec(expand.shape, lambda b, g, i, p: (0, 0)),
                  b3(8, LANES), b3(8, LANES),
                  pl.BlockSpec((1, 2 * LANES, WINDOW), lambda b, g, i, p: (layer * batch + b, 0, 0)),
                  b3(8, 2 * LANES), b3(2 * LANES, LANES),
                  b5(), b5()],
        out_specs=[b5(), b3(2 * LANES, WINDOW)],
        scratch_shapes=[pltpu.VMEM((2, n_ent, 2, LANES, PAGE_SIZE), F32), pltpu.SemaphoreType.DMA((2,))],
    )
    return pl.pallas_call(
        functools.partial(_sel_attn_sample_kernel, page_base=layer * n_phys, n_pages=n_pages, tq=tq,
                          n_cache_blocks=n_cache_blocks, total=batch * N_KV_HEADS),
        grid_spec=grid_spec,
        out_shape=[jax.ShapeDtypeStruct((batch, N_KV_HEADS, tq, 8, LANES), F32),
                   jax.ShapeDtypeStruct((batch, 2 * LANES, WINDOW), F32)],
        compiler_params=pltpu.CompilerParams(dimension_semantics=("arbitrary", "arbitrary"),
                                             vmem_limit_bytes=VMEM_LIMIT),
        name="sel_attn_sample",
    )(idx_flat, pt_flat, cache_t, q_qh, bias, expand, knew, vnew, win_t, wnew, wnew_t, ocmp_qh, gate_qh)


def _prep_w_in(w):
    d = w.shape[0]
    wq = (w[:, 768:1280] * (HEAD_DIM ** -0.5 * LOG2E)).reshape(d, N_HEADS, HEAD_DIM)
    z = jnp.zeros_like(wq)
    grp = (jnp.arange(N_HEADS) // GQA)[None, :, None]
    wq = jnp.concatenate([jnp.where(grp == 0, wq, z), jnp.where(grp == 1, wq, z)], axis=-1).reshape(d, N_HEADS * LANES)
    gate = jnp.pad(w[:, 2048:2072], ((0, 0), (0, LANES - N_HEADS * N_BRANCH)))
    return jnp.concatenate([w[:, :768], wq, w[:, 1280:2048], gate, w[:, 2072:]], axis=1).astype(BF16)


def _prep_w1(w1):
    w = w1.reshape(2, 2, CMP_STRIDE, HEAD_DIM, CMP_HID)
    return jnp.transpose(w, (0, 2, 3, 1, 4)).reshape(2, CMP_STRIDE * HEAD_DIM, 2 * CMP_HID)


def _prep_w1_grouped(w1s):
    w = w1s.reshape(2, CMP_STRIDE, 1, HEAD_DIM, 1, 2 * CMP_HID)
    eye = jnp.eye(N_KV_HEADS, dtype=w.dtype).reshape(1, 1, N_KV_HEADS, 1, N_KV_HEADS, 1)
    return (w * eye).reshape(2, CMP_STRIDE * N_KV_HEADS * HEAD_DIM, N_KV_HEADS * 2 * CMP_HID)


def _prep_w2(w2):
    z = jnp.zeros_like(w2)
    return jnp.stack([jnp.concatenate([w2, z], axis=-1), jnp.concatenate([z, w2], axis=-1)], axis=1).astype(BF16)


def _cover_matrix(nc, ns_real, ns_pad):
    m = np.arange(nc)[:, None]
    b = np.arange(ns_pad)[None, :]
    return ((m >= 4 * b) & (m <= 4 * b + 4) & (m >= 1) & (b < ns_real)).astype(np.float32)


def _block_bias_matrix():
    k = np.arange(KEY_TILE)[:, None]
    b = np.arange(LANES)[None, :]
    return jnp.asarray(np.where(k // SLC_BLOCK == b, NEG_INF, 0.0).astype(np.float32), dtype=BF16)


def _expand_matrix():
    r = np.arange(LANES)[:, None]
    c = np.arange(N_SELECT * PAGE_SIZE)[None, :]
    return jnp.asarray((c // SLC_BLOCK == r).astype(np.float32), dtype=BF16)


def kernel(x_prompt, x_sample, cache_nsa_kv, state_win_kv, state_conv, state_ffn_conv, page_table, c_prompt, c_sample, w_ada, b_ada, w_in, conv_a_w, conv_a_b, cmp_pe, cmp_w1, cmp_w2, sgu_ln_g, sgu_ln_b, sgu_w, sgu_b, w_o, ln_g, ln_b, w_ffn_up, conv_f_w, conv_f_b, w_ffn_down):
    depth = w_in.shape[0]
    _, t, d_model = x_prompt.shape
    nb, tq, _ = x_sample.shape
    n_phys = cache_nsa_kv.shape[1]
    n_pages = page_table.shape[1]
    past = n_pages * PAGE_SIZE
    d_ff = w_ffn_down.shape[1]
    alpha = (2 * depth) ** 0.25
    rs = nb * tq
    kvw = 4 * N_KV_HEADS * HEAD_DIM
    assert x_prompt.shape[0] == 1 and c_prompt.shape[0] == 1
    assert d_model == 1024 and t % KEY_TILE == 0 and t >= WINDOW + Q_BLOCK
    assert tq == 4 and rs == GMLP_CHUNK and past % KEY_TILE == 0 and past >= WINDOW
    assert state_win_kv.shape[2] == WINDOW

    rc = -(-(1 + nb) // 8) * 8
    c_all = jnp.pad(jnp.concatenate([c_prompt, c_sample], axis=0), ((0, rc - 1 - nb), (0, 0)))
    mods = _ada_call(c_all, w_ada, b_ada)

    nc_p, ns_p = t // CMP_STRIDE, t // SLC_BLOCK
    covert_p = jnp.asarray(_cover_matrix(nc_p, ns_p, ns_p).T, dtype=BF16)
    nc_s = (past + tq) // CMP_STRIDE
    ns_s = -(-(past + tq) // SLC_BLOCK)
    ns_s_pad = -(-ns_s // LANES) * LANES
    n_cache_blocks = past // SLC_BLOCK
    cover_s = jnp.asarray(_cover_matrix(nc_s, ns_s, ns_s_pad), dtype=BF16)
    ebias = _block_bias_matrix()
    expand = _expand_matrix()
    pt_flat = page_table.reshape(-1)
    cache_t = jnp.transpose(cache_nsa_kv.reshape(depth * n_phys, PAGE_SIZE, kvw), (0, 2, 1))
    win_t_all = jnp.transpose(state_win_kv.reshape(depth * nb, WINDOW, 2 * LANES), (0, 2, 1))

    xp = x_prompt[0]
    xs = jnp.transpose(x_sample, (1, 0, 2)).reshape(rs, d_model)
    tril_full = jnp.tril(jnp.ones((GMLP_CHUNK, GMLP_CHUNK), F32))
    tril_tq = jnp.tril(jnp.ones((tq, tq), F32))

    ps, ss = [], []
    for l in range(depth):
        w_all = _prep_w_in(w_in[l])
        wo_p = w_o[l].astype(BF16)
        wup = w_ffn_up[l].astype(BF16)
        wdn = w_ffn_down[l].astype(BF16)
        w1s = _prep_w1(cmp_w1[l])
        w1g = _prep_w1_grouped(w1s).astype(BF16)
        w1s = w1s.astype(BF16)
        w2p = _prep_w2(cmp_w2[l])
        pe_flat = jnp.broadcast_to(cmp_pe[l].reshape(2, 1, CMP_LEN * HEAD_DIM), (2, 8, CMP_LEN * HEAD_DIM))
        pet = _pe_term_call(pe_flat, cmp_w1[l].reshape(2, CMP_LEN * HEAD_DIM, CMP_HID))
        cw, cb = conv_a_w[l], conv_a_b[l].reshape(1, -1)
        cfw, cfb = conv_f_w[l], conv_f_b[l].reshape(1, -1)
        lng, lnb = sgu_ln_g[l].reshape(1, -1), sgu_ln_b[l].reshape(1, -1)

        def mod_rows(r0, r1, rep):
            parts = [mods[l, r0:r1, k * d_model:(k + 1) * d_model] for k in range(6)]
            return [jnp.tile(p_, (rep, 1)) if rep > 1 else p_ for p_ in parts]

        sh1, sc1, g1, sh2, sc2, g2 = mod_rows(0, 1, 1)
        wm_p = (sgu_w[l] * tril_full).astype(BF16)
        sb_p = jnp.repeat(sgu_b[l].T, HEAD_DIM, axis=1)
        (oa, oc, cst, qt, kvp, kvt, kcr, vcr, ksl, kwn, vslt, vwnt, gatet) = _in_proj_call(
            xp, sc1, sh1, w_all, cw, cb, jnp.zeros((2, 256), F32), lng, lnb, wm_p, sb_p,
            shift=1, tm=512, transposed=True, name="in_proj_prompt")
        kc, vct = _compress_prompt_call(kcr.reshape(nc_p, CMP_STRIDE * LANES), vcr.reshape(nc_p, CMP_STRIDE * LANES),
                                        w1g, pet, w2p)
        ob = _attn_prompt_call(qt, gatet, kc, vct, covert_p, ksl, vslt, kwn, vwnt, ebias)
        xp, fst = _out_ffn_call(xp, oa, ob, oc, g1, sc2, sh2, g2, wo_p, ln_g[l], ln_b[l], wup, cfw, cfb,
                                jnp.zeros((2, d_ff), F32), wdn, shift=1, tm=256, alpha=alpha, name="out_ffn_prompt")
        paged = kvp.reshape(t // PAGE_SIZE, 4, N_KV_HEADS, HEAD_DIM, PAGE_SIZE)
        winr = kvt[:, t - WINDOW:].reshape(2, N_KV_HEADS, HEAD_DIM, WINDOW)
        ps.append((jnp.transpose(paged, (0, 4, 1, 2, 3))[None], jnp.transpose(winr, (3, 0, 1, 2))[None],
                   cst[None], fst[None]))

        sh1, sc1, g1, sh2, sc2, g2 = mod_rows(1, 1 + nb, tq)
        eye_b = jnp.eye(nb, dtype=F32)
        wm_s = jax.vmap(lambda w: jnp.kron(w[:tq, :tq] * tril_tq, eye_b))(sgu_w[l]).astype(BF16)
        sb_s = jnp.repeat(jnp.repeat(sgu_b[l][:, :tq].T, nb, axis=0), HEAD_DIM, axis=1)
        cpast = jnp.transpose(state_conv[l], (1, 0, 2)).reshape(2 * nb, -1)
        (oa, oc, cst, q, kvf, ksl, vsl, gate, vrow) = _in_proj_call(
            xs, sc1, sh1, w_all, cw, cb, cpast, lng, lnb, wm_s, sb_s, shift=nb, tm=rs, transposed=False,
            name="in_proj_sample")
        kc, vc = _cmp_stream_call(pt_flat, cache_t, w1s, pet, w2p, layer=l, n_phys=n_phys, batch=nb,
                                  n_pages=n_pages, pg=min(32, n_pages))

        def by_batch(a):
            return jnp.transpose(a.reshape(tq, nb, -1), (1, 0, 2))

        qb5 = by_batch(q).reshape(nb, tq, N_KV_HEADS, GQA, LANES)
        q_hq = jnp.pad(jnp.transpose(qb5, (0, 2, 3, 1, 4)), ((0, 0), (0, 0), (0, 0), (0, 8 - tq), (0, 0)))
        q_hq = q_hq.reshape(nb, N_KV_HEADS, 32, LANES)
        q_qh = jnp.pad(jnp.transpose(qb5, (0, 2, 1, 3, 4)), ((0, 0), (0, 0), (0, 0), (0, 8 - GQA), (0, 0)))
        ocmp, idx, bias = _cmp_attn_sample_call(q_hq, kc, vc, cover_s, past=past, ns=ns_s,
                                                n_cache_blocks=n_cache_blocks)
        ocmp_qh = jnp.transpose(ocmp.reshape(nb, N_KV_HEADS, GQA, 8, LANES)[:, :, :, :tq], (0, 1, 3, 2, 4))
        ocmp_qh = jnp.pad(ocmp_qh, ((0, 0), (0, 0), (0, 0), (0, 8 - GQA), (0, 0)))
        g5 = by_batch(gate)[:, :, :N_HEADS * N_BRANCH].reshape(nb, tq, N_KV_HEADS, GQA, N_BRANCH)
        gate_qh = jnp.pad(jnp.transpose(g5, (0, 2, 1, 3, 4)),
                          ((0, 0), (0, 0), (0, 0), (0, 8 - GQA), (0, LANES - N_BRANCH)))
        pad8 = lambda a: jnp.pad(by_batch(a), ((0, 0), (0, 8 - tq), (0, 0)))
        wnew_rows = by_batch(kvf[:, kvw:])
        wnew_t = jnp.pad(jnp.transpose(wnew_rows, (0, 2, 1)), ((0, 0), (0, 0), (LANES - tq, 0)))
        o5, wout_t = _sel_attn_sample_call(
            idx[:, :, :tq, :N_SELECT].reshape(-1), pt_flat,
            cache_t.reshape(depth * n_phys, 4, N_KV_HEADS, HEAD_DIM, PAGE_SIZE), q_qh, bias, expand, pad8(ksl), pad8(vsl),
            win_t_all, jnp.pad(wnew_rows, ((0, 0), (0, 8 - tq), (0, 0))), wnew_t, ocmp_qh, gate_qh,
            layer=l, n_phys=n_phys, n_pages=n_pages, tq=tq, n_cache_blocks=n_cache_blocks)
        o5 = o5[:, :, :, :GQA, :HEAD_DIM] + o5[:, :, :, :GQA, HEAD_DIM:]
        ob = jnp.transpose(o5, (2, 0, 1, 3, 4)).reshape(rs, N_HEADS * HEAD_DIM).astype(BF16)
        fpast = jnp.transpose(state_ffn_conv[l], (1, 0, 2)).reshape(2 * nb, -1)
        xs, fst = _out_ffn_call(xs, oa, ob, oc, g1, sc2, sh2, g2, wo_p, ln_g[l], ln_b[l], wup, cfw, cfb, fpast, wdn,
                                shift=nb, tm=rs, alpha=alpha, name="out_ffn_sample")
        ss.append((by_batch(kvf[:, :kvw]).reshape(nb, tq, 4, N_KV_HEADS, HEAD_DIM),
                   jnp.transpose(wout_t, (0, 2, 1)).reshape(nb, WINDOW, 2, N_KV_HEADS, HEAD_DIM),
                   jnp.transpose(cst.reshape(2, nb, -1), (1, 0, 2)),
                   jnp.transpose(fst.reshape(2, nb, -1), (1, 0, 2)),
                   by_batch(vrow)))

    ys = jnp.transpose(xs.reshape(tq, nb, d_model), (1, 0, 2))
    return (xp[None], ys,
            jnp.stack([s[0] for s in ps]), jnp.stack([s[1] for s in ps]),
            jnp.stack([s[2] for s in ps]), jnp.stack([s[3] for s in ps]),
            jnp.stack([s[0] for s in ss]), jnp.stack([s[1] for s in ss]),
            jnp.stack([s[2] for s in ss]), jnp.stack([s[3] for s in ss]),
            jnp.stack([s[4] for s in ss]))
```
